```python
import math
import jax
import jax.numpy as jnp
from jax import lax
import numpy as np

D_MODEL = 1024
BATCH = 16
SEQ = 2048
DEPTH = 1

GRID_W = 64
CTX_LEN = 256

D_RNN = 1024
LRU_BLOCKS = 16
LRU_BLOCK_W = D_RNN // LRU_BLOCKS
LRU_C = 8.0
CONV_W = 4

SSD_INNER = 2 * D_MODEL
SSD_HEAD_DIM = 64
SSD_HEADS = SSD_INNER // SSD_HEAD_DIM
SSD_GROUPS = 8
HEADS_PER_GROUP = SSD_HEADS // SSD_GROUPS
SSD_STATE = 128
SSD_CHUNK = 128
SSD_CONV_DIM = SSD_INNER + 2 * SSD_GROUPS * SSD_STATE

COL_XR = 0
COL_GR = COL_XR + D_RNN
COL_Z = COL_GR + D_RNN
COL_XBC = COL_Z + SSD_INNER
COL_DT = COL_XBC + SSD_CONV_DIM
COL_GM = COL_DT + 2 * SSD_HEADS
IN_COLS = COL_GM + 2 * D_MODEL

MOE_EXPERTS = 32
MOE_TOP_K = 4
MOE_FF = D_MODEL
SWIGLU_LIMIT = 7.0
SWIGLU_ALPHA = 1.702

LN_EPS = 1e-5
RMS_EPS = 1e-5

kernel_name = 'hybrid_rglru_ssd_moe_deepnorm_prefix'

F32 = jnp.float32


def _layer_norm(x, g=None, b=None):
    xf = x.astype(F32)
    mu = xf.mean(-1, keepdims=True)
    var = jnp.square(xf - mu).mean(-1, keepdims=True)
    y = (xf - mu) * lax.rsqrt(var + LN_EPS)
    if g is not None:
        y = y * g.astype(F32) + b.astype(F32)
    return y.astype(x.dtype)


def _modulate(x, shift, scale):
    return (_layer_norm(x) * (1.0 + scale) + shift).astype(x.dtype)


def _flip(u):
    return jnp.flip(u, axis=1)


def _dw_conv(u, w, b):
    k = w.shape[0]
    left = k // 2
    t = u.shape[1]
    up = jnp.pad(u, ((0, 0), (left, k - 1 - left), (0, 0)))
    return sum(up[:, j:j + t] * w[j] for j in range(k)) + b


def _lin_combine(e1, e2):
    a1, b1 = e1
    a2, b2 = e2
    return a1 * a2, a2 * b1 + b2


def _rglru(u, h0, wa, ba, wx, bx, lam):
    bsz, t, _ = u.shape
    ub = u.astype(F32).reshape(bsz, t, LRU_BLOCKS, LRU_BLOCK_W)
    r = jax.nn.sigmoid(jnp.einsum('btnd,nde->btne', ub, wa.astype(F32)) + ba.astype(F32))
    i = jax.nn.sigmoid(jnp.einsum('btnd,nde->btne', ub, wx.astype(F32)) + bx.astype(F32))
    log_a = -LRU_C * r * jax.nn.softplus(-lam.astype(F32).reshape(LRU_BLOCKS, LRU_BLOCK_W))
    a = jnp.exp(log_a).reshape(bsz, t, D_RNN)
    bterm = (jnp.sqrt(-jnp.expm1(2.0 * log_a)) * (i * ub)).reshape(bsz, t, D_RNN)
    bterm = bterm.at[:, 0].add(a[:, 0] * h0)
    _, h = lax.associative_scan(_lin_combine, (a, bterm), axis=1)
    return h, h[:, -1]


def _ssd_inputs(xbc_raw, dt_raw, conv_w, conv_b, dt_bias):
    xbc = jax.nn.silu(_dw_conv(xbc_raw, conv_w, conv_b)).astype(F32)
    bsz, t, _ = xbc.shape
    gn = SSD_GROUPS * SSD_STATE
    xs = xbc[..., :SSD_INNER].reshape(bsz, t, SSD_HEADS, SSD_HEAD_DIM)
    bm = xbc[..., SSD_INNER:SSD_INNER + gn].reshape(bsz, t, SSD_GROUPS, SSD_STATE)
    cm = xbc[..., SSD_INNER + gn:].reshape(bsz, t, SSD_GROUPS, SSD_STATE)
    dt = jax.nn.softplus(dt_raw.astype(F32).reshape(bsz, t, 2, SSD_HEADS) + dt_bias.astype(F32))
    return xs, bm, cm, dt


def _ssd_chunked(xs, dt, bm, cm, h0, a, with_y):
    bsz, t = xs.shape[:2]
    nc = t // SSD_CHUNK
    x = xs.reshape(bsz, nc, SSD_CHUNK, SSD_GROUPS, HEADS_PER_GROUP, SSD_HEAD_DIM)
    dtc = dt.reshape(bsz, nc, SSD_CHUNK, SSD_GROUPS, HEADS_PER_GROUP)
    bc = bm.reshape(bsz, nc, SSD_CHUNK, SSD_GROUPS, SSD_STATE)
    cc = cm.reshape(bsz, nc, SSD_CHUNK, SSD_GROUPS, SSD_STATE)
    acum = jnp.cumsum(dtc * a.reshape(SSD_GROUPS, HEADS_PER_GROUP), axis=2)
    a_last = acum[:, :, -1]
    w_state = jnp.exp(a_last[:, :, None] - acum) * dtc
    states = jnp.einsum('bcqgn,bcqgr,bcqgrp->bcgrpn', bc, w_state, x)

    def step(h, inp):
        s, dec = inp
        return dec[..., None, None] * h + s, h

    h_last, h_in = lax.scan(step, h0, (jnp.moveaxis(states, 1, 0), jnp.moveaxis(jnp.exp(a_last), 1, 0)))
    if not with_y:
        return None, h_last
    h_in = jnp.moveaxis(h_in, 0, 1)
    y_off = jnp.einsum('bcqgn,bcgrpn,bcqgr->bcqgrp', cc, h_in, jnp.exp(acum))
    seg = acum[:, :, :, None] - acum[:, :, None, :]
    lower = jnp.tril(jnp.ones((SSD_CHUNK, SSD_CHUNK), dtype=bool))[:, :, None, None]
    decay = jnp.exp(jnp.where(lower, seg, -jnp.inf))
    cb = jnp.einsum('bcqgn,bckgn->bcqkg', cc, bc)
    y_diag = jnp.einsum('bcqkg,bcqkgr,bckgr,bckgrp->bcqgrp', cb, decay, dtc, x)
    y = (y_diag + y_off).reshape(bsz, t, SSD_HEADS, SSD_HEAD_DIM)
    return y, h_last


def _ssd_skip(y, xs, d_skip):
    bsz, t = y.shape[:2]
    return (y + d_skip.astype(F32)[:, None] * xs).reshape(bsz, t, SSD_INNER)


def _ssd_gate_norm(inner, z, norm_w):
    bsz, t, _ = inner.shape
    y = (inner * jax.nn.silu(z.astype(F32))).reshape(bsz, t, SSD_GROUPS, SSD_INNER // SSD_GROUPS)
    y = y * lax.rsqrt(jnp.mean(jnp.square(y), axis=-1, keepdims=True) + RMS_EPS)
    return y.reshape(bsz, t, SSD_INNER) * norm_w.astype(F32)


def _merge(y_lru, y_ssd, gm_raw, b_merge, w_br_lru, w_br_ssd, w_out):
    g = jax.nn.sigmoid(gm_raw.astype(F32) + b_merge.astype(F32))
    u = g[..., :D_MODEL] * (y_lru @ w_br_lru.astype(F32)) + g[..., D_MODEL:] * (y_ssd @ w_br_ssd.astype(F32))
    return u @ w_out.astype(F32)


def _token_mixer(h_lat, h_ctx, rows, need_ctx, w_in, b_merge, conv_lru_w, conv_lru_b, lru_wa, lru_ba, lru_wx,
                 lru_bx, lru_lambda, conv_ssd_w, conv_ssd_b, ssd_dt_bias, ssd_a_log, ssd_d, ssd_norm_w,
                 w_br_lru, w_br_ssd, w_out):
    bsz, t_lat, _ = h_lat.shape

    def cols(h, lo, hi):
        return h @ w_in[:, lo:hi]

    def to_cols(u):
        return u.reshape(bsz, rows, GRID_W, -1).transpose(0, 2, 1, 3).reshape(bsz, t_lat, -1)

    def from_cols(u):
        return u.reshape(bsz, GRID_W, rows, -1).transpose(0, 2, 1, 3).reshape(bsz, t_lat, -1)

    u_c = _dw_conv(cols(h_ctx, COL_XR, COL_GR), conv_lru_w, conv_lru_b)
    u_l = _dw_conv(cols(h_lat, COL_XR, COL_GR), conv_lru_w, conv_lru_b)
    h0 = jnp.zeros((bsz, D_RNN), F32)
    p_f = (lru_wa[0], lru_ba[0], lru_wx[0], lru_bx[0], lru_lambda[0])
    p_b = (lru_wa[1], lru_ba[1], lru_wx[1], lru_bx[1], lru_lambda[1])
    rc_f, rs_f = _rglru(u_c, h0, *p_f)
    rl_f, _ = _rglru(u_l, rs_f, *p_f)
    rc_b, rs_b = _rglru(_flip(u_c), h0, *p_b)
    rl_b, _ = _rglru(_flip(u_l), rs_b, *p_b)
    a_lat = (rl_f + _flip(rl_b)) * jax.nn.gelu(cols(h_lat, COL_GR, COL_Z).astype(F32))

    xs_c, bm_c, cm_c, dt_c = _ssd_inputs(cols(h_ctx, COL_XBC, COL_DT), cols(h_ctx, COL_DT, COL_GM),
                                         conv_ssd_w, conv_ssd_b, ssd_dt_bias)
    xs_l, bm_l, cm_l, dt_l = _ssd_inputs(to_cols(cols(h_lat, COL_XBC, COL_DT)), to_cols(cols(h_lat, COL_DT, COL_GM)),
                                         conv_ssd_w, conv_ssd_b, ssd_dt_bias)
    a = -jnp.exp(ssd_a_log.astype(F32))
    s0 = jnp.zeros((bsz, SSD_GROUPS, HEADS_PER_GROUP, SSD_HEAD_DIM, SSD_STATE), F32)
    yc_f, ss_f = _ssd_chunked(xs_c, dt_c[:, :, 0], bm_c, cm_c, s0, a[0], need_ctx)
    yl_f, _ = _ssd_chunked(xs_l, dt_l[:, :, 0], bm_l, cm_l, ss_f, a[0], True)
    yc_b, ss_b = _ssd_chunked(_flip(xs_c), _flip(dt_c[:, :, 1]), _flip(bm_c), _flip(cm_c), s0, a[1], need_ctx)
    yl_b, _ = _ssd_chunked(_flip(xs_l), _flip(dt_l[:, :, 1]), _flip(bm_l), _flip(cm_l), ss_b, a[1], True)
    b_lat = _ssd_gate_norm(from_cols(_ssd_skip(yl_f + _flip(yl_b), xs_l, ssd_d)),
                           cols(h_lat, COL_Z, COL_XBC), ssd_norm_w)

    out_lat = _merge(a_lat, b_lat, cols(h_lat, COL_GM, IN_COLS), b_merge, w_br_lru, w_br_ssd, w_out)
    out_ctx = None
    if need_ctx:
        a_ctx = (rc_f + _flip(rc_b)) * jax.nn.gelu(cols(h_ctx, COL_GR, COL_Z).astype(F32))
        b_ctx = _ssd_gate_norm(_ssd_skip(yc_f + _flip(yc_b), xs_c, ssd_d), cols(h_ctx, COL_Z, COL_XBC), ssd_norm_w)
        out_ctx = _merge(a_ctx, b_ctx, cols(h_ctx, COL_GM, IN_COLS), b_merge, w_br_lru, w_br_ssd,
                         w_out).astype(h_ctx.dtype)
    return out_lat.astype(h_lat.dtype), out_ctx


def _moe(h, router_w, router_b, w1, b1, w2, b2):
    bsz, t, d = h.shape
    tok = h.reshape(bsz * t, d)
    logits = (tok @ router_w + router_b).astype(F32)
    top_v, top_i = lax.top_k(logits, MOE_TOP_K)
    wts = jax.nn.softmax(top_v, axis=-1)
    gates = jnp.sum(jax.nn.one_hot(top_i, MOE_EXPERTS, dtype=F32) * wts[..., None], axis=1)
    out = jnp.zeros((bsz * t, d), F32)
    for e in range(MOE_EXPERTS):
        gu = (tok @ w1[e] + b1[e]).astype(F32)
        g = jnp.minimum(gu[:, :MOE_FF], SWIGLU_LIMIT)
        u = jnp.clip(gu[:, MOE_FF:], -SWIGLU_LIMIT, SWIGLU_LIMIT)
        y = ((u + 1.0) * g * jax.nn.sigmoid(SWIGLU_ALPHA * g)) @ w2[e].astype(F32) + b2[e].astype(F32)
        out = out + gates[:, e:e + 1] * y
    return out.reshape(bsz, t, d).astype(h.dtype)


def setup_inputs(seed: int = 0) -> dict:
    key = jax.random.key(seed)
    ks = iter(jax.random.split(key, 48))

    def nrm(shape, scale):
        return jax.random.normal(next(ks), shape, F32) * scale

    def unif(shape, lo, hi):
        return jax.random.uniform(next(ks), shape, F32, lo, hi)

    L = DEPTH
    beta = (8.0 * DEPTH) ** -0.25
    a_c = unif((L, 2, D_RNN), 0.9, 0.999)
    a_base = a_c ** (1.0 / LRU_C)
    dt0 = jnp.exp(unif((L, 2, SSD_HEADS), math.log(1e-3), math.log(1e-1)))
    return {
        'x': nrm((BATCH, SEQ, D_MODEL), 1.0),
        'c': nrm((BATCH, D_MODEL), 1.0),
        'ctx': nrm((BATCH, CTX_LEN, D_MODEL), 1.0),
        'c_ctx': nrm((D_MODEL,), 1.0),
        'w_ada': nrm((L, D_MODEL, 6 * D_MODEL), D_MODEL ** -0.5),
        'b_ada': nrm((L, 6 * D_MODEL), 0.02),
        'w_in': nrm((L, D_MODEL, IN_COLS), D_MODEL ** -0.5),
        'b_merge': nrm((L, 2 * D_MODEL), 0.02),
        'conv_lru_w': nrm((L, CONV_W, D_RNN), CONV_W ** -0.5),
        'conv_lru_b': nrm((L, D_RNN), 0.02),
        'lru_wa': nrm((L, 2, LRU_BLOCKS, LRU_BLOCK_W, LRU_BLOCK_W), LRU_BLOCK_W ** -0.5),
        'lru_ba': nrm((L, 2, LRU_BLOCKS, LRU_BLOCK_W), 0.02),
        'lru_wx': nrm((L, 2, LRU_BLOCKS, LRU_BLOCK_W, LRU_BLOCK_W), LRU_BLOCK_W ** -0.5),
        'lru_bx': nrm((L, 2, LRU_BLOCKS, LRU_BLOCK_W), 0.02),
        'lru_lambda': jnp.log(a_base) - jnp.log1p(-a_base),
        'conv_ssd_w': nrm((L, CONV_W, SSD_CONV_DIM), CONV_W ** -0.5),
        'conv_ssd_b': nrm((L, SSD_CONV_DIM), 0.02),
        'ssd_dt_bias': dt0 + jnp.log(-jnp.expm1(-dt0)),
        'ssd_a_log': jnp.log(unif((L, 2, SSD_HEADS), 1.0, 16.0)),
        'ssd_d': 1.0 + nrm((L, SSD_HEADS), 0.1),
        'ssd_norm_w': 1.0 + nrm((L, SSD_INNER), 0.1),
        'w_br_lru': nrm((L, D_RNN, D_MODEL), D_RNN ** -0.5),
        'w_br_ssd': nrm((L, SSD_INNER, D_MODEL), SSD_INNER ** -0.5),
        'w_out': nrm((L, D_MODEL, D_MODEL), beta * D_MODEL ** -0.5),
        'ln1_g': 1.0 + nrm((L, D_MODEL), 0.05),
        'ln1_b': nrm((L, D_MODEL), 0.02),
        'router_w': nrm((L, D_MODEL, MOE_EXPERTS), D_MODEL ** -0.5),
        'router_b': nrm((L, MOE_EXPERTS), 0.01),
        'moe_w1': nrm((L, MOE_EXPERTS, D_MODEL, 2 * MOE_FF), D_MODEL ** -0.5),
        'moe_b1': nrm((L, MOE_EXPERTS, 2 * MOE_FF), 0.02),
        'moe_w2': nrm((L, MOE_EXPERTS, MOE_FF, D_MODEL), beta * MOE_FF ** -0.5),
        'moe_b2': nrm((L, MOE_EXPERTS, D_MODEL), 0.02),
        'ln2_g': 1.0 + nrm((L, D_MODEL), 0.05),
        'ln2_b': nrm((L, D_MODEL), 0.02),
    }


def reference(x, c, ctx, c_ctx, w_ada, b_ada, w_in, b_merge, conv_lru_w, conv_lru_b, lru_wa, lru_ba, lru_wx,
              lru_bx, lru_lambda, conv_ssd_w, conv_ssd_b, ssd_dt_bias, ssd_a_log, ssd_d, ssd_norm_w, w_br_lru,
              w_br_ssd, w_out, ln1_g, ln1_b, router_w, router_b, moe_w1, moe_b1, moe_w2, moe_b2, ln2_g, ln2_b):
    alpha = (2.0 * DEPTH) ** 0.25
    rows = x.shape[1] // GRID_W
    for l in range(DEPTH):
        need_ctx = l < DEPTH - 1
        mod_l = jax.nn.silu(c) @ w_ada[l] + b_ada[l]
        mod_c = jax.nn.silu(c_ctx) @ w_ada[l] + b_ada[l]
        sh1, sc1, g1, sh2, sc2, g2 = jnp.split(mod_l[:, None, :], 6, axis=-1)
        csh1, csc1, cg1, csh2, csc2, cg2 = jnp.split(mod_c, 6, axis=-1)
        mix_l, mix_c = _token_mixer(
            _modulate(x, sh1, sc1), _modulate(ctx, csh1, csc1), rows, need_ctx, w_in[l], b_merge[l],
            conv_lru_w[l], conv_lru_b[l], lru_wa[l], lru_ba[l], lru_wx[l], lru_bx[l], lru_lambda[l],
            conv_ssd_w[l], conv_ssd_b[l], ssd_dt_bias[l], ssd_a_log[l], ssd_d[l], ssd_norm_w[l],
            w_br_lru[l], w_br_ssd[l], w_out[l])
        x = _layer_norm(alpha * x + g1 * mix_l, ln1_g[l], ln1_b[l])
        x = _layer_norm(alpha * x + g2 * _moe(_modulate(x, sh2, sc2), router_w[l], router_b[l], moe_w1[l],
                                              moe_b1[l], moe_w2[l], moe_b2[l]), ln2_g[l], ln2_b[l])
        if need_ctx:
            ctx = _layer_norm(alpha * ctx + cg1 * mix_c, ln1_g[l], ln1_b[l])
            ctx = _layer_norm(alpha * ctx + cg2 * _moe(_modulate(ctx, csh2, csc2), router_w[l], router_b[l],
                                                       moe_w1[l], moe_b1[l], moe_w2[l], moe_b2[l]),
                              ln2_g[l], ln2_b[l])
    return x
```

```python
import functools

import jax
import jax.numpy as jnp
from jax import lax
from jax.experimental import pallas as pl
from jax.experimental.pallas import tpu as pltpu

F32 = jnp.float32
BF16 = jnp.bfloat16
HIGHEST = lax.Precision.HIGHEST

GRID_W = 64
LRU_BLOCK_W = 64
LRU_C = 8.0
CONV_W = 4
SSD_HEAD_DIM = 64
SSD_GROUPS = 8
SSD_STATE = 128
SSD_CHUNK = 128
MOE_TOP_K = 4
SWIGLU_LIMIT = 7.0
SWIGLU_ALPHA = 1.702
LN_EPS = 1e-5
RMS_EPS = 1e-5
LANES = 128
NEG_BIG = -1e30
VMEM_LIMIT = 56 * 1024 * 1024


def _cparams(*sem):
    return pltpu.CompilerParams(dimension_semantics=sem, vmem_limit_bytes=VMEM_LIMIT)


def _ln_rows(x):
    mu = jnp.mean(x, axis=-1, keepdims=True)
    xc = x - mu
    var = jnp.mean(xc * xc, axis=-1, keepdims=True)
    return xc * lax.rsqrt(var + LN_EPS)


def _sigmoid(x):
    return 1.0 / (1.0 + jnp.exp(-x))


def _silu(x):
    return x * _sigmoid(x)


def _softplus(x):
    return jnp.maximum(x, 0.0) + jnp.log(1.0 + jnp.exp(-jnp.abs(x)))


def _ada_kernel(c_ref, w_ref, b_ref, o_ref):
    c = c_ref[...]
    o_ref[...] = jnp.dot(_silu(c), w_ref[...], precision=HIGHEST, preferred_element_type=F32) + b_ref[...]


def _ada(c_all, w, b):
    m, d = c_all.shape
    n = w.shape[1]
    tn = 1024
    return pl.pallas_call(
        _ada_kernel,
        out_shape=jax.ShapeDtypeStruct((m, n), F32),
        grid=(n // tn,),
        in_specs=[pl.BlockSpec((m, d), lambda j: (0, 0)),
                  pl.BlockSpec((d, tn), lambda j: (0, j)),
                  pl.BlockSpec((1, tn), lambda j: (0, j))],
        out_specs=pl.BlockSpec((m, tn), lambda j: (0, j)),
        compiler_params=_cparams("arbitrary"),
        name="ada",
    )(c_all, w, b.reshape(1, n))


def _inproj_kernel(x_ref, sh_ref, sc_ref, w_ref, *rest, conv):
    if conv:
        cw_ref, cb_ref, o_ref, h_ref = rest
    else:
        o_ref, h_ref = rest
    rows = h_ref.shape[0]

    @pl.when(pl.program_id(1) == 0)
    def _():
        h = _ln_rows(x_ref[...]) * (1.0 + sc_ref[...]) + sh_ref[...]
        h_ref[...] = h.reshape(h_ref.shape).astype(BF16)

    acc = jnp.dot(h_ref[...], w_ref[...], preferred_element_type=F32)
    if conv:
        ridx = lax.broadcasted_iota(jnp.int32, acc.shape, 0)
        cw = cw_ref[...]
        out = acc * cw[2:3, :] + cb_ref[...]
        for j, shift in ((0, 2), (1, 1)):
            rolled = pltpu.roll(acc, shift, 0)
            out = out + jnp.where(ridx >= shift, rolled, 0.0) * cw[j:j + 1, :]
        rolled = pltpu.roll(acc, rows - 1, 0)
        out = out + jnp.where(ridx < rows - 1, rolled, 0.0) * cw[3:4, :]
        acc = _silu(out)
    o_ref[...] = acc.reshape(o_ref.shape).astype(o_ref.dtype)


def _inproj(x3, sh3, sc3, w, tile, tn, sel=None, conv_w=None, conv_b=None, out_dtype=F32, first=0, count=None):
    gdim, rdim, d = x3.shape
    g, r = tile
    n = w.shape[1]
    rows = g * r
    nblk_g = gdim // g if count is None else count
    nblk_r = rdim // r
    if sel is None:
        sel = lambda i: i
    conv = conv_w is not None
    mg, mr = sh3.shape[1], sh3.shape[2]
    in_specs = [pl.BlockSpec((g, r, d), lambda i, j: ((i // nblk_r) + first, i % nblk_r, 0)),
                pl.BlockSpec((1, mg, mr), lambda i, j: (sel((i // nblk_r) + first), 0, 0)),
                pl.BlockSpec((1, mg, mr), lambda i, j: (sel((i // nblk_r) + first), 0, 0)),
                pl.BlockSpec((d, tn), lambda i, j: (0, j))]
    args = [x3, sh3, sc3, w]
    if conv:
        assert r == rdim and g == 1
        in_specs += [pl.BlockSpec((CONV_W, tn), lambda i, j: (0, j)),
                     pl.BlockSpec((1, tn), lambda i, j: (0, j))]
        args += [conv_w, conv_b.reshape(1, n)]
    out_g = nblk_g * g
    return pl.pallas_call(
        functools.partial(_inproj_kernel, conv=conv),
        out_shape=jax.ShapeDtypeStruct((out_g, rdim, n), out_dtype),
        grid=(nblk_g * nblk_r, n // tn),
        in_specs=in_specs,
        out_specs=pl.BlockSpec((g, r, tn), lambda i, j: (i // nblk_r, i % nblk_r, j)),
        scratch_shapes=[pltpu.VMEM((rows, d), BF16)],
        compiler_params=_cparams("arbitrary", "arbitrary"),
        name="inproj_conv" if conv else "inproj",
    )(*args)


def _lru_gates(win_ref, w_ref, ba_ref, bx_ref, lam_ref, cw_ref, cb_ref, a_ref, b_ref, tt):
    bsz, c = win_ref.shape[1], win_ref.shape[2]
    cw = cw_ref[...]
    u = cb_ref[...].reshape(1, 1, c) + sum(win_ref[pl.ds(j, tt)] * cw[j:j + 1, :].reshape(1, 1, c)
                                           for j in range(CONV_W))
    u2 = u.reshape(tt * bsz, c)
    ub = u2.astype(BF16)
    sp = _softplus(-lam_ref[...])
    pack = w_ref.shape[1]
    for j in range(c // pack):
        lo = j * pack
        pre = jnp.dot(ub[:, lo:lo + pack], w_ref[j], preferred_element_type=F32)
        r = _sigmoid(pre[:, :pack] + ba_ref[:, lo:lo + pack])
        i = _sigmoid(pre[:, pack:] + bx_ref[:, lo:lo + pack])
        log_a = (-LRU_C) * r * sp[:, lo:lo + pack]
        a = jnp.exp(log_a)
        bt = jnp.sqrt(1.0 - jnp.exp(2.0 * log_a)) * (i * u2[:, lo:lo + pack])
        a_ref[:, :, lo:lo + pack] = a.reshape(tt, bsz, pack)
        b_ref[:, :, lo:lo + pack] = bt.reshape(tt, bsz, pack)


def _lru_fill_window(win_ref, x_ref, prev_ref, next_ref, at_start, at_end, tt):
    zero2 = jnp.zeros(prev_ref.shape, F32)
    win_ref[pl.ds(0, 2)] = jnp.where(at_start, zero2, prev_ref[...])
    win_ref[pl.ds(2, tt)] = x_ref[...]
    win_ref[pl.ds(tt + 2, 1)] = jnp.where(at_end, jnp.zeros(next_ref.shape, F32), next_ref[...])


def _lru_bwd_kernel(x_ref, prev_ref, next_ref, w_ref, ba_ref, bx_ref, lam_ref, cw_ref, cb_ref,
                    hb_ref, win_ref, a_ref, b_ref, h_ref, *, tt, ncb, nlb):
    i = pl.program_id(0)
    blk = jnp.where(i < ncb, ncb - 1 - i, ncb + nlb - 1 - (i - ncb))
    at_start = (blk == 0) | (blk == ncb)
    at_end = (blk == ncb - 1) | (blk == ncb + nlb - 1)

    @pl.when(i == 0)
    def _():
        h_ref[...] = jnp.zeros(h_ref.shape, F32)

    _lru_fill_window(win_ref, x_ref, prev_ref, next_ref, at_start, at_end, tt)
    _lru_gates(win_ref, w_ref, ba_ref, bx_ref, lam_ref, cw_ref, cb_ref, a_ref, b_ref, tt)

    def step(k, h):
        t = tt - 1 - k
        h = a_ref[t] * h + b_ref[t]
        a_ref[t] = h
        return h

    h_ref[...] = lax.fori_loop(0, tt, step, h_ref[...])

    @pl.when(i >= ncb)
    def _():
        hb_ref[...] = a_ref[...]


def _lru_fwd_kernel(x_ref, prev_ref, next_ref, w_ref, ba_ref, bx_ref, lam_ref, cw_ref, cb_ref,
                    hb_ref, gr_ref, gm_ref, bm_ref, wbr_ref,
                    o_ref, win_ref, a_ref, b_ref, h_ref, *, tt, ncb, nlb):
    i = pl.program_id(0)
    at_start = (i == 0) | (i == ncb)
    at_end = (i == ncb - 1) | (i == ncb + nlb - 1)

    @pl.when(i == 0)
    def _():
        h_ref[...] = jnp.zeros(h_ref.shape, F32)

    _lru_fill_window(win_ref, x_ref, prev_ref, next_ref, at_start, at_end, tt)
    _lru_gates(win_ref, w_ref, ba_ref, bx_ref, lam_ref, cw_ref, cb_ref, a_ref, b_ref, tt)

    def step(t, h):
        h = a_ref[t] * h + b_ref[t]
        a_ref[t] = h
        return h

    h_ref[...] = lax.fori_loop(0, tt, step, h_ref[...])

    @pl.when(i >= ncb)
    def _():
        bsz, c = h_ref.shape
        rows = tt * bsz
        a_lat = (a_ref[...] + hb_ref[...]) * jax.nn.gelu(gr_ref[...], approximate=True)
        proj = jnp.dot(a_lat.reshape(rows, c).astype(BF16), wbr_ref[...], preferred_element_type=F32)
        gate = _sigmoid(gm_ref[...].reshape(rows, -1) + bm_ref[...])
        o_ref[...] = (gate * proj).reshape(o_ref.shape)


def _lru_specs(tt, bsz, c, blk_of, ttot):
    half = tt // 2
    return [pl.BlockSpec((tt, bsz, c), lambda i: (blk_of(i), 0, 0)),
            pl.BlockSpec((2, bsz, c), lambda i: (jnp.maximum(blk_of(i) * half - 1, 0), 0, 0)),
            pl.BlockSpec((1, bsz, c), lambda i: (jnp.minimum(blk_of(i) * tt + tt, ttot - 1), 0, 0))]


def _lru(xr_all, grgm, p_f, p_b, conv_w, conv_b, b_merge_lru, w_br, t_ctx, tt):
    ttot, bsz, c = xr_all.shape
    ncb, nlb = t_ctx // tt, (ttot - t_ctx) // tt
    nblk = ncb + nlb
    d = w_br.shape[1]
    const2 = lambda i: (0, 0)
    const3 = lambda i: (0, 0, 0)
    par_specs = [pl.BlockSpec(p_f[0].shape, const3), pl.BlockSpec((1, c), const2), pl.BlockSpec((1, c), const2),
                 pl.BlockSpec((1, c), const2), pl.BlockSpec((CONV_W, c), const2), pl.BlockSpec((1, c), const2)]
    scratch = [pltpu.VMEM((tt + 3, bsz, c), F32), pltpu.VMEM((tt, bsz, c), F32),
               pltpu.VMEM((tt, bsz, c), F32), pltpu.VMEM((bsz, c), F32)]

    bwd_blk = lambda i: jnp.where(i < ncb, ncb - 1 - i, ncb + nlb - 1 - (i - ncb))
    hb = pl.pallas_call(
        functools.partial(_lru_bwd_kernel, tt=tt, ncb=ncb, nlb=nlb),
        out_shape=jax.ShapeDtypeStruct((nlb * tt, bsz, c), F32),
        grid=(nblk,),
        in_specs=_lru_specs(tt, bsz, c, bwd_blk, ttot) + par_specs,
        out_specs=pl.BlockSpec((tt, bsz, c), lambda i: (jnp.where(i < ncb, nlb - 1, nblk - 1 - i), 0, 0)),
        scratch_shapes=scratch,
        compiler_params=_cparams("arbitrary"),
        name="lru_bwd",
    )(xr_all, xr_all, xr_all, *p_b, conv_w, conv_b.reshape(1, c))

    assert c == d
    lat = lambda i: (jnp.maximum(i - ncb, 0), 0, 0)
    lat1 = lambda i: (jnp.maximum(i - ncb, 0), 0, 1)
    return pl.pallas_call(
        functools.partial(_lru_fwd_kernel, tt=tt, ncb=ncb, nlb=nlb),
        out_shape=jax.ShapeDtypeStruct((nlb * tt, bsz, d), F32),
        grid=(nblk,),
        in_specs=_lru_specs(tt, bsz, c, lambda i: i, ttot) + par_specs + [
            pl.BlockSpec((tt, bsz, c), lat), pl.BlockSpec((tt, bsz, c), lat), pl.BlockSpec((tt, bsz, d), lat1),
            pl.BlockSpec((1, d), const2), pl.BlockSpec((c, d), const2)],
        out_specs=pl.BlockSpec((tt, bsz, d), lat),
        scratch_shapes=scratch,
        compiler_params=_cparams("arbitrary"),
        name="lru_fwd",
    )(xr_all, xr_all, xr_all, *p_f, conv_w, conv_b.reshape(1, c), hb, grgm, grgm, b_merge_lru.reshape(1, d), w_br)


def _lru_params(wa, ba, wx, bx, lam, pack):
    nb, bw, _ = wa.shape
    per = pack // bw
    c = nb * bw

    def bd(w):
        w4 = w.reshape(nb // per, per, bw, bw)
        eye = jnp.eye(per, dtype=w.dtype)
        return jnp.einsum('gpde,pq->gpdqe', w4, eye).reshape(nb // per, pack, pack)

    w = jnp.concatenate([bd(wa), bd(wx)], axis=-1).astype(BF16)
    return w, ba.reshape(1, c), bx.reshape(1, c), lam.reshape(1, c)


def _hi_lo(v):
    hi = v.astype(BF16)
    lo = (v - hi.astype(F32)).astype(BF16)
    return jnp.concatenate([hi, lo], axis=-1)


def _ssd_kernel(xbc_ref, dt_ref, dtb_ref, alog_ref, e_ref, dsk_ref, h0_ref, *outs,
                reverse, with_y, add_skip, lane0, inner):
    if with_y:
        y_ref, hfin_ref, st_ref = outs
    else:
        hfin_ref, st_ref = outs
    i = pl.program_id(1)
    q = SSD_CHUNK
    n = SSD_STATE
    gw = inner // SSD_GROUPS
    hpg = gw // SSD_HEAD_DIM

    @pl.when(i == 0)
    def _():
        st_ref[...] = h0_ref[0]

    dt = _softplus(dt_ref[0] + dtb_ref[...])
    da = dt * (-jnp.exp(alog_ref[...]))
    ri = lax.broadcasted_iota(jnp.int32, (q, q), 0)
    ci = lax.broadcasted_iota(jnp.int32, (q, q), 1)
    tri = (ri <= ci) if reverse else (ri >= ci)
    acum = jnp.dot(tri.astype(F32), da, precision=HIGHEST, preferred_element_type=F32)
    a_tot = jnp.sum(da, axis=0, keepdims=True)
    w_state = jnp.exp(a_tot - acum) * dt
    e2 = e_ref[...]
    ws_x = jnp.dot(_hi_lo(w_state), e2, preferred_element_type=F32)
    dec_x = jnp.dot(_hi_lo(jnp.broadcast_to(jnp.exp(a_tot), (8, LANES))), e2,
                    preferred_element_type=F32)[0:1, :]
    if with_y:
        eac_x = jnp.dot(_hi_lo(jnp.exp(acum)), e2, preferred_element_type=F32)
        acum_t = acum.T
        dt_t = dt.T
        rb = lax.broadcasted_iota(jnp.int32, (hpg * q, gw), 0) // q
        lb = lax.broadcasted_iota(jnp.int32, (hpg * q, gw), 1) // SSD_HEAD_DIM
        bd_mask = rb == lb

    for g in range(SSD_GROUPS):
        lo = g * gw
        xg = xbc_ref[0, :, lo:lo + gw]
        bg = xbc_ref[0, :, inner + g * n:inner + (g + 1) * n]
        cg = xbc_ref[0, :, inner + (SSD_GROUPS + g) * n:inner + (SSD_GROUPS + g + 1) * n]
        xgf = xg.astype(F32)
        xw = (xgf * ws_x[:, lo:lo + gw]).astype(BF16)
        st = st_ref[g]
        upd = lax.dot_general(bg, xw, (((0,), (0,)), ((), ())), preferred_element_type=F32)
        st_ref[g] = dec_x[:, lo:lo + gw] * st + upd
        if with_y:
            y_off = jnp.dot(cg, st.astype(BF16), preferred_element_type=F32) * eac_x[:, lo:lo + gw]
            cb = lax.dot_general(cg, bg, (((1,), (1,)), ((), ())), preferred_element_type=F32)
            ls = []
            for r in range(hpg):
                lane = lane0 + g * hpg + r
                seg = acum[:, lane:lane + 1] - acum_t[lane:lane + 1, :]
                l_h = cb * jnp.exp(jnp.where(tri, seg, NEG_BIG)) * dt_t[lane:lane + 1, :]
                ls.append(l_h.astype(BF16))
            lcat = jnp.concatenate(ls, axis=1)
            xbd = jnp.where(bd_mask, jnp.concatenate([xg] * hpg, axis=0), jnp.zeros((), BF16))
            y = y_off + jnp.dot(lcat, xbd, preferred_element_type=F32)
            if add_skip:
                y = y + dsk_ref[:, lo:lo + gw] * xgf
            y_ref[0, :, lo:lo + gw] = y

    @pl.when(i == pl.num_programs(1) - 1)
    def _():
        hfin_ref[0] = st_ref[...]


def _ssd(xbc, dt_raw, dtb, alog, e2, dsk, h0, *, reverse, with_y, add_skip, lane0):
    bsz, s, width = xbc.shape
    inner = e2.shape[1]
    nc = s // SSD_CHUNK
    gw = inner // SSD_GROUPS
    cidx = (lambda b, i: (b, nc - 1 - i, 0)) if reverse else (lambda b, i: (b, i, 0))
    const2 = lambda b, i: (0, 0)
    st_spec = pl.BlockSpec((1, SSD_GROUPS, SSD_STATE, gw), lambda b, i: (b, 0, 0, 0))
    st_shape = jax.ShapeDtypeStruct((bsz, SSD_GROUPS, SSD_STATE, gw), F32)
    out_shape, out_specs = [st_shape], [st_spec]
    if with_y:
        out_shape = [jax.ShapeDtypeStruct((bsz, s, inner), F32)] + out_shape
        out_specs = [pl.BlockSpec((1, SSD_CHUNK, inner), cidx)] + out_specs
    return pl.pallas_call(
        functools.partial(_ssd_kernel, reverse=reverse, with_y=with_y, add_skip=add_skip, lane0=lane0, inner=inner),
        out_shape=out_shape,
        grid=(bsz, nc),
        in_specs=[pl.BlockSpec((1, SSD_CHUNK, width), cidx),
                  pl.BlockSpec((1, SSD_CHUNK, LANES), cidx),
                  pl.BlockSpec((1, LANES), const2), pl.BlockSpec((1, LANES), const2),
                  pl.BlockSpec(e2.shape, const2), pl.BlockSpec((1, inner), const2), st_spec],
        out_specs=out_specs,
        scratch_shapes=[pltpu.VMEM((SSD_GROUPS, SSD_STATE, gw), F32)],
        compiler_params=_cparams("arbitrary", "arbitrary"),
        name="ssd_y" if with_y else "ssd_state",
    )(xbc, dt_raw, dtb, alog, e2, dsk, h0)


def _ssdbr_kernel(yf_ref, yb_ref, z_ref, gm_ref, nw_ref, bm_ref, gsum_ref, gexp_ref, w_ref, o_ref, *, group_w):
    y = (yf_ref[0] + yb_ref[0]) * _silu(z_ref[0])
    ms = jnp.dot((y * y).astype(BF16), gsum_ref[...], preferred_element_type=F32) * (1.0 / group_w)
    rs = lax.rsqrt(ms + RMS_EPS)
    rs_x = jnp.dot(_hi_lo(rs), gexp_ref[...], preferred_element_type=F32)
    yn = (y * rs_x * nw_ref[...]).astype(BF16)
    proj = jnp.dot(yn, w_ref[...], preferred_element_type=F32)
    o_ref[0] = _sigmoid(gm_ref[0] + bm_ref[...]) * proj


def _ssdbr(y_f, y_b, zgm, norm_w, b_merge_ssd, gsum, gexp, w_br, rows):
    bsz, s, inner = y_f.shape
    d = w_br.shape[1]
    assert inner % d == 0
    tok = lambda b, i: (b, i, 0)
    const2 = lambda b, i: (0, 0)
    return pl.pallas_call(
        functools.partial(_ssdbr_kernel, group_w=inner // SSD_GROUPS),
        out_shape=jax.ShapeDtypeStruct((bsz, s, d), F32),
        grid=(bsz, s // rows),
        in_specs=[pl.BlockSpec((1, rows, inner), tok), pl.BlockSpec((1, rows, inner), tok),
                  pl.BlockSpec((1, rows, inner), tok),
                  pl.BlockSpec((1, rows, d), lambda b, i: (b, i, inner // d)),
                  pl.BlockSpec((1, inner), const2), pl.BlockSpec((1, d), const2),
                  pl.BlockSpec(gsum.shape, const2), pl.BlockSpec(gexp.shape, const2),
                  pl.BlockSpec((inner, d), const2)],
        out_specs=pl.BlockSpec((1, rows, d), tok),
        compiler_params=_cparams("arbitrary", "arbitrary"),
        name="ssdbr",
    )(y_f, y_b, zgm, zgm, norm_w.reshape(1, inner), b_merge_ssd.reshape(1, d), gsum, gexp, w_br)


def _out_kernel(ul_ref, us_ref, x_ref, g1_ref, sh2_ref, sc2_ref, wout_ref, lg_ref, lb_ref, rw_ref, rb_ref,
                x1_ref, h2_ref, gates_ref, *, alpha):
    u = (ul_ref[0] + us_ref[0]).astype(BF16)
    mix = jnp.dot(u, wout_ref[...], preferred_element_type=F32)
    x1 = _ln_rows(alpha * x_ref[0] + g1_ref[0] * mix) * lg_ref[...] + lb_ref[...]
    x1_ref[0] = x1
    h2 = _ln_rows(x1) * (1.0 + sc2_ref[0]) + sh2_ref[0]
    h2_ref[0] = h2.astype(BF16)
    logits = jnp.dot(h2, rw_ref[...], precision=HIGHEST, preferred_element_type=F32) + rb_ref[...]
    lane = lax.broadcasted_iota(jnp.int32, logits.shape, 1)
    work = logits
    sel = jnp.zeros(logits.shape, jnp.bool_)
    top = None
    for k in range(MOE_TOP_K):
        m = jnp.max(work, axis=1, keepdims=True)
        if k == 0:
            top = m
        idx = jnp.min(jnp.where(work == m, lane, LANES), axis=1, keepdims=True)
        pick = lane == idx
        sel = sel | pick
        work = jnp.where(pick, 3.0 * NEG_BIG, work)
    e = jnp.where(sel, jnp.exp(logits - top), 0.0)
    gates_ref[0] = e / jnp.sum(e, axis=1, keepdims=True)


def _out(u_l, u_s, x, g1, sh2, sc2, w_out, ln_g, ln_b, rw, rb, rows, alpha):
    bsz, t, d = x.shape
    tok = lambda b, i: (b, i, 0)
    per_b = lambda b, i: (b, 0, 0)
    const2 = lambda b, i: (0, 0)
    return pl.pallas_call(
        functools.partial(_out_kernel, alpha=alpha),
        out_shape=[jax.ShapeDtypeStruct((bsz, t, d), F32), jax.ShapeDtypeStruct((bsz, t, d), BF16),
                   jax.ShapeDtypeStruct((bsz, t, LANES), F32)],
        grid=(bsz, t // rows),
        in_specs=[pl.BlockSpec((1, rows, d), tok), pl.BlockSpec((1, rows, d), tok), pl.BlockSpec((1, rows, d), tok),
                  pl.BlockSpec((1, 1, d), per_b), pl.BlockSpec((1, 1, d), per_b), pl.BlockSpec((1, 1, d), per_b),
                  pl.BlockSpec((d, d), const2), pl.BlockSpec((1, d), const2), pl.BlockSpec((1, d), const2),
                  pl.BlockSpec((d, LANES), const2), pl.BlockSpec((1, LANES), const2)],
        out_specs=[pl.BlockSpec((1, rows, d), tok), pl.BlockSpec((1, rows, d), tok),
                   pl.BlockSpec((1, rows, LANES), tok)],
        compiler_params=_cparams("arbitrary", "arbitrary"),
        name="out",
    )(u_l, u_s, x, g1, sh2, sc2, w_out, ln_g.reshape(1, d), ln_b.reshape(1, d), rw, rb)


def _moe_kernel(h2_ref, gates_ref, x1_ref, g2_ref, w1_ref, b1_ref, w2_ref, b2_ref, lg_ref, lb_ref,
                o_ref, acc_ref, *, alpha, ff):
    e = pl.program_id(2)

    @pl.when(e == 0)
    def _():
        acc_ref[...] = jnp.zeros(acc_ref.shape, F32)

    gu = jnp.dot(h2_ref[0], w1_ref[0].astype(BF16), preferred_element_type=F32) + b1_ref[0]
    g = jnp.minimum(gu[:, :ff], SWIGLU_LIMIT)
    u = jnp.clip(gu[:, ff:], -SWIGLU_LIMIT, SWIGLU_LIMIT)
    act = ((u + 1.0) * g * _sigmoid(SWIGLU_ALPHA * g)).astype(BF16)
    y = jnp.dot(act, w2_ref[0].astype(BF16), preferred_element_type=F32) + b2_ref[0]
    gates = gates_ref[0]
    lane = lax.broadcasted_iota(jnp.int32, gates.shape, 1)
    gcol = jnp.sum(jnp.where(lane == e, gates, 0.0), axis=1, keepdims=True)
    acc_ref[...] += gcol * y

    @pl.when(e == pl.num_programs(2) - 1)
    def _():
        o_ref[0] = _ln_rows(alpha * x1_ref[0] + g2_ref[0] * acc_ref[...]) * lg_ref[...] + lb_ref[...]


def _moe(h2, gates, x1, g2, w1, b1, w2, b2, ln_g, ln_b, rows, alpha):
    bsz, t, d = x1.shape
    n_exp, _, ff2 = w1.shape
    ff = ff2 // 2
    tok = lambda b, i, e: (b, i, 0)
    per_b = lambda b, i, e: (b, 0, 0)
    per_e = lambda b, i, e: (e, 0, 0)
    const2 = lambda b, i, e: (0, 0)
    return pl.pallas_call(
        functools.partial(_moe_kernel, alpha=alpha, ff=ff),
        out_shape=jax.ShapeDtypeStruct((bsz, t, d), F32),
        grid=(bsz, t // rows, n_exp),
        in_specs=[pl.BlockSpec((1, rows, d), tok), pl.BlockSpec((1, rows, LANES), tok),
                  pl.BlockSpec((1, rows, d), tok), pl.BlockSpec((1, 1, d), per_b),
                  pl.BlockSpec((1, d, ff2), per_e), pl.BlockSpec((1, 1, ff2), per_e),
                  pl.BlockSpec((1, ff, d), per_e), pl.BlockSpec((1, 1, d), per_e),
                  pl.BlockSpec((1, d), const2), pl.BlockSpec((1, d), const2)],
        out_specs=pl.BlockSpec((1, rows, d), tok),
        scratch_shapes=[pltpu.VMEM((rows, d), F32)],
        compiler_params=_cparams("arbitrary", "arbitrary", "arbitrary"),
        name="moe",
    )(h2, gates, x1, g2, w1, b1.reshape(n_exp, 1, ff2), w2, b2.reshape(n_exp, 1, d),
      ln_g.reshape(1, d), ln_b.reshape(1, d))


def _to_cols(u, grid_rows):
    b, t, c = u.shape
    return u.reshape(b, grid_rows, GRID_W, c).transpose(0, 2, 1, 3).reshape(b, t, c)


def _from_cols(u, grid_rows):
    b, t, c = u.shape
    return u.reshape(b, GRID_W, grid_rows, c).transpose(0, 2, 1, 3).reshape(b, t, c)


def kernel(x, c, ctx, c_ctx, w_ada, b_ada, w_in, b_merge, conv_lru_w, conv_lru_b, lru_wa, lru_ba, lru_wx, lru_bx, lru_lambda, conv_ssd_w, conv_ssd_b, ssd_dt_bias, ssd_a_log, ssd_d, ssd_norm_w, w_br_lru, w_br_ssd, w_out, ln1_g, ln1_b, router_w, router_b, moe_w1, moe_b1, moe_w2, moe_b2, ln2_g, ln2_b):
    depth = w_ada.shape[0]
    assert depth == 1, "single-layer stack: the context tokens only supply scan states"
    bsz, t, d = x.shape
    t_ctx = ctx.shape[1]
    alpha = (2.0 * depth) ** 0.25
    grid_rows = t // GRID_W
    d_rnn = w_br_lru.shape[1]
    inner = w_br_ssd.shape[1]
    heads = inner // SSD_HEAD_DIM
    gn = SSD_GROUPS * SSD_STATE
    col_gr = d_rnn
    col_z = col_gr + d_rnn
    col_xbc = col_z + inner
    col_dt = col_xbc + inner + 2 * gn
    col_gm = col_dt + 2 * heads
    assert 2 * heads <= LANES and t_ctx % SSD_CHUNK == 0 and t % SSD_CHUNK == 0

    pad = (-(bsz + 1)) % 8
    c_all = jnp.concatenate([c, c_ctx[None, :], jnp.zeros((pad, d), F32)], axis=0)
    mod = _ada(c_all, w_ada[0], b_ada[0])
    sh1, sc1, g1, sh2, sc2, g2 = (mod[:bsz, k * d:(k + 1) * d] for k in range(6))
    csh1, csc1 = mod[bsz:bsz + 1, 0:d], mod[bsz:bsz + 1, d:2 * d]

    w_in_b = w_in[0].astype(BF16)

    tt = 32
    xall_tm = jnp.concatenate([jnp.transpose(ctx, (1, 0, 2)), jnp.transpose(x, (1, 0, 2))], axis=0)
    sh_tm = jnp.stack([jnp.broadcast_to(csh1, (bsz, d)), sh1])
    sc_tm = jnp.stack([jnp.broadcast_to(csc1, (bsz, d)), sc1])
    seg = lambda i: jnp.where(i * tt >= t_ctx, 1, 0)
    xr_all = _inproj(xall_tm, sh_tm, sc_tm, w_in_b[:, :col_gr], (tt, bsz), 1024, sel=seg)
    w_grgm = jnp.concatenate([w_in_b[:, col_gr:col_z], w_in_b[:, col_gm:col_gm + d]], axis=1)
    grgm = _inproj(xall_tm, sh_tm, sc_tm, w_grgm, (tt, bsz), 1024, sel=seg, first=t_ctx // tt, count=t // tt)
    p_f = _lru_params(lru_wa[0, 0], lru_ba[0, 0], lru_wx[0, 0], lru_bx[0, 0], lru_lambda[0, 0], 256)
    p_b = _lru_params(lru_wa[0, 1], lru_ba[0, 1], lru_wx[0, 1], lru_bx[0, 1], lru_lambda[0, 1], 256)
    u_lru_tm = _lru(xr_all, grgm, p_f, p_b, conv_lru_w[0], conv_lru_b[0], b_merge[0, :d],
                    w_br_lru[0].astype(BF16), t_ctx, tt)

    x_cm = _to_cols(x, grid_rows)
    sh_b, sc_b = sh1[:, None, :], sc1[:, None, :]
    csh_b, csc_b = csh1[None], csc1[None]
    zero = lambda i: 0
    w_xbc = w_in_b[:, col_xbc:col_dt]
    w_dt = jnp.pad(w_in_b[:, col_dt:col_gm], ((0, 0), (0, LANES - 2 * heads)))
    w_zgm = jnp.concatenate([w_in_b[:, col_z:col_xbc], w_in_b[:, col_gm + d:]], axis=1)
    xbc = _inproj(x_cm, sh_b, sc_b, w_xbc, (1, t), 256, conv_w=conv_ssd_w[0], conv_b=conv_ssd_b[0], out_dtype=BF16)
    xbc_c = _inproj(ctx, csh_b, csc_b, w_xbc, (1, t_ctx), 256, sel=zero, conv_w=conv_ssd_w[0],
                    conv_b=conv_ssd_b[0], out_dtype=BF16)
    dt_raw = _inproj(x_cm, sh_b, sc_b, w_dt, (1, 512), LANES)
    dt_raw_c = _inproj(ctx, csh_b, csc_b, w_dt, (1, t_ctx), LANES, sel=zero)
    zgm = _inproj(x_cm, sh_b, sc_b, w_zgm, (1, 512), 1024)

    lane_pad = LANES - 2 * heads
    dtb = jnp.pad(ssd_dt_bias[0].reshape(1, 2 * heads), ((0, 0), (0, lane_pad)))
    alog = jnp.pad(ssd_a_log[0].reshape(1, 2 * heads), ((0, 0), (0, lane_pad)))
    dsk = jnp.repeat(ssd_d[0], SSD_HEAD_DIM).reshape(1, inner)
    head_of_lane = jnp.arange(inner) // SSD_HEAD_DIM

    def expand(lane0):
        e = (jnp.arange(LANES)[:, None] == head_of_lane[None, :] + lane0).astype(BF16)
        return jnp.concatenate([e, e], axis=0)

    gw = inner // SSD_GROUPS
    s0 = jnp.zeros((bsz, SSD_GROUPS, SSD_STATE, gw), F32)
    common = (dtb, alog)
    (st_f,) = _ssd(xbc_c, dt_raw_c, *common, expand(0), dsk, s0, reverse=False, with_y=False, add_skip=False, lane0=0)
    (st_b,) = _ssd(xbc_c, dt_raw_c, *common, expand(heads), dsk, s0, reverse=True, with_y=False, add_skip=False,
                   lane0=heads)
    y_f, _ = _ssd(xbc, dt_raw, *common, expand(0), dsk, st_f, reverse=False, with_y=True, add_skip=True, lane0=0)
    y_b, _ = _ssd(xbc, dt_raw, *common, expand(heads), dsk, st_b, reverse=True, with_y=True, add_skip=False,
                  lane0=heads)

    group_of_lane = jnp.arange(inner) // gw
    gsum = (group_of_lane[:, None] == jnp.arange(LANES)[None, :]).astype(BF16)
    gexp = jnp.concatenate([gsum.T, gsum.T], axis=0)
    u_ssd_cm = _ssdbr(y_f, y_b, zgm, ssd_norm_w[0], b_merge[0, d:], gsum, gexp, w_br_ssd[0].astype(BF16), 256)

    u_lru = jnp.transpose(u_lru_tm, (1, 0, 2))
    u_ssd = _from_cols(u_ssd_cm, grid_rows)
    n_exp = router_w.shape[2]
    rw = jnp.pad(router_w[0], ((0, 0), (0, LANES - n_exp)))
    rb = jnp.pad(router_b[0].reshape(1, n_exp), ((0, 0), (0, LANES - n_exp)), constant_values=NEG_BIG)
    x1, h2, gates = _out(u_lru, u_ssd, x, g1[:, None, :], sh2[:, None, :], sc2[:, None, :],
                         w_out[0].astype(BF16), ln1_g[0], ln1_b[0], rw, rb, 512, alpha)

    return _moe(h2, gates, x1, g2[:, None, :], moe_w1[0], moe_b1[0], moe_w2[0], moe_b2[0],
                ln2_g[0], ln2_b[0], 512, alpha)
```

```python
import functools

import jax
import jax.numpy as jnp
from jax import lax
from jax.experimental import pallas as pl
from jax.experimental.pallas import tpu as pltpu

F32 = jnp.float32
BF16 = jnp.bfloat16
HIGHEST = lax.Precision.HIGHEST

GRID_W = 64
LRU_BLOCK_W = 64
LRU_C = 8.0
CONV_W = 4
SSD_HEAD_DIM = 64
SSD_GROUPS = 8
SSD_STATE = 128
SSD_CHUNK = 128
MOE_TOP_K = 4
SWIGLU_LIMIT = 7.0
SWIGLU_ALPHA = 1.702
LN_EPS = 1e-5
RMS_EPS = 1e-5
LANES = 128
NEG_BIG = -1e30
VMEM_LIMIT = 56 * 1024 * 1024


def _cparams(*sem):
    return pltpu.CompilerParams(dimension_semantics=sem, vmem_limit_bytes=VMEM_LIMIT)


def _ln_rows(x):
    mu = jnp.mean(x, axis=-1, keepdims=True)
    xc = x - mu
    var = jnp.mean(xc * xc, axis=-1, keepdims=True)
    return xc * lax.rsqrt(var + LN_EPS)


def _sigmoid(x):
    return 1.0 / (1.0 + jnp.exp(-x))


def _silu(x):
    return x * _sigmoid(x)


def _softplus(x):
    return jnp.maximum(x, 0.0) + jnp.log(1.0 + jnp.exp(-jnp.abs(x)))


def _ada_kernel(c_ref, w_ref, b_ref, o_ref):
    c = c_ref[...]
    o_ref[...] = jnp.dot(_silu(c), w_ref[...], precision=HIGHEST, preferred_element_type=F32) + b_ref[...]


def _ada(c_all, w, b):
    m, d = c_all.shape
    n = w.shape[1]
    tn = 1024
    return pl.pallas_call(
        _ada_kernel,
        out_shape=jax.ShapeDtypeStruct((m, n), F32),
        grid=(n // tn,),
        in_specs=[pl.BlockSpec((m, d), lambda j: (0, 0)),
                  pl.BlockSpec((d, tn), lambda j: (0, j)),
                  pl.BlockSpec((1, tn), lambda j: (0, j))],
        out_specs=pl.BlockSpec((m, tn), lambda j: (0, j)),
        compiler_params=_cparams("arbitrary"),
        name="ada",
    )(c_all, w, b.reshape(1, n))


def _inproj_kernel(x_ref, sh_ref, sc_ref, w_ref, *rest, conv):
    if conv:
        cw_ref, cb_ref, o_ref, h_ref = rest
    else:
        o_ref, h_ref = rest
    rows = h_ref.shape[0]

    @pl.when(pl.program_id(1) == 0)
    def _():
        h = _ln_rows(x_ref[...]) * (1.0 + sc_ref[...]) + sh_ref[...]
        h_ref[...] = h.reshape(h_ref.shape).astype(BF16)

    acc = jnp.dot(h_ref[...], w_ref[...], preferred_element_type=F32)
    if conv:
        ridx = lax.broadcasted_iota(jnp.int32, acc.shape, 0)
        cw = cw_ref[...]
        out = acc * cw[2:3, :] + cb_ref[...]
        for j, shift in ((0, 2), (1, 1)):
            rolled = pltpu.roll(acc, shift, 0)
            out = out + jnp.where(ridx >= shift, rolled, 0.0) * cw[j:j + 1, :]
        rolled = pltpu.roll(acc, rows - 1, 0)
        out = out + jnp.where(ridx < rows - 1, rolled, 0.0) * cw[3:4, :]
        acc = _silu(out)
    o_ref[...] = acc.reshape(o_ref.shape).astype(o_ref.dtype)


def _inproj(x3, sh3, sc3, w, tile, tn, sel=None, conv_w=None, conv_b=None, out_dtype=F32, first=0, count=None):
    gdim, rdim, d = x3.shape
    g, r = tile
    n = w.shape[1]
    rows = g * r
    nblk_g = gdim // g if count is None else count
    nblk_r = rdim // r
    if sel is None:
        sel = lambda i: i
    conv = conv_w is not None
    mg, mr = sh3.shape[1], sh3.shape[2]
    in_specs = [pl.BlockSpec((g, r, d), lambda i, j: ((i // nblk_r) + first, i % nblk_r, 0)),
                pl.BlockSpec((1, mg, mr), lambda i, j: (sel((i // nblk_r) + first), 0, 0)),
                pl.BlockSpec((1, mg, mr), lambda i, j: (sel((i // nblk_r) + first), 0, 0)),
                pl.BlockSpec((d, tn), lambda i, j: (0, j))]
    args = [x3, sh3, sc3, w]
    if conv:
        assert r == rdim and g == 1
        in_specs += [pl.BlockSpec((CONV_W, tn), lambda i, j: (0, j)),
                     pl.BlockSpec((1, tn), lambda i, j: (0, j))]
        args += [conv_w, conv_b.reshape(1, n)]
    out_g = nblk_g * g
    return pl.pallas_call(
        functools.partial(_inproj_kernel, conv=conv),
        out_shape=jax.ShapeDtypeStruct((out_g, rdim, n), out_dtype),
        grid=(nblk_g * nblk_r, n // tn),
        in_specs=in_specs,
        out_specs=pl.BlockSpec((g, r, tn), lambda i, j: (i // nblk_r, i % nblk_r, j)),
        scratch_shapes=[pltpu.VMEM((rows, d), BF16)],
        compiler_params=_cparams("arbitrary", "arbitrary"),
        name="inproj_conv" if conv else "inproj",
    )(*args)


def _lru_gates(win_ref, w_ref, ba_ref, bx_ref, lam_ref, cw_ref, cb_ref, a_ref, b_ref, tt):
    bsz, c = win_ref.shape[1], win_ref.shape[2]
    cw = cw_ref[...]
    u = cb_ref[...].reshape(1, 1, c) + sum(win_ref[pl.ds(j, tt)] * cw[j:j + 1, :].reshape(1, 1, c)
                                           for j in range(CONV_W))
    u2 = u.reshape(tt * bsz, c)
    ub = u2.astype(BF16)
    sp = _softplus(-lam_ref[...])
    pack = w_ref.shape[1]
    for j in range(c // pack):
        lo = j * pack
        pre = jnp.dot(ub[:, lo:lo + pack], w_ref[j], preferred_element_type=F32)
        r = _sigmoid(pre[:, :pack] + ba_ref[:, lo:lo + pack])
        i = _sigmoid(pre[:, pack:] + bx_ref[:, lo:lo + pack])
        log_a = (-LRU_C) * r * sp[:, lo:lo + pack]
        a = jnp.exp(log_a)
        bt = jnp.sqrt(1.0 - jnp.exp(2.0 * log_a)) * (i * u2[:, lo:lo + pack])
        a_ref[:, :, lo:lo + pack] = a.reshape(tt, bsz, pack)
        b_ref[:, :, lo:lo + pack] = bt.reshape(tt, bsz, pack)


def _lru_fill_window(win_ref, x_ref, prev_ref, next_ref, at_start, at_end, tt):
    zero2 = jnp.zeros(prev_ref.shape, F32)
    win_ref[pl.ds(0, 2)] = jnp.where(at_start, zero2, prev_ref[...])
    win_ref[pl.ds(2, tt)] = x_ref[...]
    win_ref[pl.ds(tt + 2, 1)] = jnp.where(at_end, jnp.zeros(next_ref.shape, F32), next_ref[...])


def _lru_bwd_kernel(x_ref, prev_ref, next_ref, w_ref, ba_ref, bx_ref, lam_ref, cw_ref, cb_ref,
                    hb_ref, win_ref, a_ref, b_ref, h_ref, *, tt, ncb, nlb):
    i = pl.program_id(0)
    blk = jnp.where(i < ncb, ncb - 1 - i, ncb + nlb - 1 - (i - ncb))
    at_start = (blk == 0) | (blk == ncb)
    at_end = (blk == ncb - 1) | (blk == ncb + nlb - 1)

    @pl.when(i == 0)
    def _():
        h_ref[...] = jnp.zeros(h_ref.shape, F32)

    _lru_fill_window(win_ref, x_ref, prev_ref, next_ref, at_start, at_end, tt)
    _lru_gates(win_ref, w_ref, ba_ref, bx_ref, lam_ref, cw_ref, cb_ref, a_ref, b_ref, tt)

    def step(k, h):
        t = tt - 1 - k
        h = a_ref[t] * h + b_ref[t]
        a_ref[t] = h
        return h

    h_ref[...] = lax.fori_loop(0, tt, step, h_ref[...])

    @pl.when(i >= ncb)
    def _():
        hb_ref[...] = a_ref[...]


def _lru_fwd_kernel(x_ref, prev_ref, next_ref, w_ref, ba_ref, bx_ref, lam_ref, cw_ref, cb_ref,
                    hb_ref, gr_ref, gm_ref, bm_ref, wbr_ref,
                    o_ref, win_ref, a_ref, b_ref, h_ref, *, tt, ncb, nlb):
    i = pl.program_id(0)
    at_start = (i == 0) | (i == ncb)
    at_end = (i == ncb - 1) | (i == ncb + nlb - 1)

    @pl.when(i == 0)
    def _():
        h_ref[...] = jnp.zeros(h_ref.shape, F32)

    _lru_fill_window(win_ref, x_ref, prev_ref, next_ref, at_start, at_end, tt)
    _lru_gates(win_ref, w_ref, ba_ref, bx_ref, lam_ref, cw_ref, cb_ref, a_ref, b_ref, tt)

    def step(t, h):
        h = a_ref[t] * h + b_ref[t]
        a_ref[t] = h
        return h

    h_ref[...] = lax.fori_loop(0, tt, step, h_ref[...])

    @pl.when(i >= ncb)
    def _():
        bsz, c = h_ref.shape
        rows = tt * bsz
        a_lat = (a_ref[...] + hb_ref[...]) * jax.nn.gelu(gr_ref[...], approximate=True)
        proj = jnp.dot(a_lat.reshape(rows, c).astype(BF16), wbr_ref[...], preferred_element_type=F32)
        gate = _sigmoid(gm_ref[...].reshape(rows, -1) + bm_ref[...])
        o_ref[...] = (gate * proj).reshape(o_ref.shape)


def _lru_specs(tt, bsz, c, blk_of, ttot):
    half = tt // 2
    return [pl.BlockSpec((tt, bsz, c), lambda i: (blk_of(i), 0, 0)),
            pl.BlockSpec((2, bsz, c), lambda i: (jnp.maximum(blk_of(i) * half - 1, 0), 0, 0)),
            pl.BlockSpec((1, bsz, c), lambda i: (jnp.minimum(blk_of(i) * tt + tt, ttot - 1), 0, 0))]


def _lru(xr_all, grgm, p_f, p_b, conv_w, conv_b, b_merge_lru, w_br, t_ctx, tt):
    ttot, bsz, c = xr_all.shape
    ncb, nlb = t_ctx // tt, (ttot - t_ctx) // tt
    nblk = ncb + nlb
    d = w_br.shape[1]
    const2 = lambda i: (0, 0)
    const3 = lambda i: (0, 0, 0)
    par_specs = [pl.BlockSpec(p_f[0].shape, const3), pl.BlockSpec((1, c), const2), pl.BlockSpec((1, c), const2),
                 pl.BlockSpec((1, c), const2), pl.BlockSpec((CONV_W, c), const2), pl.BlockSpec((1, c), const2)]
    scratch = [pltpu.VMEM((tt + 3, bsz, c), F32), pltpu.VMEM((tt, bsz, c), F32),
               pltpu.VMEM((tt, bsz, c), F32), pltpu.VMEM((bsz, c), F32)]

    bwd_blk = lambda i: jnp.where(i < ncb, ncb - 1 - i, ncb + nlb - 1 - (i - ncb))
    hb = pl.pallas_call(
        functools.partial(_lru_bwd_kernel, tt=tt, ncb=ncb, nlb=nlb),
        out_shape=jax.ShapeDtypeStruct((nlb * tt, bsz, c), F32),
        grid=(nblk,),
        in_specs=_lru_specs(tt, bsz, c, bwd_blk, ttot) + par_specs,
        out_specs=pl.BlockSpec((tt, bsz, c), lambda i: (jnp.where(i < ncb, nlb - 1, nblk - 1 - i), 0, 0)),
        scratch_shapes=scratch,
        compiler_params=_cparams("arbitrary"),
        name="lru_bwd",
    )(xr_all, xr_all, xr_all, *p_b, conv_w, conv_b.reshape(1, c))

    assert c == d
    lat = lambda i: (jnp.maximum(i - ncb, 0), 0, 0)
    lat1 = lambda i: (jnp.maximum(i - ncb, 0), 0, 1)
    return pl.pallas_call(
        functools.partial(_lru_fwd_kernel, tt=tt, ncb=ncb, nlb=nlb),
        out_shape=jax.ShapeDtypeStruct((nlb * tt, bsz, d), F32),
        grid=(nblk,),
        in_specs=_lru_specs(tt, bsz, c, lambda i: i, ttot) + par_specs + [
            pl.BlockSpec((tt, bsz, c), lat), pl.BlockSpec((tt, bsz, c), lat), pl.BlockSpec((tt, bsz, d), lat1),
            pl.BlockSpec((1, d), const2), pl.BlockSpec((c, d), const2)],
        out_specs=pl.BlockSpec((tt, bsz, d), lat),
        scratch_shapes=scratch,
        compiler_params=_cparams("arbitrary"),
        name="lru_fwd",
    )(xr_all, xr_all, xr_all, *p_f, conv_w, conv_b.reshape(1, c), hb, grgm, grgm, b_merge_lru.reshape(1, d), w_br)


def _lru_params(wa, ba, wx, bx, lam, pack):
    nb, bw, _ = wa.shape
    per = pack // bw
    c = nb * bw

    def bd(w):
        w4 = w.reshape(nb // per, per, bw, bw)
        eye = jnp.eye(per, dtype=w.dtype)
        return jnp.einsum('gpde,pq->gpdqe', w4, eye).reshape(nb // per, pack, pack)

    w = jnp.concatenate([bd(wa), bd(wx)], axis=-1).astype(BF16)
    return w, ba.reshape(1, c), bx.reshape(1, c), lam.reshape(1, c)


def _hi_lo(v):
    hi = v.astype(BF16)
    lo = (v - hi.astype(F32)).astype(BF16)
    return jnp.concatenate([hi, lo], axis=-1)


def _ssd_kernel(xbc_ref, dt_ref, dtb_ref, alog_ref, e_ref, dsk_ref, h0_ref, *outs,
                reverse, with_y, add_skip, lane0, inner):
    if with_y:
        y_ref, hfin_ref, st_ref = outs
    else:
        hfin_ref, st_ref = outs
    i = pl.program_id(1)
    q = SSD_CHUNK
    n = SSD_STATE
    gw = inner // SSD_GROUPS
    hpg = gw // SSD_HEAD_DIM

    @pl.when(i == 0)
    def _():
        st_ref[...] = h0_ref[0]

    dt = _softplus(dt_ref[0] + dtb_ref[...])
    da = dt * (-jnp.exp(alog_ref[...]))
    ri = lax.broadcasted_iota(jnp.int32, (q, q), 0)
    ci = lax.broadcasted_iota(jnp.int32, (q, q), 1)
    tri = (ri <= ci) if reverse else (ri >= ci)
    acum = jnp.dot(tri.astype(F32), da, precision=HIGHEST, preferred_element_type=F32)
    a_tot = jnp.sum(da, axis=0, keepdims=True)
    w_state = jnp.exp(a_tot - acum) * dt
    e2 = e_ref[...]
    ws_x = jnp.dot(_hi_lo(w_state), e2, preferred_element_type=F32)
    dec_x = jnp.dot(_hi_lo(jnp.broadcast_to(jnp.exp(a_tot), (8, LANES))), e2,
                    preferred_element_type=F32)[0:1, :]
    if with_y:
        eac_x = jnp.dot(_hi_lo(jnp.exp(acum)), e2, preferred_element_type=F32)
        acum_t = acum.T
        dt_t = dt.T
        rb = lax.broadcasted_iota(jnp.int32, (hpg * q, gw), 0) // q
        lb = lax.broadcasted_iota(jnp.int32, (hpg * q, gw), 1) // SSD_HEAD_DIM
        bd_mask = rb == lb

    for g in range(SSD_GROUPS):
        lo = g * gw
        xg = xbc_ref[0, :, lo:lo + gw]
        bg = xbc_ref[0, :, inner + g * n:inner + (g + 1) * n]
        cg = xbc_ref[0, :, inner + (SSD_GROUPS + g) * n:inner + (SSD_GROUPS + g + 1) * n]
        xgf = xg.astype(F32)
        xw = (xgf * ws_x[:, lo:lo + gw]).astype(BF16)
        st = st_ref[g]
        upd = lax.dot_general(bg, xw, (((0,), (0,)), ((), ())), preferred_element_type=F32)
        st_ref[g] = dec_x[:, lo:lo + gw] * st + upd
        if with_y:
            y_off = jnp.dot(cg, st.astype(BF16), preferred_element_type=F32) * eac_x[:, lo:lo + gw]
            cb = lax.dot_general(cg, bg, (((1,), (1,)), ((), ())), preferred_element_type=F32)
            ls = []
            for r in range(hpg):
                lane = lane0 + g * hpg + r
                seg = acum[:, lane:lane + 1] - acum_t[lane:lane + 1, :]
                l_h = cb * jnp.exp(jnp.where(tri, seg, NEG_BIG)) * dt_t[lane:lane + 1, :]
                ls.append(l_h.astype(BF16))
            lcat = jnp.concatenate(ls, axis=1)
            xbd = jnp.where(bd_mask, jnp.concatenate([xg] * hpg, axis=0), jnp.zeros((), BF16))
            y = y_off + jnp.dot(lcat, xbd, preferred_element_type=F32)
            if add_skip:
                y = y + dsk_ref[:, lo:lo + gw] * xgf
            y_ref[0, :, lo:lo + gw] = y

    @pl.when(i == pl.num_programs(1) - 1)
    def _():
        hfin_ref[0] = st_ref[...]


def _ssd(xbc, dt_raw, dtb, alog, e2, dsk, h0, *, reverse, with_y, add_skip, lane0):
    bsz, s, width = xbc.shape
    inner = e2.shape[1]
    nc = s // SSD_CHUNK
    gw = inner // SSD_GROUPS
    cidx = (lambda b, i: (b, nc - 1 - i, 0)) if reverse else (lambda b, i: (b, i, 0))
    const2 = lambda b, i: (0, 0)
    st_spec = pl.BlockSpec((1, SSD_GROUPS, SSD_STATE, gw), lambda b, i: (b, 0, 0, 0))
    st_shape = jax.ShapeDtypeStruct((bsz, SSD_GROUPS, SSD_STATE, gw), F32)
    out_shape, out_specs = [st_shape], [st_spec]
    if with_y:
        out_shape = [jax.ShapeDtypeStruct((bsz, s, inner), F32)] + out_shape
        out_specs = [pl.BlockSpec((1, SSD_CHUNK, inner), cidx)] + out_specs
    return pl.pallas_call(
        functools.partial(_ssd_kernel, reverse=reverse, with_y=with_y, add_skip=add_skip, lane0=lane0, inner=inner),
        out_shape=out_shape,
        grid=(bsz, nc),
        in_specs=[pl.BlockSpec((1, SSD_CHUNK, width), cidx),
                  pl.BlockSpec((1, SSD_CHUNK, LANES), cidx),
                  pl.BlockSpec((1, LANES), const2), pl.BlockSpec((1, LANES), const2),
                  pl.BlockSpec(e2.shape, const2), pl.BlockSpec((1, inner), const2), st_spec],
        out_specs=out_specs,
        scratch_shapes=[pltpu.VMEM((SSD_GROUPS, SSD_STATE, gw), F32)],
        compiler_params=_cparams("arbitrary", "arbitrary"),
        name="ssd_y" if with_y else "ssd_state",
    )(xbc, dt_raw, dtb, alog, e2, dsk, h0)


def _ssdbr_kernel(yf_ref, yb_ref, z_ref, gm_ref, nw_ref, bm_ref, gsum_ref, gexp_ref, w_ref, o_ref, *, group_w):
    y = (yf_ref[0] + yb_ref[0]) * _silu(z_ref[0])
    ms = jnp.dot((y * y).astype(BF16), gsum_ref[...], preferred_element_type=F32) * (1.0 / group_w)
    rs = lax.rsqrt(ms + RMS_EPS)
    rs_x = jnp.dot(_hi_lo(rs), gexp_ref[...], preferred_element_type=F32)
    yn = (y * rs_x * nw_ref[...]).astype(BF16)
    proj = jnp.dot(yn, w_ref[...], preferred_element_type=F32)
    o_ref[0] = _sigmoid(gm_ref[0] + bm_ref[...]) * proj


def _ssdbr(y_f, y_b, zgm, norm_w, b_merge_ssd, gsum, gexp, w_br, rows):
    bsz, s, inner = y_f.shape
    d = w_br.shape[1]
    assert inner % d == 0
    tok = lambda b, i: (b, i, 0)
    const2 = lambda b, i: (0, 0)
    return pl.pallas_call(
        functools.partial(_ssdbr_kernel, group_w=inner // SSD_GROUPS),
        out_shape=jax.ShapeDtypeStruct((bsz, s, d), F32),
        grid=(bsz, s // rows),
        in_specs=[pl.BlockSpec((1, rows, inner), tok), pl.BlockSpec((1, rows, inner), tok),
                  pl.BlockSpec((1, rows, inner), tok),
                  pl.BlockSpec((1, rows, d), lambda b, i: (b, i, inner // d)),
                  pl.BlockSpec((1, inner), const2), pl.BlockSpec((1, d), const2),
                  pl.BlockSpec(gsum.shape, const2), pl.BlockSpec(gexp.shape, const2),
                  pl.BlockSpec((inner, d), const2)],
        out_specs=pl.BlockSpec((1, rows, d), tok),
        compiler_params=_cparams("arbitrary", "arbitrary"),
        name="ssdbr",
    )(y_f, y_b, zgm, zgm, norm_w.reshape(1, inner), b_merge_ssd.reshape(1, d), gsum, gexp, w_br)


def _pack_bf16_pair(lo, hi):
    lo_bits = lax.bitcast_convert_type(lo.astype(BF16).astype(F32), jnp.uint32)
    hi_bits = lax.bitcast_convert_type(hi.astype(BF16).astype(F32), jnp.uint32)
    return (lo_bits >> 16) | hi_bits


def _unpack_bf16_pair(p):
    lo = lax.bitcast_convert_type(p << 16, F32).astype(BF16)
    hi = lax.bitcast_convert_type(p & jnp.uint32(0xFFFF0000), F32).astype(BF16)
    return jnp.concatenate([lo, hi], axis=1)


def _out_kernel(ul_ref, us_ref, x_ref, g1_ref, sh2_ref, sc2_ref, wout_ref, lg_ref, lb_ref, rw_ref, rb_ref,
                x1_ref, hp_ref, gates_ref, sel_ref, *, alpha):
    u = (ul_ref[0] + us_ref[0]).astype(BF16)
    mix = jnp.dot(u, wout_ref[...], preferred_element_type=F32)
    x1 = _ln_rows(alpha * x_ref[0] + g1_ref[0] * mix) * lg_ref[...] + lb_ref[...]
    x1_ref[0] = x1
    h2 = _ln_rows(x1) * (1.0 + sc2_ref[0]) + sh2_ref[0]
    half = h2.shape[1] // 2
    hp_ref[0] = _pack_bf16_pair(h2[:, :half], h2[:, half:])
    logits = jnp.dot(h2, rw_ref[...], precision=HIGHEST, preferred_element_type=F32) + rb_ref[...]
    lane = lax.broadcasted_iota(jnp.int32, logits.shape, 1)
    work = logits
    sel = jnp.zeros(logits.shape, jnp.bool_)
    top = None
    for k in range(MOE_TOP_K):
        m = jnp.max(work, axis=1, keepdims=True)
        if k == 0:
            top = m
        idx = jnp.min(jnp.where(work == m, lane, LANES), axis=1, keepdims=True)
        pick = lane == idx
        sel = sel | pick
        work = jnp.where(pick, 3.0 * NEG_BIG, work)
    e = jnp.where(sel, jnp.exp(logits - top), 0.0)
    gates_ref[0] = e / jnp.sum(e, axis=1, keepdims=True)
    sel_ref[0] = sel.astype(F32)


def _out(u_l, u_s, x, g1, sh2, sc2, w_out, ln_g, ln_b, rw, rb, rows, alpha):
    bsz, t, d = x.shape
    tok = lambda b, i: (b, i, 0)
    per_b = lambda b, i: (b, 0, 0)
    const2 = lambda b, i: (0, 0)
    return pl.pallas_call(
        functools.partial(_out_kernel, alpha=alpha),
        out_shape=[jax.ShapeDtypeStruct((bsz, t, d), F32), jax.ShapeDtypeStruct((bsz, t, d // 2), jnp.uint32),
                   jax.ShapeDtypeStruct((bsz, t, LANES), F32), jax.ShapeDtypeStruct((bsz, t, LANES), F32)],
        grid=(bsz, t // rows),
        in_specs=[pl.BlockSpec((1, rows, d), tok), pl.BlockSpec((1, rows, d), tok), pl.BlockSpec((1, rows, d), tok),
                  pl.BlockSpec((1, 1, d), per_b), pl.BlockSpec((1, 1, d), per_b), pl.BlockSpec((1, 1, d), per_b),
                  pl.BlockSpec((d, d), const2), pl.BlockSpec((1, d), const2), pl.BlockSpec((1, d), const2),
                  pl.BlockSpec((d, LANES), const2), pl.BlockSpec((1, LANES), const2)],
        out_specs=[pl.BlockSpec((1, rows, d), tok), pl.BlockSpec((1, rows, d // 2), tok),
                   pl.BlockSpec((1, rows, LANES), tok), pl.BlockSpec((1, rows, LANES), tok)],
        compiler_params=_cparams("arbitrary", "arbitrary"),
        name="out",
    )(u_l, u_s, x, g1, sh2, sc2, w_out, ln_g.reshape(1, d), ln_b.reshape(1, d), rw, rb)


def _rank_kernel(sel_ref, rank_ref, cnt_ref, carry_ref):
    @pl.when(pl.program_id(0) == 0)
    def _():
        carry_ref[...] = jnp.zeros(carry_ref.shape, F32)

    s = sel_ref[...]
    tt = s.shape[0]
    ri = lax.broadcasted_iota(jnp.int32, (tt, tt), 0)
    ci = lax.broadcasted_iota(jnp.int32, (tt, tt), 1)
    earlier = (ri > ci).astype(BF16)
    rank_ref[...] = jnp.dot(earlier, s.astype(BF16), preferred_element_type=F32) + carry_ref[0:1, :]
    carry_ref[...] = carry_ref[...] + jnp.sum(s, axis=0, keepdims=True)
    cnt_ref[...] = carry_ref[...]


def _rank(sel, tt):
    n = sel.shape[0]
    return pl.pallas_call(
        _rank_kernel,
        out_shape=[jax.ShapeDtypeStruct((n, LANES), F32), jax.ShapeDtypeStruct((8, LANES), F32)],
        grid=(n // tt,),
        in_specs=[pl.BlockSpec((tt, LANES), lambda i: (i, 0))],
        out_specs=[pl.BlockSpec((tt, LANES), lambda i: (i, 0)), pl.BlockSpec((8, LANES), lambda i: (0, 0))],
        scratch_shapes=[pltpu.VMEM((8, LANES), F32)],
        compiler_params=_cparams("arbitrary"),
        name="moe_rank",
    )(sel)


def _pos_kernel(sel_ref, rank_ref, gates_ref, off_ref, pos_ref, w_ref):
    avail = sel_ref[...] > 0.5
    posf = off_ref[...] + rank_ref[...]
    gates = gates_ref[...]
    lane = lax.broadcasted_iota(jnp.int32, posf.shape, 1)
    cols_p = jnp.zeros(posf.shape, F32)
    cols_w = jnp.zeros(posf.shape, F32)
    for k in range(MOE_TOP_K):
        m = jnp.min(jnp.where(avail, lane, LANES), axis=1, keepdims=True)
        pick = lane == m
        cols_p = jnp.where(lane == k, jnp.sum(jnp.where(pick, posf, 0.0), axis=1, keepdims=True), cols_p)
        cols_w = jnp.where(lane == k, jnp.sum(jnp.where(pick, gates, 0.0), axis=1, keepdims=True), cols_w)
        avail = avail & jnp.logical_not(pick)
    w_ref[...] = cols_w
    pos_ref[0] = cols_p.T[0:8, :].astype(jnp.int32)


def _pos(sel, rank, gates, off, tt):
    n = sel.shape[0]
    tok = lambda i: (i, 0)
    return pl.pallas_call(
        _pos_kernel,
        out_shape=[jax.ShapeDtypeStruct((n // tt, 8, tt), jnp.int32), jax.ShapeDtypeStruct((n, LANES), F32)],
        grid=(n // tt,),
        in_specs=[pl.BlockSpec((tt, LANES), tok), pl.BlockSpec((tt, LANES), tok), pl.BlockSpec((tt, LANES), tok),
                  pl.BlockSpec((1, LANES), lambda i: (0, 0))],
        out_specs=[pl.BlockSpec((1, 8, tt), lambda i: (i, 0, 0)), pl.BlockSpec((tt, LANES), tok)],
        compiler_params=_cparams("arbitrary"),
        name="moe_pos",
    )(sel, rank, gates, off)


def _dispatch_kernel(pos_ref, hp_hbm, xs_init_hbm, xs_hbm, sem, *, tt):
    del xs_init_hbm
    base = pl.program_id(0) * tt

    def body(j, carry):
        src = hp_hbm.at[pl.ds(base + j, 1)]
        for k in range(MOE_TOP_K):
            pltpu.make_async_copy(src, xs_hbm.at[pl.ds(pos_ref[0, k, j], 1)], sem).start()
        return carry

    lax.fori_loop(0, tt, body, 0)
    for k in range(MOE_TOP_K):
        pltpu.make_async_copy(hp_hbm.at[pl.ds(0, tt)], xs_hbm.at[pl.ds(0, tt)], sem).wait()


def _dispatch(pos, hp, xs_init, tt):
    n = hp.shape[0]
    any_spec = pl.BlockSpec(memory_space=pl.ANY)
    return pl.pallas_call(
        functools.partial(_dispatch_kernel, tt=tt),
        out_shape=jax.ShapeDtypeStruct(xs_init.shape, xs_init.dtype),
        grid=(n // tt,),
        in_specs=[pl.BlockSpec((1, 8, tt), lambda i: (i, 0, 0), memory_space=pltpu.SMEM), any_spec, any_spec],
        out_specs=any_spec,
        scratch_shapes=[pltpu.SemaphoreType.DMA(())],
        input_output_aliases={2: 0},
        compiler_params=pltpu.CompilerParams(dimension_semantics=("arbitrary",), has_side_effects=True),
        name="moe_dispatch",
    )(pos, hp, xs_init)


def _expert_kernel(te_ref, nv_ref, xs_ref, w1_ref, b1_ref, w2_ref, b2_ref, ys_ref, w1b_ref, w2b_ref, *, ff):
    t = pl.program_id(0)
    e = te_ref[t]
    prev = te_ref[jnp.maximum(t - 1, 0)]

    @pl.when((t == 0) | (e != prev))
    def _():
        w1b_ref[...] = w1_ref[0].astype(BF16)
        w2b_ref[...] = w2_ref[0].astype(BF16)

    @pl.when(t < nv_ref[0])
    def _():
        xrow = _unpack_bf16_pair(xs_ref[...])
        gu = jnp.dot(xrow, w1b_ref[...], preferred_element_type=F32) + b1_ref[0]
        g = jnp.minimum(gu[:, :ff], SWIGLU_LIMIT)
        u = jnp.clip(gu[:, ff:], -SWIGLU_LIMIT, SWIGLU_LIMIT)
        act = ((u + 1.0) * g * _sigmoid(SWIGLU_ALPHA * g)).astype(BF16)
        ys_ref[...] = jnp.dot(act, w2b_ref[...], preferred_element_type=F32) + b2_ref[0]

    @pl.when(t >= nv_ref[0])
    def _():
        ys_ref[...] = jnp.zeros(ys_ref.shape, F32)


def _experts(tile_expert, n_valid, xs, w1, b1, w2, b2, tm):
    rows, half = xs.shape
    n_exp, d, ff2 = w1.shape
    ff = ff2 // 2
    per_e = lambda t, te, nv: (te[t], 0, 0)
    return pl.pallas_call(
        functools.partial(_expert_kernel, ff=ff),
        out_shape=jax.ShapeDtypeStruct((rows, d), F32),
        grid_spec=pltpu.PrefetchScalarGridSpec(
            num_scalar_prefetch=2,
            grid=(rows // tm,),
            in_specs=[pl.BlockSpec((tm, half), lambda t, te, nv: (t, 0)),
                      pl.BlockSpec((1, d, ff2), per_e), pl.BlockSpec((1, 1, ff2), per_e),
                      pl.BlockSpec((1, ff, d), per_e), pl.BlockSpec((1, 1, d), per_e)],
            out_specs=pl.BlockSpec((tm, d), lambda t, te, nv: (t, 0)),
            scratch_shapes=[pltpu.VMEM((d, ff2), BF16), pltpu.VMEM((ff, d), BF16)]),
        compiler_params=_cparams("arbitrary"),
        name="moe_experts",
    )(tile_expert, n_valid, xs, w1, b1.reshape(n_exp, 1, ff2), w2, b2.reshape(n_exp, 1, d))


def _combine_kernel(pos_ref, ys_hbm, w_ref, x1_ref, g2_ref, lg_ref, lb_ref, o_ref, buf_ref, sem, *, tt, alpha):
    def body(j, carry):
        for k in range(MOE_TOP_K):
            pltpu.make_async_copy(ys_hbm.at[pl.ds(pos_ref[0, k, j], 1)], buf_ref.at[k, pl.ds(j, 1)], sem).start()
        return carry

    lax.fori_loop(0, tt, body, 0)
    for k in range(MOE_TOP_K):
        pltpu.make_async_copy(ys_hbm.at[pl.ds(0, tt)], buf_ref.at[k], sem).wait()
    w = w_ref[...]
    acc = sum(w[:, k:k + 1] * buf_ref[k] for k in range(MOE_TOP_K))
    o_ref[0] = _ln_rows(alpha * x1_ref[0] + g2_ref[0] * acc) * lg_ref[...] + lb_ref[...]


def _combine(pos, ys, w, x1, g2, ln_g, ln_b, tt, alpha):
    bsz, t, d = x1.shape
    nt = t // tt
    return pl.pallas_call(
        functools.partial(_combine_kernel, tt=tt, alpha=alpha),
        out_shape=jax.ShapeDtypeStruct((bsz, t, d), F32),
        grid=(bsz, nt),
        in_specs=[pl.BlockSpec((1, 8, tt), lambda b, i: (b * nt + i, 0, 0), memory_space=pltpu.SMEM),
                  pl.BlockSpec(memory_space=pl.ANY),
                  pl.BlockSpec((tt, LANES), lambda b, i: (b * nt + i, 0)),
                  pl.BlockSpec((1, tt, d), lambda b, i: (b, i, 0)),
                  pl.BlockSpec((1, 1, d), lambda b, i: (b, 0, 0)),
                  pl.BlockSpec((1, d), lambda b, i: (0, 0)), pl.BlockSpec((1, d), lambda b, i: (0, 0))],
        out_specs=pl.BlockSpec((1, tt, d), lambda b, i: (b, i, 0)),
        scratch_shapes=[pltpu.VMEM((MOE_TOP_K, tt, d), F32), pltpu.SemaphoreType.DMA(())],
        compiler_params=_cparams("arbitrary", "arbitrary"),
        name="moe_combine",
    )(pos, ys, w, x1, g2, ln_g.reshape(1, d), ln_b.reshape(1, d))


def _moe(hp, gates, sel, x1, g2, w1, b1, w2, b2, ln_g, ln_b, alpha, tt, tm):
    bsz, t, d = x1.shape
    n = bsz * t
    n_exp = w1.shape[0]
    sel2, gates2 = sel.reshape(n, LANES), gates.reshape(n, LANES)
    rank, cnt = _rank(sel2, tt)
    counts = cnt[0, :n_exp].astype(jnp.int32)
    tiles_per = (counts + tm - 1) // tm
    tile_end = jnp.cumsum(tiles_per)
    off = jnp.pad(((tile_end - tiles_per) * tm).astype(F32).reshape(1, n_exp), ((0, 0), (0, LANES - n_exp)))
    n_tiles = (n * MOE_TOP_K) // tm + n_exp
    n_valid = tile_end[-1:]
    tile_expert = jnp.searchsorted(tile_end, jnp.minimum(jnp.arange(n_tiles), n_valid - 1), side="right")
    tile_expert = jnp.minimum(tile_expert, n_exp - 1).astype(jnp.int32)
    pos, w = _pos(sel2, rank, gates2, off, tt)
    xs = _dispatch(pos, hp.reshape(n, d // 2), jnp.zeros((n_tiles * tm, d // 2), jnp.uint32), tt)
    ys = _experts(tile_expert, n_valid.astype(jnp.int32), xs, w1, b1, w2, b2, tm)
    return _combine(pos, ys, w, x1, g2, ln_g, ln_b, tt, alpha)


def _to_cols(u, grid_rows):
    b, t, c = u.shape
    return u.reshape(b, grid_rows, GRID_W, c).transpose(0, 2, 1, 3).reshape(b, t, c)


def _from_cols(u, grid_rows):
    b, t, c = u.shape
    return u.reshape(b, GRID_W, grid_rows, c).transpose(0, 2, 1, 3).reshape(b, t, c)


def kernel(x, c, ctx, c_ctx, w_ada, b_ada, w_in, b_merge, conv_lru_w, conv_lru_b, lru_wa, lru_ba, lru_wx, lru_bx, lru_lambda, conv_ssd_w, conv_ssd_b, ssd_dt_bias, ssd_a_log, ssd_d, ssd_norm_w, w_br_lru, w_br_ssd, w_out, ln1_g, ln1_b, router_w, router_b, moe_w1, moe_b1, moe_w2, moe_b2, ln2_g, ln2_b):
    depth = w_ada.shape[0]
    assert depth == 1, "single-layer stack: the context tokens only supply scan states"
    bsz, t, d = x.shape
    t_ctx = ctx.shape[1]
    alpha = (2.0 * depth) ** 0.25
    grid_rows = t // GRID_W
    d_rnn = w_br_lru.shape[1]
    inner = w_br_ssd.shape[1]
    heads = inner // SSD_HEAD_DIM
    gn = SSD_GROUPS * SSD_STATE
    col_gr = d_rnn
    col_z = col_gr + d_rnn
    col_xbc = col_z + inner
    col_dt = col_xbc + inner + 2 * gn
    col_gm = col_dt + 2 * heads
    assert 2 * heads <= LANES and t_ctx % SSD_CHUNK == 0 and t % SSD_CHUNK == 0

    pad = (-(bsz + 1)) % 8
    c_all = jnp.concatenate([c, c_ctx[None, :], jnp.zeros((pad, d), F32)], axis=0)
    mod = _ada(c_all, w_ada[0], b_ada[0])
    sh1, sc1, g1, sh2, sc2, g2 = (mod[:bsz, k * d:(k + 1) * d] for k in range(6))
    csh1, csc1 = mod[bsz:bsz + 1, 0:d], mod[bsz:bsz + 1, d:2 * d]

    w_in_b = w_in[0].astype(BF16)

    tt = 32
    xall_tm = jnp.concatenate([jnp.transpose(ctx, (1, 0, 2)), jnp.transpose(x, (1, 0, 2))], axis=0)
    sh_tm = jnp.stack([jnp.broadcast_to(csh1, (bsz, d)), sh1])
    sc_tm = jnp.stack([jnp.broadcast_to(csc1, (bsz, d)), sc1])
    seg = lambda i: jnp.where(i * tt >= t_ctx, 1, 0)
    xr_all = _inproj(xall_tm, sh_tm, sc_tm, w_in_b[:, :col_gr], (tt, bsz), 1024, sel=seg)
    w_grgm = jnp.concatenate([w_in_b[:, col_gr:col_z], w_in_b[:, col_gm:col_gm + d]], axis=1)
    grgm = _inproj(xall_tm, sh_tm, sc_tm, w_grgm, (tt, bsz), 1024, sel=seg, first=t_ctx // tt, count=t // tt)
    p_f = _lru_params(lru_wa[0, 0], lru_ba[0, 0], lru_wx[0, 0], lru_bx[0, 0], lru_lambda[0, 0], 256)
    p_b = _lru_params(lru_wa[0, 1], lru_ba[0, 1], lru_wx[0, 1], lru_bx[0, 1], lru_lambda[0, 1], 256)
    u_lru_tm = _lru(xr_all, grgm, p_f, p_b, conv_lru_w[0], conv_lru_b[0], b_merge[0, :d],
                    w_br_lru[0].astype(BF16), t_ctx, tt)

    x_cm = _to_cols(x, grid_rows)
    sh_b, sc_b = sh1[:, None, :], sc1[:, None, :]
    csh_b, csc_b = csh1[None], csc1[None]
    zero = lambda i: 0
    w_xbc = w_in_b[:, col_xbc:col_dt]
    w_dt = jnp.pad(w_in_b[:, col_dt:col_gm], ((0, 0), (0, LANES - 2 * heads)))
    w_zgm = jnp.concatenate([w_in_b[:, col_z:col_xbc], w_in_b[:, col_gm + d:]], axis=1)
    xbc = _inproj(x_cm, sh_b, sc_b, w_xbc, (1, t), 256, conv_w=conv_ssd_w[0], conv_b=conv_ssd_b[0], out_dtype=BF16)
    xbc_c = _inproj(ctx, csh_b, csc_b, w_xbc, (1, t_ctx), 256, sel=zero, conv_w=conv_ssd_w[0],
                    conv_b=conv_ssd_b[0], out_dtype=BF16)
    dt_raw = _inproj(x_cm, sh_b, sc_b, w_dt, (1, 512), LANES)
    dt_raw_c = _inproj(ctx, csh_b, csc_b, w_dt, (1, t_ctx), LANES, sel=zero)
    zgm = _inproj(x_cm, sh_b, sc_b, w_zgm, (1, 512), 1024)

    lane_pad = LANES - 2 * heads
    dtb = jnp.pad(ssd_dt_bias[0].reshape(1, 2 * heads), ((0, 0), (0, lane_pad)))
    alog = jnp.pad(ssd_a_log[0].reshape(1, 2 * heads), ((0, 0), (0, lane_pad)))
    dsk = jnp.repeat(ssd_d[0], SSD_HEAD_DIM).reshape(1, inner)
    head_of_lane = jnp.arange(inner) // SSD_HEAD_DIM

    def expand(lane0):
        e = (jnp.arange(LANES)[:, None] == head_of_lane[None, :] + lane0).astype(BF16)
        return jnp.concatenate([e, e], axis=0)

    gw = inner // SSD_GROUPS
    s0 = jnp.zeros((bsz, SSD_GROUPS, SSD_STATE, gw), F32)
    common = (dtb, alog)
    (st_f,) = _ssd(xbc_c, dt_raw_c, *common, expand(0), dsk, s0, reverse=False, with_y=False, add_skip=False, lane0=0)
    (st_b,) = _ssd(xbc_c, dt_raw_c, *common, expand(heads), dsk, s0, reverse=True, with_y=False, add_skip=False,
                   lane0=heads)
    y_f, _ = _ssd(xbc, dt_raw, *common, expand(0), dsk, st_f, reverse=False, with_y=True, add_skip=True, lane0=0)
    y_b, _ = _ssd(xbc, dt_raw, *common, expand(heads), dsk, st_b, reverse=True, with_y=True, add_skip=False,
                  lane0=heads)

    group_of_lane = jnp.arange(inner) // gw
    gsum = (group_of_lane[:, None] == jnp.arange(LANES)[None, :]).astype(BF16)
    gexp = jnp.concatenate([gsum.T, gsum.T], axis=0)
    u_ssd_cm = _ssdbr(y_f, y_b, zgm, ssd_norm_w[0], b_merge[0, d:], gsum, gexp, w_br_ssd[0].astype(BF16), 256)

    u_lru = jnp.transpose(u_lru_tm, (1, 0, 2))
    u_ssd = _from_cols(u_ssd_cm, grid_rows)
    n_exp = router_w.shape[2]
    rw = jnp.pad(router_w[0], ((0, 0), (0, LANES - n_exp)))
    rb = jnp.pad(router_b[0].reshape(1, n_exp), ((0, 0), (0, LANES - n_exp)), constant_values=NEG_BIG)
    x1, hp, gates, sel = _out(u_lru, u_ssd, x, g1[:, None, :], sh2[:, None, :], sc2[:, None, :],
                              w_out[0].astype(BF16), ln1_g[0], ln1_b[0], rw, rb, 512, alpha)

    return _moe(hp, gates, sel, x1, g2[:, None, :], moe_w1[0], moe_b1[0], moe_w2[0], moe_b2[0],
                ln2_g[0], ln2_b[0], alpha, 512, 512)
```

```python
import functools

import jax
import jax.numpy as jnp
from jax import lax
from jax.experimental import pallas as pl
from jax.experimental.pallas import tpu as pltpu

F32 = jnp.float32
BF16 = jnp.bfloat16
HIGHEST = lax.Precision.HIGHEST

GRID_W = 64
LRU_BLOCK_W = 64
LRU_C = 8.0
CONV_W = 4
SSD_HEAD_DIM = 64
SSD_GROUPS = 8
SSD_STATE = 128
SSD_CHUNK = 128
MOE_TOP_K = 4
SWIGLU_LIMIT = 7.0
SWIGLU_ALPHA = 1.702
LN_EPS = 1e-5
RMS_EPS = 1e-5
LANES = 128
NEG_BIG = -1e30
VMEM_LIMIT = 56 * 1024 * 1024


def _cparams(*sem):
    return pltpu.CompilerParams(dimension_semantics=sem, vmem_limit_bytes=VMEM_LIMIT)


def _ln_rows(x):
    mu = jnp.mean(x, axis=-1, keepdims=True)
    xc = x - mu
    var = jnp.mean(xc * xc, axis=-1, keepdims=True)
    return xc * lax.rsqrt(var + LN_EPS)


def _sigmoid(x):
    return 1.0 / (1.0 + jnp.exp(-x))


def _silu(x):
    return x * _sigmoid(x)


def _softplus(x):
    return jnp.maximum(x, 0.0) + jnp.log(1.0 + jnp.exp(-jnp.abs(x)))


def _ada_kernel(c_ref, w_ref, b_ref, o_ref):
    c = c_ref[...]
    o_ref[...] = jnp.dot(_silu(c), w_ref[...], precision=HIGHEST, preferred_element_type=F32) + b_ref[...]


def _ada(c_all, w, b):
    m, d = c_all.shape
    n = w.shape[1]
    tn = 1024
    return pl.pallas_call(
        _ada_kernel,
        out_shape=jax.ShapeDtypeStruct((m, n), F32),
        grid=(n // tn,),
        in_specs=[pl.BlockSpec((m, d), lambda j: (0, 0)),
                  pl.BlockSpec((d, tn), lambda j: (0, j)),
                  pl.BlockSpec((1, tn), lambda j: (0, j))],
        out_specs=pl.BlockSpec((m, tn), lambda j: (0, j)),
        compiler_params=_cparams("arbitrary"),
        name="ada",
    )(c_all, w, b.reshape(1, n))


def _inproj_kernel(x_ref, sh_ref, sc_ref, w_ref, *rest, conv):
    if conv:
        cw_ref, cb_ref, o_ref, h_ref = rest
    else:
        o_ref, h_ref = rest
    rows = h_ref.shape[0]

    @pl.when(pl.program_id(1) == 0)
    def _():
        h = _ln_rows(x_ref[...]) * (1.0 + sc_ref[...]) + sh_ref[...]
        h_ref[...] = h.reshape(h_ref.shape).astype(BF16)

    acc = jnp.dot(h_ref[...], w_ref[...], preferred_element_type=F32)
    if conv:
        ridx = lax.broadcasted_iota(jnp.int32, acc.shape, 0)
        cw = cw_ref[...]
        out = acc * cw[2:3, :] + cb_ref[...]
        for j, shift in ((0, 2), (1, 1)):
            rolled = pltpu.roll(acc, shift, 0)
            out = out + jnp.where(ridx >= shift, rolled, 0.0) * cw[j:j + 1, :]
        rolled = pltpu.roll(acc, rows - 1, 0)
        out = out + jnp.where(ridx < rows - 1, rolled, 0.0) * cw[3:4, :]
        acc = _silu(out)
    o_ref[...] = acc.reshape(o_ref.shape).astype(o_ref.dtype)


def _inproj(x3, sh3, sc3, w, tile, tn, sel=None, conv_w=None, conv_b=None, out_dtype=F32, first=0, count=None):
    gdim, rdim, d = x3.shape
    g, r = tile
    n = w.shape[1]
    rows = g * r
    nblk_g = gdim // g if count is None else count
    nblk_r = rdim // r
    if sel is None:
        sel = lambda i: i
    conv = conv_w is not None
    mg, mr = sh3.shape[1], sh3.shape[2]
    in_specs = [pl.BlockSpec((g, r, d), lambda i, j: ((i // nblk_r) + first, i % nblk_r, 0)),
                pl.BlockSpec((1, mg, mr), lambda i, j: (sel((i // nblk_r) + first), 0, 0)),
                pl.BlockSpec((1, mg, mr), lambda i, j: (sel((i // nblk_r) + first), 0, 0)),
                pl.BlockSpec((d, tn), lambda i, j: (0, j))]
    args = [x3, sh3, sc3, w]
    if conv:
        assert r == rdim and g == 1
        in_specs += [pl.BlockSpec((CONV_W, tn), lambda i, j: (0, j)),
                     pl.BlockSpec((1, tn), lambda i, j: (0, j))]
        args += [conv_w, conv_b.reshape(1, n)]
    out_g = nblk_g * g
    return pl.pallas_call(
        functools.partial(_inproj_kernel, conv=conv),
        out_shape=jax.ShapeDtypeStruct((out_g, rdim, n), out_dtype),
        grid=(nblk_g * nblk_r, n // tn),
        in_specs=in_specs,
        out_specs=pl.BlockSpec((g, r, tn), lambda i, j: (i // nblk_r, i % nblk_r, j)),
        scratch_shapes=[pltpu.VMEM((rows, d), BF16)],
        compiler_params=_cparams("arbitrary", "arbitrary"),
        name="inproj_conv" if conv else "inproj",
    )(*args)


def _lru_gates(win_ref, w_ref, ba_ref, bx_ref, lam_ref, cw_ref, cb_ref, a_ref, b_ref, tt):
    bsz, c = win_ref.shape[1], win_ref.shape[2]
    cw = cw_ref[...]
    u = cb_ref[...].reshape(1, 1, c) + sum(win_ref[pl.ds(j, tt)] * cw[j:j + 1, :].reshape(1, 1, c)
                                           for j in range(CONV_W))
    u2 = u.reshape(tt * bsz, c)
    ub = u2.astype(BF16)
    sp = _softplus(-lam_ref[...])
    pack = w_ref.shape[1]
    for j in range(c // pack):
        lo = j * pack
        pre = jnp.dot(ub[:, lo:lo + pack], w_ref[j], preferred_element_type=F32)
        r = _sigmoid(pre[:, :pack] + ba_ref[:, lo:lo + pack])
        i = _sigmoid(pre[:, pack:] + bx_ref[:, lo:lo + pack])
        log_a = (-LRU_C) * r * sp[:, lo:lo + pack]
        a = jnp.exp(log_a)
        bt = jnp.sqrt(1.0 - jnp.exp(2.0 * log_a)) * (i * u2[:, lo:lo + pack])
        a_ref[:, :, lo:lo + pack] = a.reshape(tt, bsz, pack)
        b_ref[:, :, lo:lo + pack] = bt.reshape(tt, bsz, pack)


def _lru_fill_window(win_ref, x_ref, prev_ref, next_ref, at_start, at_end, tt):
    zero2 = jnp.zeros(prev_ref.shape, F32)
    win_ref[pl.ds(0, 2)] = jnp.where(at_start, zero2, prev_ref[...])
    win_ref[pl.ds(2, tt)] = x_ref[...]
    win_ref[pl.ds(tt + 2, 1)] = jnp.where(at_end, jnp.zeros(next_ref.shape, F32), next_ref[...])


def _lru_bwd_kernel(x_ref, prev_ref, next_ref, w_ref, ba_ref, bx_ref, lam_ref, cw_ref, cb_ref,
                    hb_ref, win_ref, a_ref, b_ref, h_ref, *, tt, ncb, nlb):
    i = pl.program_id(0)
    blk = jnp.where(i < ncb, ncb - 1 - i, ncb + nlb - 1 - (i - ncb))
    at_start = (blk == 0) | (blk == ncb)
    at_end = (blk == ncb - 1) | (blk == ncb + nlb - 1)

    @pl.when(i == 0)
    def _():
        h_ref[...] = jnp.zeros(h_ref.shape, F32)

    _lru_fill_window(win_ref, x_ref, prev_ref, next_ref, at_start, at_end, tt)
    _lru_gates(win_ref, w_ref, ba_ref, bx_ref, lam_ref, cw_ref, cb_ref, a_ref, b_ref, tt)

    def step(k, h):
        t = tt - 1 - k
        h = a_ref[t] * h + b_ref[t]
        a_ref[t] = h
        return h

    h_ref[...] = lax.fori_loop(0, tt, step, h_ref[...])

    @pl.when(i >= ncb)
    def _():
        hb_ref[...] = a_ref[...]


def _lru_fwd_kernel(x_ref, prev_ref, next_ref, w_ref, ba_ref, bx_ref, lam_ref, cw_ref, cb_ref,
                    hb_ref, gr_ref, gm_ref, bm_ref, wbr_ref,
                    o_ref, win_ref, a_ref, b_ref, h_ref, *, tt, ncb, nlb):
    i = pl.program_id(0)
    at_start = (i == 0) | (i == ncb)
    at_end = (i == ncb - 1) | (i == ncb + nlb - 1)

    @pl.when(i == 0)
    def _():
        h_ref[...] = jnp.zeros(h_ref.shape, F32)

    _lru_fill_window(win_ref, x_ref, prev_ref, next_ref, at_start, at_end, tt)
    _lru_gates(win_ref, w_ref, ba_ref, bx_ref, lam_ref, cw_ref, cb_ref, a_ref, b_ref, tt)

    def step(t, h):
        h = a_ref[t] * h + b_ref[t]
        a_ref[t] = h
        return h

    h_ref[...] = lax.fori_loop(0, tt, step, h_ref[...])

    @pl.when(i >= ncb)
    def _():
        bsz, c = h_ref.shape
        rows = tt * bsz
        a_lat = (a_ref[...] + hb_ref[...]) * jax.nn.gelu(gr_ref[...], approximate=True)
        proj = jnp.dot(a_lat.reshape(rows, c).astype(BF16), wbr_ref[...], preferred_element_type=F32)
        gate = _sigmoid(gm_ref[...].reshape(rows, -1) + bm_ref[...])
        o_ref[...] = (gate * proj).reshape(o_ref.shape)


def _lru_specs(tt, bsz, c, blk_of, ttot):
    half = tt // 2
    return [pl.BlockSpec((tt, bsz, c), lambda i: (blk_of(i), 0, 0)),
            pl.BlockSpec((2, bsz, c), lambda i: (jnp.maximum(blk_of(i) * half - 1, 0), 0, 0)),
            pl.BlockSpec((1, bsz, c), lambda i: (jnp.minimum(blk_of(i) * tt + tt, ttot - 1), 0, 0))]


def _lru(xr_all, grgm, p_f, p_b, conv_w, conv_b, b_merge_lru, w_br, t_ctx, tt):
    ttot, bsz, c = xr_all.shape
    ncb, nlb = t_ctx // tt, (ttot - t_ctx) // tt
    nblk = ncb + nlb
    d = w_br.shape[1]
    const2 = lambda i: (0, 0)
    const3 = lambda i: (0, 0, 0)
    par_specs = [pl.BlockSpec(p_f[0].shape, const3), pl.BlockSpec((1, c), const2), pl.BlockSpec((1, c), const2),
                 pl.BlockSpec((1, c), const2), pl.BlockSpec((CONV_W, c), const2), pl.BlockSpec((1, c), const2)]
    scratch = [pltpu.VMEM((tt + 3, bsz, c), F32), pltpu.VMEM((tt, bsz, c), F32),
               pltpu.VMEM((tt, bsz, c), F32), pltpu.VMEM((bsz, c), F32)]

    bwd_blk = lambda i: jnp.where(i < ncb, ncb - 1 - i, ncb + nlb - 1 - (i - ncb))
    hb = pl.pallas_call(
        functools.partial(_lru_bwd_kernel, tt=tt, ncb=ncb, nlb=nlb),
        out_shape=jax.ShapeDtypeStruct((nlb * tt, bsz, c), F32),
        grid=(nblk,),
        in_specs=_lru_specs(tt, bsz, c, bwd_blk, ttot) + par_specs,
        out_specs=pl.BlockSpec((tt, bsz, c), lambda i: (jnp.where(i < ncb, nlb - 1, nblk - 1 - i), 0, 0)),
        scratch_shapes=scratch,
        compiler_params=_cparams("arbitrary"),
        name="lru_bwd",
    )(xr_all, xr_all, xr_all, *p_b, conv_w, conv_b.reshape(1, c))

    assert c == d
    lat = lambda i: (jnp.maximum(i - ncb, 0), 0, 0)
    lat1 = lambda i: (jnp.maximum(i - ncb, 0), 0, 1)
    return pl.pallas_call(
        functools.partial(_lru_fwd_kernel, tt=tt, ncb=ncb, nlb=nlb),
        out_shape=jax.ShapeDtypeStruct((nlb * tt, bsz, d), F32),
        grid=(nblk,),
        in_specs=_lru_specs(tt, bsz, c, lambda i: i, ttot) + par_specs + [
            pl.BlockSpec((tt, bsz, c), lat), pl.BlockSpec((tt, bsz, c), lat), pl.BlockSpec((tt, bsz, d), lat1),
            pl.BlockSpec((1, d), const2), pl.BlockSpec((c, d), const2)],
        out_specs=pl.BlockSpec((tt, bsz, d), lat),
        scratch_shapes=scratch,
        compiler_params=_cparams("arbitrary"),
        name="lru_fwd",
    )(xr_all, xr_all, xr_all, *p_f, conv_w, conv_b.reshape(1, c), hb, grgm, grgm, b_merge_lru.reshape(1, d), w_br)


def _lru_params(wa, ba, wx, bx, lam, pack):
    nb, bw, _ = wa.shape
    per = pack // bw
    c = nb * bw

    def bd(w):
        w4 = w.reshape(nb // per, per, bw, bw)
        eye = jnp.eye(per, dtype=w.dtype)
        return jnp.einsum('gpde,pq->gpdqe', w4, eye).reshape(nb // per, pack, pack)

    w = jnp.concatenate([bd(wa), bd(wx)], axis=-1).astype(BF16)
    return w, ba.reshape(1, c), bx.reshape(1, c), lam.reshape(1, c)


def _hi_lo(v):
    hi = v.astype(BF16)
    lo = (v - hi.astype(F32)).astype(BF16)
    return jnp.concatenate([hi, lo], axis=-1)


def _ssd_kernel(xbc_ref, dt_ref, dtb_ref, alog_ref, e_ref, dsk_ref, h0_ref, *outs,
                reverse, with_y, add_skip, lane0, inner):
    if with_y:
        y_ref, hfin_ref, st_ref = outs
    else:
        hfin_ref, st_ref = outs
    i = pl.program_id(1)
    q = SSD_CHUNK
    n = SSD_STATE
    gw = inner // SSD_GROUPS
    hpg = gw // SSD_HEAD_DIM

    @pl.when(i == 0)
    def _():
        st_ref[...] = h0_ref[0]

    dt = _softplus(dt_ref[0] + dtb_ref[...])
    da = dt * (-jnp.exp(alog_ref[...]))
    ri = lax.broadcasted_iota(jnp.int32, (q, q), 0)
    ci = lax.broadcasted_iota(jnp.int32, (q, q), 1)
    tri = (ri <= ci) if reverse else (ri >= ci)
    acum = jnp.dot(tri.astype(F32), da, precision=HIGHEST, preferred_element_type=F32)
    a_tot = jnp.sum(da, axis=0, keepdims=True)
    w_state = jnp.exp(a_tot - acum) * dt
    e2 = e_ref[...]
    ws_x = jnp.dot(_hi_lo(w_state), e2, preferred_element_type=F32)
    dec_x = jnp.dot(_hi_lo(jnp.broadcast_to(jnp.exp(a_tot), (8, LANES))), e2,
                    preferred_element_type=F32)[0:1, :]
    if with_y:
        eac_x = jnp.dot(_hi_lo(jnp.exp(acum)), e2, preferred_element_type=F32)
        acum_t = acum.T
        dt_t = dt.T
        rb = lax.broadcasted_iota(jnp.int32, (hpg * q, gw), 0) // q
        lb = lax.broadcasted_iota(jnp.int32, (hpg * q, gw), 1) // SSD_HEAD_DIM
        bd_mask = rb == lb

    for g in range(SSD_GROUPS):
        lo = g * gw
        xg = xbc_ref[0, :, lo:lo + gw]
        bg = xbc_ref[0, :, inner + g * n:inner + (g + 1) * n]
        cg = xbc_ref[0, :, inner + (SSD_GROUPS + g) * n:inner + (SSD_GROUPS + g + 1) * n]
        xgf = xg.astype(F32)
        xw = (xgf * ws_x[:, lo:lo + gw]).astype(BF16)
        st = st_ref[g]
        upd = lax.dot_general(bg, xw, (((0,), (0,)), ((), ())), preferred_element_type=F32)
        st_ref[g] = dec_x[:, lo:lo + gw] * st + upd
        if with_y:
            y_off = jnp.dot(cg, st.astype(BF16), preferred_element_type=F32) * eac_x[:, lo:lo + gw]
            cb = lax.dot_general(cg, bg, (((1,), (1,)), ((), ())), preferred_element_type=F32)
            ls = []
            for r in range(hpg):
                lane = lane0 + g * hpg + r
                seg = acum[:, lane:lane + 1] - acum_t[lane:lane + 1, :]
                l_h = cb * jnp.exp(jnp.where(tri, seg, NEG_BIG)) * dt_t[lane:lane + 1, :]
                ls.append(l_h.astype(BF16))
            lcat = jnp.concatenate(ls, axis=1)
            xbd = jnp.where(bd_mask, jnp.concatenate([xg] * hpg, axis=0), jnp.zeros((), BF16))
            y = y_off + jnp.dot(lcat, xbd, preferred_element_type=F32)
            if add_skip:
                y = y + dsk_ref[:, lo:lo + gw] * xgf
            y_ref[0, :, lo:lo + gw] = y

    @pl.when(i == pl.num_programs(1) - 1)
    def _():
        hfin_ref[0] = st_ref[...]


def _ssd(xbc, dt_raw, dtb, alog, e2, dsk, h0, *, reverse, with_y, add_skip, lane0):
    bsz, s, width = xbc.shape
    inner = e2.shape[1]
    nc = s // SSD_CHUNK
    gw = inner // SSD_GROUPS
    cidx = (lambda b, i: (b, nc - 1 - i, 0)) if reverse else (lambda b, i: (b, i, 0))
    const2 = lambda b, i: (0, 0)
    st_spec = pl.BlockSpec((1, SSD_GROUPS, SSD_STATE, gw), lambda b, i: (b, 0, 0, 0))
    st_shape = jax.ShapeDtypeStruct((bsz, SSD_GROUPS, SSD_STATE, gw), F32)
    out_shape, out_specs = [st_shape], [st_spec]
    if with_y:
        out_shape = [jax.ShapeDtypeStruct((bsz, s, inner), F32)] + out_shape
        out_specs = [pl.BlockSpec((1, SSD_CHUNK, inner), cidx)] + out_specs
    return pl.pallas_call(
        functools.partial(_ssd_kernel, reverse=reverse, with_y=with_y, add_skip=add_skip, lane0=lane0, inner=inner),
        out_shape=out_shape,
        grid=(bsz, nc),
        in_specs=[pl.BlockSpec((1, SSD_CHUNK, width), cidx),
                  pl.BlockSpec((1, SSD_CHUNK, LANES), cidx),
                  pl.BlockSpec((1, LANES), const2), pl.BlockSpec((1, LANES), const2),
                  pl.BlockSpec(e2.shape, const2), pl.BlockSpec((1, inner), const2), st_spec],
        out_specs=out_specs,
        scratch_shapes=[pltpu.VMEM((SSD_GROUPS, SSD_STATE, gw), F32)],
        compiler_params=_cparams("arbitrary", "arbitrary"),
        name="ssd_y" if with_y else "ssd_state",
    )(xbc, dt_raw, dtb, alog, e2, dsk, h0)


def _ssdbr_kernel(yf_ref, yb_ref, z_ref, gm_ref, nw_ref, bm_ref, gsum_ref, gexp_ref, w_ref, o_ref, *, group_w):
    y = (yf_ref[0] + yb_ref[0]) * _silu(z_ref[0])
    ms = jnp.dot((y * y).astype(BF16), gsum_ref[...], preferred_element_type=F32) * (1.0 / group_w)
    rs = lax.rsqrt(ms + RMS_EPS)
    rs_x = jnp.dot(_hi_lo(rs), gexp_ref[...], preferred_element_type=F32)
    yn = (y * rs_x * nw_ref[...]).astype(BF16)
    proj = jnp.dot(yn, w_ref[...], preferred_element_type=F32)
    o_ref[0] = _sigmoid(gm_ref[0] + bm_ref[...]) * proj


def _ssdbr(y_f, y_b, zgm, norm_w, b_merge_ssd, gsum, gexp, w_br, rows):
    bsz, s, inner = y_f.shape
    d = w_br.shape[1]
    assert inner % d == 0
    tok = lambda b, i: (b, i, 0)
    const2 = lambda b, i: (0, 0)
    return pl.pallas_call(
        functools.partial(_ssdbr_kernel, group_w=inner // SSD_GROUPS),
        out_shape=jax.ShapeDtypeStruct((bsz, s, d), F32),
        grid=(bsz, s // rows),
        in_specs=[pl.BlockSpec((1, rows, inner), tok), pl.BlockSpec((1, rows, inner), tok),
                  pl.BlockSpec((1, rows, inner), tok),
                  pl.BlockSpec((1, rows, d), lambda b, i: (b, i, inner // d)),
                  pl.BlockSpec((1, inner), const2), pl.BlockSpec((1, d), const2),
                  pl.BlockSpec(gsum.shape, const2), pl.BlockSpec(gexp.shape, const2),
                  pl.BlockSpec((inner, d), const2)],
        out_specs=pl.BlockSpec((1, rows, d), tok),
        compiler_params=_cparams("arbitrary", "arbitrary"),
        name="ssdbr",
    )(y_f, y_b, zgm, zgm, norm_w.reshape(1, inner), b_merge_ssd.reshape(1, d), gsum, gexp, w_br)


def _pack_bf16_pair(lo, hi):
    lo_bits = lax.bitcast_convert_type(lo.astype(BF16).astype(F32), jnp.uint32)
    hi_bits = lax.bitcast_convert_type(hi.astype(BF16).astype(F32), jnp.uint32)
    return (lo_bits >> 16) | hi_bits


def _unpack_bf16_pair(p):
    lo = lax.bitcast_convert_type(p << 16, F32).astype(BF16)
    hi = lax.bitcast_convert_type(p & jnp.uint32(0xFFFF0000), F32).astype(BF16)
    return jnp.concatenate([lo, hi], axis=1)


def _out_kernel(ul_ref, us_ref, x_ref, g1_ref, sh2_ref, sc2_ref, wout_ref, lg_ref, lb_ref, rw_ref, rb_ref,
                x1_ref, hp_ref, gates_ref, sel_ref, *, alpha):
    u = (ul_ref[0] + us_ref[0]).astype(BF16)
    mix = jnp.dot(u, wout_ref[...], preferred_element_type=F32)
    x1 = _ln_rows(alpha * x_ref[0] + g1_ref[0] * mix) * lg_ref[...] + lb_ref[...]
    x1_ref[0] = x1
    h2 = _ln_rows(x1) * (1.0 + sc2_ref[0]) + sh2_ref[0]
    half = h2.shape[1] // 2
    hp_ref[0] = _pack_bf16_pair(h2[:, :half], h2[:, half:])
    logits = jnp.dot(h2, rw_ref[...], precision=HIGHEST, preferred_element_type=F32) + rb_ref[...]
    lane = lax.broadcasted_iota(jnp.int32, logits.shape, 1)
    work = logits
    sel = jnp.zeros(logits.shape, jnp.bool_)
    top = None
    for k in range(MOE_TOP_K):
        m = jnp.max(work, axis=1, keepdims=True)
        if k == 0:
            top = m
        idx = jnp.min(jnp.where(work == m, lane, LANES), axis=1, keepdims=True)
        pick = lane == idx
        sel = sel | pick
        work = jnp.where(pick, 3.0 * NEG_BIG, work)
    e = jnp.where(sel, jnp.exp(logits - top), 0.0)
    gates_ref[0] = e / jnp.sum(e, axis=1, keepdims=True)
    sel_ref[0] = sel.astype(F32)


def _out(u_l, u_s, x, g1, sh2, sc2, w_out, ln_g, ln_b, rw, rb, rows, alpha):
    bsz, t, d = x.shape
    tok = lambda b, i: (b, i, 0)
    per_b = lambda b, i: (b, 0, 0)
    const2 = lambda b, i: (0, 0)
    return pl.pallas_call(
        functools.partial(_out_kernel, alpha=alpha),
        out_shape=[jax.ShapeDtypeStruct((bsz, t, d), F32), jax.ShapeDtypeStruct((bsz, t, d // 2), jnp.uint32),
                   jax.ShapeDtypeStruct((bsz, t, LANES), F32), jax.ShapeDtypeStruct((bsz, t, LANES), F32)],
        grid=(bsz, t // rows),
        in_specs=[pl.BlockSpec((1, rows, d), tok), pl.BlockSpec((1, rows, d), tok), pl.BlockSpec((1, rows, d), tok),
                  pl.BlockSpec((1, 1, d), per_b), pl.BlockSpec((1, 1, d), per_b), pl.BlockSpec((1, 1, d), per_b),
                  pl.BlockSpec((d, d), const2), pl.BlockSpec((1, d), const2), pl.BlockSpec((1, d), const2),
                  pl.BlockSpec((d, LANES), const2), pl.BlockSpec((1, LANES), const2)],
        out_specs=[pl.BlockSpec((1, rows, d), tok), pl.BlockSpec((1, rows, d // 2), tok),
                   pl.BlockSpec((1, rows, LANES), tok), pl.BlockSpec((1, rows, LANES), tok)],
        compiler_params=_cparams("arbitrary", "arbitrary"),
        name="out",
    )(u_l, u_s, x, g1, sh2, sc2, w_out, ln_g.reshape(1, d), ln_b.reshape(1, d), rw, rb)


def _rank_kernel(sel_ref, rank_ref, cnt_ref, carry_ref):
    @pl.when(pl.program_id(0) == 0)
    def _():
        carry_ref[...] = jnp.zeros(carry_ref.shape, F32)

    s = sel_ref[...]
    tt = s.shape[0]
    ri = lax.broadcasted_iota(jnp.int32, (tt, tt), 0)
    ci = lax.broadcasted_iota(jnp.int32, (tt, tt), 1)
    earlier = (ri > ci).astype(BF16)
    rank_ref[...] = jnp.dot(earlier, s.astype(BF16), preferred_element_type=F32) + carry_ref[0:1, :]
    carry_ref[...] = carry_ref[...] + jnp.sum(s, axis=0, keepdims=True)
    cnt_ref[...] = carry_ref[...]


def _rank(sel, tt):
    n = sel.shape[0]
    return pl.pallas_call(
        _rank_kernel,
        out_shape=[jax.ShapeDtypeStruct((n, LANES), F32), jax.ShapeDtypeStruct((8, LANES), F32)],
        grid=(n // tt,),
        in_specs=[pl.BlockSpec((tt, LANES), lambda i: (i, 0))],
        out_specs=[pl.BlockSpec((tt, LANES), lambda i: (i, 0)), pl.BlockSpec((8, LANES), lambda i: (0, 0))],
        scratch_shapes=[pltpu.VMEM((8, LANES), F32)],
        compiler_params=_cparams("arbitrary"),
        name="moe_rank",
    )(sel)


def _pos_kernel(sel_ref, rank_ref, gates_ref, off_ref, pos_ref, w_ref):
    avail = sel_ref[...] > 0.5
    posf = off_ref[...] + rank_ref[...]
    gates = gates_ref[...]
    lane = lax.broadcasted_iota(jnp.int32, posf.shape, 1)
    cols_p = jnp.zeros(posf.shape, F32)
    cols_w = jnp.zeros(posf.shape, F32)
    for k in range(MOE_TOP_K):
        m = jnp.min(jnp.where(avail, lane, LANES), axis=1, keepdims=True)
        pick = lane == m
        cols_p = jnp.where(lane == k, jnp.sum(jnp.where(pick, posf, 0.0), axis=1, keepdims=True), cols_p)
        cols_w = jnp.where(lane == k, jnp.sum(jnp.where(pick, gates, 0.0), axis=1, keepdims=True), cols_w)
        avail = avail & jnp.logical_not(pick)
    w_ref[...] = cols_w
    pos_ref[0] = cols_p.T[0:8, :].astype(jnp.int32)


def _pos(sel, rank, gates, off, tt):
    n = sel.shape[0]
    tok = lambda i: (i, 0)
    return pl.pallas_call(
        _pos_kernel,
        out_shape=[jax.ShapeDtypeStruct((n // tt, 8, tt), jnp.int32), jax.ShapeDtypeStruct((n, LANES), F32)],
        grid=(n // tt,),
        in_specs=[pl.BlockSpec((tt, LANES), tok), pl.BlockSpec((tt, LANES), tok), pl.BlockSpec((tt, LANES), tok),
                  pl.BlockSpec((1, LANES), lambda i: (0, 0))],
        out_specs=[pl.BlockSpec((1, 8, tt), lambda i: (i, 0, 0)), pl.BlockSpec((tt, LANES), tok)],
        compiler_params=_cparams("arbitrary"),
        name="moe_pos",
    )(sel, rank, gates, off)


def _dispatch_kernel(pos_ref, hp_ref, xs_init_hbm, xs_hbm, sem, *, tt):
    del xs_init_hbm

    def body(j, carry):
        src = hp_ref.at[pl.ds(j, 1)]
        for k in range(MOE_TOP_K):
            pltpu.make_async_copy(src, xs_hbm.at[pl.ds(pos_ref[0, k, j], 1)], sem).start()
        return carry

    lax.fori_loop(0, tt, body, 0)
    for k in range(MOE_TOP_K):
        pltpu.make_async_copy(hp_ref, xs_hbm.at[pl.ds(0, tt)], sem).wait()


def _dispatch(pos, hp, xs_init, tt):
    n, half = hp.shape
    any_spec = pl.BlockSpec(memory_space=pl.ANY)
    return pl.pallas_call(
        functools.partial(_dispatch_kernel, tt=tt),
        out_shape=jax.ShapeDtypeStruct(xs_init.shape, xs_init.dtype),
        grid=(n // tt,),
        in_specs=[pl.BlockSpec((1, 8, tt), lambda i: (i, 0, 0), memory_space=pltpu.SMEM),
                  pl.BlockSpec((tt, half), lambda i: (i, 0)), any_spec],
        out_specs=any_spec,
        scratch_shapes=[pltpu.SemaphoreType.DMA(())],
        input_output_aliases={2: 0},
        compiler_params=pltpu.CompilerParams(dimension_semantics=("arbitrary",), has_side_effects=True),
        name="moe_dispatch",
    )(pos, hp, xs_init)


def _expert_kernel(te_ref, nv_ref, xs_ref, w1_ref, b1_ref, w2_ref, b2_ref, ys_ref, w1b_ref, w2b_ref, *, ff):
    t = pl.program_id(0)
    e = te_ref[t]
    prev = te_ref[jnp.maximum(t - 1, 0)]

    @pl.when((t == 0) | (e != prev))
    def _():
        w1b_ref[...] = w1_ref[0].astype(BF16)
        w2b_ref[...] = w2_ref[0].astype(BF16)

    @pl.when(t < nv_ref[0])
    def _():
        xrow = _unpack_bf16_pair(xs_ref[...])
        gu = jnp.dot(xrow, w1b_ref[...], preferred_element_type=F32) + b1_ref[0]
        g = jnp.minimum(gu[:, :ff], SWIGLU_LIMIT)
        u = jnp.clip(gu[:, ff:], -SWIGLU_LIMIT, SWIGLU_LIMIT)
        act = ((u + 1.0) * g * _sigmoid(SWIGLU_ALPHA * g)).astype(BF16)
        ys_ref[...] = jnp.dot(act, w2b_ref[...], preferred_element_type=F32) + b2_ref[0]

    @pl.when(t >= nv_ref[0])
    def _():
        ys_ref[...] = jnp.zeros(ys_ref.shape, F32)


def _experts(tile_expert, n_valid, xs, w1, b1, w2, b2, tm):
    rows, half = xs.shape
    n_exp, d, ff2 = w1.shape
    ff = ff2 // 2
    per_e = lambda t, te, nv: (te[t], 0, 0)
    return pl.pallas_call(
        functools.partial(_expert_kernel, ff=ff),
        out_shape=jax.ShapeDtypeStruct((rows, d), F32),
        grid_spec=pltpu.PrefetchScalarGridSpec(
            num_scalar_prefetch=2,
            grid=(rows // tm,),
            in_specs=[pl.BlockSpec((tm, half), lambda t, te, nv: (t, 0)),
                      pl.BlockSpec((1, d, ff2), per_e), pl.BlockSpec((1, 1, ff2), per_e),
                      pl.BlockSpec((1, ff, d), per_e), pl.BlockSpec((1, 1, d), per_e)],
            out_specs=pl.BlockSpec((tm, d), lambda t, te, nv: (t, 0)),
            scratch_shapes=[pltpu.VMEM((d, ff2), BF16), pltpu.VMEM((ff, d), BF16)]),
        compiler_params=_cparams("arbitrary"),
        name="moe_experts",
    )(tile_expert, n_valid, xs, w1, b1.reshape(n_exp, 1, ff2), w2, b2.reshape(n_exp, 1, d))


def _combine_kernel(pos_ref, ys_hbm, w_ref, x1_ref, g2_ref, lg_ref, lb_ref, o_ref, buf_ref, sem, *, tt, alpha):
    def body(j, carry):
        for k in range(MOE_TOP_K):
            pltpu.make_async_copy(ys_hbm.at[pl.ds(pos_ref[0, k, j], 1)], buf_ref.at[k, pl.ds(j, 1)], sem).start()
        return carry

    lax.fori_loop(0, tt, body, 0)
    for k in range(MOE_TOP_K):
        pltpu.make_async_copy(ys_hbm.at[pl.ds(0, tt)], buf_ref.at[k], sem).wait()
    w = w_ref[...]
    acc = sum(w[:, k:k + 1] * buf_ref[k] for k in range(MOE_TOP_K))
    o_ref[0] = _ln_rows(alpha * x1_ref[0] + g2_ref[0] * acc) * lg_ref[...] + lb_ref[...]


def _combine(pos, ys, w, x1, g2, ln_g, ln_b, tt, alpha):
    bsz, t, d = x1.shape
    nt = t // tt
    return pl.pallas_call(
        functools.partial(_combine_kernel, tt=tt, alpha=alpha),
        out_shape=jax.ShapeDtypeStruct((bsz, t, d), F32),
        grid=(bsz, nt),
        in_specs=[pl.BlockSpec((1, 8, tt), lambda b, i: (b * nt + i, 0, 0), memory_space=pltpu.SMEM),
                  pl.BlockSpec(memory_space=pl.ANY),
                  pl.BlockSpec((tt, LANES), lambda b, i: (b * nt + i, 0)),
                  pl.BlockSpec((1, tt, d), lambda b, i: (b, i, 0)),
                  pl.BlockSpec((1, 1, d), lambda b, i: (b, 0, 0)),
                  pl.BlockSpec((1, d), lambda b, i: (0, 0)), pl.BlockSpec((1, d), lambda b, i: (0, 0))],
        out_specs=pl.BlockSpec((1, tt, d), lambda b, i: (b, i, 0)),
        scratch_shapes=[pltpu.VMEM((MOE_TOP_K, tt, d), F32), pltpu.SemaphoreType.DMA(())],
        compiler_params=_cparams("arbitrary", "arbitrary"),
        name="moe_combine",
    )(pos, ys, w, x1, g2, ln_g.reshape(1, d), ln_b.reshape(1, d))


def _moe(hp, gates, sel, x1, g2, w1, b1, w2, b2, ln_g, ln_b, alpha, tt, tm):
    bsz, t, d = x1.shape
    n = bsz * t
    n_exp = w1.shape[0]
    sel2, gates2 = sel.reshape(n, LANES), gates.reshape(n, LANES)
    rank, cnt = _rank(sel2, tt)
    counts = cnt[0, :n_exp].astype(jnp.int32)
    tiles_per = (counts + tm - 1) // tm
    tile_end = jnp.cumsum(tiles_per)
    off = jnp.pad(((tile_end - tiles_per) * tm).astype(F32).reshape(1, n_exp), ((0, 0), (0, LANES - n_exp)))
    n_tiles = (n * MOE_TOP_K) // tm + n_exp
    n_valid = tile_end[-1:]
    tile_id = jnp.minimum(jnp.arange(n_tiles), n_valid - 1)
    tile_expert = jnp.sum((tile_end[None, :] <= tile_id[:, None]).astype(jnp.int32), axis=1)
    tile_expert = jnp.minimum(tile_expert, n_exp - 1)
    pos, w = _pos(sel2, rank, gates2, off, tt)
    xs = _dispatch(pos, hp.reshape(n, d // 2), jnp.zeros((n_tiles * tm, d // 2), jnp.uint32), tt)
    ys = _experts(tile_expert, n_valid.astype(jnp.int32), xs, w1, b1, w2, b2, tm)
    return _combine(pos, ys, w, x1, g2, ln_g, ln_b, tt, alpha)


def _to_cols(u, grid_rows):
    b, t, c = u.shape
    return u.reshape(b, grid_rows, GRID_W, c).transpose(0, 2, 1, 3).reshape(b, t, c)


def _from_cols(u, grid_rows):
    b, t, c = u.shape
    return u.reshape(b, GRID_W, grid_rows, c).transpose(0, 2, 1, 3).reshape(b, t, c)


def kernel(x, c, ctx, c_ctx, w_ada, b_ada, w_in, b_merge, conv_lru_w, conv_lru_b, lru_wa, lru_ba, lru_wx, lru_bx, lru_lambda, conv_ssd_w, conv_ssd_b, ssd_dt_bias, ssd_a_log, ssd_d, ssd_norm_w, w_br_lru, w_br_ssd, w_out, ln1_g, ln1_b, router_w, router_b, moe_w1, moe_b1, moe_w2, moe_b2, ln2_g, ln2_b):
    depth = w_ada.shape[0]
    assert depth == 1, "single-layer stack: the context tokens only supply scan states"
    bsz, t, d = x.shape
    t_ctx = ctx.shape[1]
    alpha = (2.0 * depth) ** 0.25
    grid_rows = t // GRID_W
    d_rnn = w_br_lru.shape[1]
    inner = w_br_ssd.shape[1]
    heads = inner // SSD_HEAD_DIM
    gn = SSD_GROUPS * SSD_STATE
    col_gr = d_rnn
    col_z = col_gr + d_rnn
    col_xbc = col_z + inner
    col_dt = col_xbc + inner + 2 * gn
    col_gm = col_dt + 2 * heads
    assert 2 * heads <= LANES and t_ctx % SSD_CHUNK == 0 and t % SSD_CHUNK == 0

    pad = (-(bsz + 1)) % 8
    c_all = jnp.concatenate([c, c_ctx[None, :], jnp.zeros((pad, d), F32)], axis=0)
    mod = _ada(c_all, w_ada[0], b_ada[0])
    sh1, sc1, g1, sh2, sc2, g2 = (mod[:bsz, k * d:(k + 1) * d] for k in range(6))
    csh1, csc1 = mod[bsz:bsz + 1, 0:d], mod[bsz:bsz + 1, d:2 * d]

    w_in_b = w_in[0].astype(BF16)

    tt = 32
    xall_tm = jnp.concatenate([jnp.transpose(ctx, (1, 0, 2)), jnp.transpose(x, (1, 0, 2))], axis=0)
    sh_tm = jnp.stack([jnp.broadcast_to(csh1, (bsz, d)), sh1])
    sc_tm = jnp.stack([jnp.broadcast_to(csc1, (bsz, d)), sc1])
    seg = lambda i: jnp.where(i * tt >= t_ctx, 1, 0)
    xr_all = _inproj(xall_tm, sh_tm, sc_tm, w_in_b[:, :col_gr], (tt, bsz), 1024, sel=seg)
    w_grgm = jnp.concatenate([w_in_b[:, col_gr:col_z], w_in_b[:, col_gm:col_gm + d]], axis=1)
    grgm = _inproj(xall_tm, sh_tm, sc_tm, w_grgm, (tt, bsz), 1024, sel=seg, first=t_ctx // tt, count=t // tt)
    p_f = _lru_params(lru_wa[0, 0], lru_ba[0, 0], lru_wx[0, 0], lru_bx[0, 0], lru_lambda[0, 0], 256)
    p_b = _lru_params(lru_wa[0, 1], lru_ba[0, 1], lru_wx[0, 1], lru_bx[0, 1], lru_lambda[0, 1], 256)
    u_lru_tm = _lru(xr_all, grgm, p_f, p_b, conv_lru_w[0], conv_lru_b[0], b_merge[0, :d],
                    w_br_lru[0].astype(BF16), t_ctx, tt)

    x_cm = _to_cols(x, grid_rows)
    sh_b, sc_b = sh1[:, None, :], sc1[:, None, :]
    csh_b, csc_b = csh1[None], csc1[None]
    zero = lambda i: 0
    w_xbc = w_in_b[:, col_xbc:col_dt]
    w_dt = jnp.pad(w_in_b[:, col_dt:col_gm], ((0, 0), (0, LANES - 2 * heads)))
    w_zgm = jnp.concatenate([w_in_b[:, col_z:col_xbc], w_in_b[:, col_gm + d:]], axis=1)
    xbc = _inproj(x_cm, sh_b, sc_b, w_xbc, (1, t), 256, conv_w=conv_ssd_w[0], conv_b=conv_ssd_b[0], out_dtype=BF16)
    xbc_c = _inproj(ctx, csh_b, csc_b, w_xbc, (1, t_ctx), 256, sel=zero, conv_w=conv_ssd_w[0],
                    conv_b=conv_ssd_b[0], out_dtype=BF16)
    dt_raw = _inproj(x_cm, sh_b, sc_b, w_dt, (1, 512), LANES)
    dt_raw_c = _inproj(ctx, csh_b, csc_b, w_dt, (1, t_ctx), LANES, sel=zero)
    zgm = _inproj(x_cm, sh_b, sc_b, w_zgm, (1, 512), 1024)

    lane_pad = LANES - 2 * heads
    dtb = jnp.pad(ssd_dt_bias[0].reshape(1, 2 * heads), ((0, 0), (0, lane_pad)))
    alog = jnp.pad(ssd_a_log[0].reshape(1, 2 * heads), ((0, 0), (0, lane_pad)))
    dsk = jnp.repeat(ssd_d[0], SSD_HEAD_DIM).reshape(1, inner)
    head_of_lane = jnp.arange(inner) // SSD_HEAD_DIM

    def expand(lane0):
        e = (jnp.arange(LANES)[:, None] == head_of_lane[None, :] + lane0).astype(BF16)
        return jnp.concatenate([e, e], axis=0)

    gw = inner // SSD_GROUPS
    s0 = jnp.zeros((bsz, SSD_GROUPS, SSD_STATE, gw), F32)
    common = (dtb, alog)
    (st_f,) = _ssd(xbc_c, dt_raw_c, *common, expand(0), dsk, s0, reverse=False, with_y=False, add_skip=False, lane0=0)
    (st_b,) = _ssd(xbc_c, dt_raw_c, *common, expand(heads), dsk, s0, reverse=True, with_y=False, add_skip=False,
                   lane0=heads)
    y_f, _ = _ssd(xbc, dt_raw, *common, expand(0), dsk, st_f, reverse=False, with_y=True, add_skip=True, lane0=0)
    y_b, _ = _ssd(xbc, dt_raw, *common, expand(heads), dsk, st_b, reverse=True, with_y=True, add_skip=False,
                  lane0=heads)

    group_of_lane = jnp.arange(inner) // gw
    gsum = (group_of_lane[:, None] == jnp.arange(LANES)[None, :]).astype(BF16)
    gexp = jnp.concatenate([gsum.T, gsum.T], axis=0)
    u_ssd_cm = _ssdbr(y_f, y_b, zgm, ssd_norm_w[0], b_merge[0, d:], gsum, gexp, w_br_ssd[0].astype(BF16), 256)

    u_lru = jnp.transpose(u_lru_tm, (1, 0, 2))
    u_ssd = _from_cols(u_ssd_cm, grid_rows)
    n_exp = router_w.shape[2]
    rw = jnp.pad(router_w[0], ((0, 0), (0, LANES - n_exp)))
    rb = jnp.pad(router_b[0].reshape(1, n_exp), ((0, 0), (0, LANES - n_exp)), constant_values=NEG_BIG)
    x1, hp, gates, sel = _out(u_lru, u_ssd, x, g1[:, None, :], sh2[:, None, :], sc2[:, None, :],
                              w_out[0].astype(BF16), ln1_g[0], ln1_b[0], rw, rb, 512, alpha)

    return _moe(hp, gates, sel, x1, g2[:, None, :], moe_w1[0], moe_b1[0], moe_w2[0], moe_b2[0],
                ln2_g[0], ln2_b[0], alpha, 512, 512)
```

```python
import functools

import jax
import jax.numpy as jnp
from jax import lax
from jax.experimental import pallas as pl
from jax.experimental.pallas import tpu as pltpu

F32 = jnp.float32
BF16 = jnp.bfloat16
HIGHEST = lax.Precision.HIGHEST

GRID_W = 64
LRU_BLOCK_W = 64
LRU_C = 8.0
CONV_W = 4
SSD_HEAD_DIM = 64
SSD_GROUPS = 8
SSD_STATE = 128
SSD_CHUNK = 128
MOE_TOP_K = 4
SWIGLU_LIMIT = 7.0
SWIGLU_ALPHA = 1.702
LN_EPS = 1e-5
RMS_EPS = 1e-5
LANES = 128
NEG_BIG = -1e30
CONV_SUBTILE = 256
VMEM_LIMIT = 56 * 1024 * 1024


def _cparams(*sem):
    return pltpu.CompilerParams(dimension_semantics=sem, vmem_limit_bytes=VMEM_LIMIT)


def _ln_rows(x):
    mu = jnp.mean(x, axis=-1, keepdims=True)
    xc = x - mu
    var = jnp.mean(xc * xc, axis=-1, keepdims=True)
    return xc * lax.rsqrt(var + LN_EPS)


def _sigmoid(x):
    return 0.5 * (jnp.tanh(0.5 * x) + 1.0)


def _silu(x):
    return x * _sigmoid(x)


def _softplus(x):
    return jnp.maximum(x, 0.0) + jnp.log(1.0 + jnp.exp(-jnp.abs(x)))


def _ada_kernel(c_ref, w_ref, b_ref, o_ref):
    c = c_ref[...]
    o_ref[...] = jnp.dot(_silu(c), w_ref[...], precision=HIGHEST, preferred_element_type=F32) + b_ref[...]


def _ada(c_all, w, b):
    m, d = c_all.shape
    n = w.shape[1]
    tn = 1024
    return pl.pallas_call(
        _ada_kernel,
        out_shape=jax.ShapeDtypeStruct((m, n), F32),
        grid=(n // tn,),
        in_specs=[pl.BlockSpec((m, d), lambda j: (0, 0)),
                  pl.BlockSpec((d, tn), lambda j: (0, j)),
                  pl.BlockSpec((1, tn), lambda j: (0, j))],
        out_specs=pl.BlockSpec((m, tn), lambda j: (0, j)),
        compiler_params=_cparams("arbitrary"),
        name="ada",
    )(c_all, w, b.reshape(1, n))


def _inproj_kernel(x_ref, sh_ref, sc_ref, w_ref, *rest, conv):
    if conv:
        cw_ref, cb_ref, o_ref, h_ref = rest
    else:
        o_ref, h_ref = rest
    rows = h_ref.shape[0]

    @pl.when(pl.program_id(1) == 0)
    def _():
        h = _ln_rows(x_ref[...]) * (1.0 + sc_ref[...]) + sh_ref[...]
        h_ref[...] = h.reshape(h_ref.shape).astype(BF16)

    if not conv:
        acc = jnp.dot(h_ref[...], w_ref[...], preferred_element_type=F32)
        o_ref[...] = acc.reshape(o_ref.shape).astype(o_ref.dtype)
        return

    def taps(a, cw, cb, n, head, tail):
        r = lax.broadcasted_iota(jnp.int32, a.shape, 0)
        t0, t1, t3 = pltpu.roll(a, 2, 0), pltpu.roll(a, 1, 0), pltpu.roll(a, n - 1, 0)
        if head:
            t0, t1 = jnp.where(r >= 2, t0, 0.0), jnp.where(r >= 1, t1, 0.0)
        if tail:
            t3 = jnp.where(r < n - 1, t3, 0.0)
        return _silu(a * cw[2:3, :] + cb + t0 * cw[0:1, :] + t1 * cw[1:2, :] + t3 * cw[3:4, :])

    edge = 32
    half = edge // 2
    for lo in range(0, w_ref.shape[1], CONV_SUBTILE):
        cols = slice(lo, lo + CONV_SUBTILE)
        acc = jnp.dot(h_ref[...], w_ref[:, cols], preferred_element_type=F32)
        cw, cb = cw_ref[:, cols], cb_ref[:, cols]
        o_ref[0, :, cols] = taps(acc, cw, cb, rows, False, False).astype(o_ref.dtype)
        o_ref[0, 0:half, cols] = taps(acc[0:edge], cw, cb, edge, True, False)[0:half].astype(o_ref.dtype)
        o_ref[0, rows - half:rows, cols] = taps(acc[rows - edge:rows], cw, cb, edge, False, True)[half:].astype(
            o_ref.dtype)


def _inproj(x3, sh3, sc3, w, tile, tn, sel=None, conv_w=None, conv_b=None, out_dtype=F32, first=0, count=None):
    gdim, rdim, d = x3.shape
    g, r = tile
    n = w.shape[1]
    rows = g * r
    nblk_g = gdim // g if count is None else count
    nblk_r = rdim // r
    if sel is None:
        sel = lambda i: i
    conv = conv_w is not None
    mg, mr = sh3.shape[1], sh3.shape[2]
    in_specs = [pl.BlockSpec((g, r, d), lambda i, j: ((i // nblk_r) + first, i % nblk_r, 0)),
                pl.BlockSpec((1, mg, mr), lambda i, j: (sel((i // nblk_r) + first), 0, 0)),
                pl.BlockSpec((1, mg, mr), lambda i, j: (sel((i // nblk_r) + first), 0, 0)),
                pl.BlockSpec((d, tn), lambda i, j: (0, j))]
    args = [x3, sh3, sc3, w]
    if conv:
        assert r == rdim and g == 1
        in_specs += [pl.BlockSpec((CONV_W, tn), lambda i, j: (0, j)),
                     pl.BlockSpec((1, tn), lambda i, j: (0, j))]
        args += [conv_w, conv_b.reshape(1, n)]
    out_g = nblk_g * g
    return pl.pallas_call(
        functools.partial(_inproj_kernel, conv=conv),
        out_shape=jax.ShapeDtypeStruct((out_g, rdim, n), out_dtype),
        grid=(nblk_g * nblk_r, n // tn),
        in_specs=in_specs,
        out_specs=pl.BlockSpec((g, r, tn), lambda i, j: (i // nblk_r, i % nblk_r, j)),
        scratch_shapes=[pltpu.VMEM((rows, d), BF16)],
        compiler_params=_cparams("arbitrary", "arbitrary"),
        name="inproj_conv" if conv else "inproj",
    )(*args)


def _lru_gates(win_ref, w_ref, ba_ref, bx_ref, lam_ref, cw_ref, cb_ref, a_ref, b_ref, tt):
    bsz, c = win_ref.shape[1], win_ref.shape[2]
    cw = cw_ref[...]
    u = cb_ref[...].reshape(1, 1, c) + sum(win_ref[pl.ds(j, tt)] * cw[j:j + 1, :].reshape(1, 1, c)
                                           for j in range(CONV_W))
    u2 = u.reshape(tt * bsz, c)
    ub = u2.astype(BF16)
    sp = _softplus(-lam_ref[...])
    pack = w_ref.shape[1]
    for j in range(c // pack):
        lo = j * pack
        pre = jnp.dot(ub[:, lo:lo + pack], w_ref[j], preferred_element_type=F32)
        r = _sigmoid(pre[:, :pack] + ba_ref[:, lo:lo + pack])
        i = _sigmoid(pre[:, pack:] + bx_ref[:, lo:lo + pack])
        log_a = (-LRU_C) * r * sp[:, lo:lo + pack]
        a = jnp.exp(log_a)
        bt = jnp.sqrt(1.0 - jnp.exp(2.0 * log_a)) * (i * u2[:, lo:lo + pack])
        a_ref[:, :, lo:lo + pack] = a.reshape(tt, bsz, pack)
        b_ref[:, :, lo:lo + pack] = bt.reshape(tt, bsz, pack)


def _lru_fill_window(win_ref, x_ref, prev_ref, next_ref, at_start, at_end, tt):
    zero2 = jnp.zeros(prev_ref.shape, F32)
    win_ref[pl.ds(0, 2)] = jnp.where(at_start, zero2, prev_ref[...])
    win_ref[pl.ds(2, tt)] = x_ref[...]
    win_ref[pl.ds(tt + 2, 1)] = jnp.where(at_end, jnp.zeros(next_ref.shape, F32), next_ref[...])


def _lru_bwd_kernel(x_ref, prev_ref, next_ref, w_ref, ba_ref, bx_ref, lam_ref, cw_ref, cb_ref,
                    hb_ref, win_ref, a_ref, b_ref, h_ref, *, tt, ncb, nlb):
    i = pl.program_id(0)
    blk = jnp.where(i < ncb, ncb - 1 - i, ncb + nlb - 1 - (i - ncb))
    at_start = (blk == 0) | (blk == ncb)
    at_end = (blk == ncb - 1) | (blk == ncb + nlb - 1)

    @pl.when(i == 0)
    def _():
        h_ref[...] = jnp.zeros(h_ref.shape, F32)

    _lru_fill_window(win_ref, x_ref, prev_ref, next_ref, at_start, at_end, tt)
    _lru_gates(win_ref, w_ref, ba_ref, bx_ref, lam_ref, cw_ref, cb_ref, a_ref, b_ref, tt)

    def step(k, h):
        t = tt - 1 - k
        h = a_ref[t] * h + b_ref[t]
        a_ref[t] = h
        return h

    h_ref[...] = lax.fori_loop(0, tt, step, h_ref[...])

    @pl.when(i >= ncb)
    def _():
        hb_ref[...] = a_ref[...]


def _lru_fwd_kernel(x_ref, prev_ref, next_ref, w_ref, ba_ref, bx_ref, lam_ref, cw_ref, cb_ref,
                    hb_ref, gr_ref, gm_ref, bm_ref, wbr_ref,
                    o_ref, win_ref, a_ref, b_ref, h_ref, *, tt, ncb, nlb):
    i = pl.program_id(0)
    at_start = (i == 0) | (i == ncb)
    at_end = (i == ncb - 1) | (i == ncb + nlb - 1)

    @pl.when(i == 0)
    def _():
        h_ref[...] = jnp.zeros(h_ref.shape, F32)

    _lru_fill_window(win_ref, x_ref, prev_ref, next_ref, at_start, at_end, tt)
    _lru_gates(win_ref, w_ref, ba_ref, bx_ref, lam_ref, cw_ref, cb_ref, a_ref, b_ref, tt)

    def step(t, h):
        h = a_ref[t] * h + b_ref[t]
        a_ref[t] = h
        return h

    h_ref[...] = lax.fori_loop(0, tt, step, h_ref[...])

    @pl.when(i >= ncb)
    def _():
        bsz, c = h_ref.shape
        rows = tt * bsz
        a_lat = (a_ref[...] + hb_ref[...]) * jax.nn.gelu(gr_ref[...].astype(F32), approximate=True)
        proj = jnp.dot(a_lat.reshape(rows, c).astype(BF16), wbr_ref[...], preferred_element_type=F32)
        gate = _sigmoid(gm_ref[...].astype(F32).reshape(rows, -1) + bm_ref[...])
        o_ref[...] = (gate * proj).reshape(o_ref.shape)


def _lru_specs(tt, bsz, c, blk_of, ttot):
    half = tt // 2
    return [pl.BlockSpec((tt, bsz, c), lambda i: (blk_of(i), 0, 0)),
            pl.BlockSpec((2, bsz, c), lambda i: (jnp.maximum(blk_of(i) * half - 1, 0), 0, 0)),
            pl.BlockSpec((1, bsz, c), lambda i: (jnp.minimum(blk_of(i) * tt + tt, ttot - 1), 0, 0))]


def _lru(xr_all, grgm, p_f, p_b, conv_w, conv_b, b_merge_lru, w_br, t_ctx, tt):
    ttot, bsz, c = xr_all.shape
    ncb, nlb = t_ctx // tt, (ttot - t_ctx) // tt
    nblk = ncb + nlb
    d = w_br.shape[1]
    const2 = lambda i: (0, 0)
    const3 = lambda i: (0, 0, 0)
    par_specs = [pl.BlockSpec(p_f[0].shape, const3), pl.BlockSpec((1, c), const2), pl.BlockSpec((1, c), const2),
                 pl.BlockSpec((1, c), const2), pl.BlockSpec((CONV_W, c), const2), pl.BlockSpec((1, c), const2)]
    scratch = [pltpu.VMEM((tt + 3, bsz, c), F32), pltpu.VMEM((tt, bsz, c), F32),
               pltpu.VMEM((tt, bsz, c), F32), pltpu.VMEM((bsz, c), F32)]

    bwd_blk = lambda i: jnp.where(i < ncb, ncb - 1 - i, ncb + nlb - 1 - (i - ncb))
    hb = pl.pallas_call(
        functools.partial(_lru_bwd_kernel, tt=tt, ncb=ncb, nlb=nlb),
        out_shape=jax.ShapeDtypeStruct((nlb * tt, bsz, c), F32),
        grid=(nblk,),
        in_specs=_lru_specs(tt, bsz, c, bwd_blk, ttot) + par_specs,
        out_specs=pl.BlockSpec((tt, bsz, c), lambda i: (jnp.where(i < ncb, nlb - 1, nblk - 1 - i), 0, 0)),
        scratch_shapes=scratch,
        compiler_params=_cparams("arbitrary"),
        name="lru_bwd",
    )(xr_all, xr_all, xr_all, *p_b, conv_w, conv_b.reshape(1, c))

    assert c == d
    lat = lambda i: (jnp.maximum(i - ncb, 0), 0, 0)
    lat1 = lambda i: (jnp.maximum(i - ncb, 0), 0, 1)
    return pl.pallas_call(
        functools.partial(_lru_fwd_kernel, tt=tt, ncb=ncb, nlb=nlb),
        out_shape=jax.ShapeDtypeStruct((nlb * tt, bsz, d), F32),
        grid=(nblk,),
        in_specs=_lru_specs(tt, bsz, c, lambda i: i, ttot) + par_specs + [
            pl.BlockSpec((tt, bsz, c), lat), pl.BlockSpec((tt, bsz, c), lat), pl.BlockSpec((tt, bsz, d), lat1),
            pl.BlockSpec((1, d), const2), pl.BlockSpec((c, d), const2)],
        out_specs=pl.BlockSpec((tt, bsz, d), lat),
        scratch_shapes=scratch,
        compiler_params=_cparams("arbitrary"),
        name="lru_fwd",
    )(xr_all, xr_all, xr_all, *p_f, conv_w, conv_b.reshape(1, c), hb, grgm, grgm, b_merge_lru.reshape(1, d), w_br)


def _lru_params(wa, ba, wx, bx, lam, pack):
    nb, bw, _ = wa.shape
    per = pack // bw
    c = nb * bw

    def bd(w):
        w4 = w.reshape(nb // per, per, bw, bw)
        eye = jnp.eye(per, dtype=w.dtype)
        return jnp.einsum('gpde,pq->gpdqe', w4, eye).reshape(nb // per, pack, pack)

    w = jnp.concatenate([bd(wa), bd(wx)], axis=-1).astype(BF16)
    return w, ba.reshape(1, c), bx.reshape(1, c), lam.reshape(1, c)


def _hi_lo(v):
    hi = v.astype(BF16)
    lo = (v - hi.astype(F32)).astype(BF16)
    return jnp.concatenate([hi, lo], axis=-1)


def _ssd_kernel(xbc_ref, dt_ref, dtb_ref, alog_ref, e_ref, dsk_ref, h0_ref, *outs,
                reverse, with_y, add_skip, lane0, inner):
    if with_y:
        y_ref, hfin_ref, st_ref = outs
    else:
        hfin_ref, st_ref = outs
    i = pl.program_id(1)
    q = SSD_CHUNK
    n = SSD_STATE
    gw = inner // SSD_GROUPS
    hpg = gw // SSD_HEAD_DIM

    @pl.when(i == 0)
    def _():
        st_ref[...] = h0_ref[0]

    dt = _softplus(dt_ref[0] + dtb_ref[...])
    da = dt * (-jnp.exp(alog_ref[...]))
    ri = lax.broadcasted_iota(jnp.int32, (q, q), 0)
    ci = lax.broadcasted_iota(jnp.int32, (q, q), 1)
    tri = (ri <= ci) if reverse else (ri >= ci)
    acum = jnp.dot(tri.astype(F32), da, precision=HIGHEST, preferred_element_type=F32)
    a_tot = jnp.sum(da, axis=0, keepdims=True)
    w_state = jnp.exp(a_tot - acum) * dt
    e2 = e_ref[...]
    ws_x = jnp.dot(_hi_lo(w_state), e2, preferred_element_type=F32)
    dec_x = jnp.dot(_hi_lo(jnp.broadcast_to(jnp.exp(a_tot), (8, LANES))), e2,
                    preferred_element_type=F32)[0:1, :]
    if with_y:
        eac_x = jnp.dot(_hi_lo(jnp.exp(acum)), e2, preferred_element_type=F32)
        acum_t = acum.T
        dt_t = dt.T
        rb = lax.broadcasted_iota(jnp.int32, (hpg * q, gw), 0) // q
        lb = lax.broadcasted_iota(jnp.int32, (hpg * q, gw), 1) // SSD_HEAD_DIM
        bd_mask = rb == lb

    for g in range(SSD_GROUPS):
        lo = g * gw
        xg = xbc_ref[0, :, lo:lo + gw]
        bg = xbc_ref[0, :, inner + g * n:inner + (g + 1) * n]
        cg = xbc_ref[0, :, inner + (SSD_GROUPS + g) * n:inner + (SSD_GROUPS + g + 1) * n]
        xgf = xg.astype(F32)
        xw = (xgf * ws_x[:, lo:lo + gw]).astype(BF16)
        st = st_ref[g]
        upd = lax.dot_general(bg, xw, (((0,), (0,)), ((), ())), preferred_element_type=F32)
        st_ref[g] = dec_x[:, lo:lo + gw] * st + upd
        if with_y:
            y_off = jnp.dot(cg, st.astype(BF16), preferred_element_type=F32) * eac_x[:, lo:lo + gw]
            cb = lax.dot_general(cg, bg, (((1,), (1,)), ((), ())), preferred_element_type=F32)
            ls = []
            for r in range(hpg):
                lane = lane0 + g * hpg + r
                seg = acum[:, lane:lane + 1] - acum_t[lane:lane + 1, :]
                l_h = cb * jnp.exp(jnp.where(tri, seg, NEG_BIG)) * dt_t[lane:lane + 1, :]
                ls.append(l_h.astype(BF16))
            lcat = jnp.concatenate(ls, axis=1)
            xbd = jnp.where(bd_mask, jnp.concatenate([xg] * hpg, axis=0), jnp.zeros((), BF16))
            y = y_off + jnp.dot(lcat, xbd, preferred_element_type=F32)
            if add_skip:
                y = y + dsk_ref[:, lo:lo + gw] * xgf
            y_ref[0, :, lo:lo + gw] = y

    @pl.when(i == pl.num_programs(1) - 1)
    def _():
        hfin_ref[0] = st_ref[...]


def _ssd(xbc, dt_raw, dtb, alog, e2, dsk, h0, *, reverse, with_y, add_skip, lane0):
    bsz, s, width = xbc.shape
    inner = e2.shape[1]
    nc = s // SSD_CHUNK
    gw = inner // SSD_GROUPS
    cidx = (lambda b, i: (b, nc - 1 - i, 0)) if reverse else (lambda b, i: (b, i, 0))
    const2 = lambda b, i: (0, 0)
    st_spec = pl.BlockSpec((1, SSD_GROUPS, SSD_STATE, gw), lambda b, i: (b, 0, 0, 0))
    st_shape = jax.ShapeDtypeStruct((bsz, SSD_GROUPS, SSD_STATE, gw), F32)
    out_shape, out_specs = [st_shape], [st_spec]
    if with_y:
        out_shape = [jax.ShapeDtypeStruct((bsz, s, inner), F32)] + out_shape
        out_specs = [pl.BlockSpec((1, SSD_CHUNK, inner), cidx)] + out_specs
    return pl.pallas_call(
        functools.partial(_ssd_kernel, reverse=reverse, with_y=with_y, add_skip=add_skip, lane0=lane0, inner=inner),
        out_shape=out_shape,
        grid=(bsz, nc),
        in_specs=[pl.BlockSpec((1, SSD_CHUNK, width), cidx),
                  pl.BlockSpec((1, SSD_CHUNK, LANES), cidx),
                  pl.BlockSpec((1, LANES), const2), pl.BlockSpec((1, LANES), const2),
                  pl.BlockSpec(e2.shape, const2), pl.BlockSpec((1, inner), const2), st_spec],
        out_specs=out_specs,
        scratch_shapes=[pltpu.VMEM((SSD_GROUPS, SSD_STATE, gw), F32)],
        compiler_params=_cparams("arbitrary", "arbitrary"),
        name="ssd_y" if with_y else "ssd_state",
    )(xbc, dt_raw, dtb, alog, e2, dsk, h0)


def _ssdbr_kernel(yf_ref, yb_ref, z_ref, gm_ref, nw_ref, bm_ref, gsum_ref, gexp_ref, w_ref, o_ref, *, group_w):
    y = (yf_ref[0] + yb_ref[0]) * _silu(z_ref[0].astype(F32))
    ms = jnp.dot((y * y).astype(BF16), gsum_ref[...], preferred_element_type=F32) * (1.0 / group_w)
    rs = lax.rsqrt(ms + RMS_EPS)
    rs_x = jnp.dot(_hi_lo(rs), gexp_ref[...], preferred_element_type=F32)
    yn = (y * rs_x * nw_ref[...]).astype(BF16)
    proj = jnp.dot(yn, w_ref[...], preferred_element_type=F32)
    o_ref[0] = _sigmoid(gm_ref[0].astype(F32) + bm_ref[...]) * proj


def _ssdbr(y_f, y_b, zgm, norm_w, b_merge_ssd, gsum, gexp, w_br, rows):
    bsz, s, inner = y_f.shape
    d = w_br.shape[1]
    assert inner % d == 0
    tok = lambda b, i: (b, i, 0)
    const2 = lambda b, i: (0, 0)
    return pl.pallas_call(
        functools.partial(_ssdbr_kernel, group_w=inner // SSD_GROUPS),
        out_shape=jax.ShapeDtypeStruct((bsz, s, d), F32),
        grid=(bsz, s // rows),
        in_specs=[pl.BlockSpec((1, rows, inner), tok), pl.BlockSpec((1, rows, inner), tok),
                  pl.BlockSpec((1, rows, inner), tok),
                  pl.BlockSpec((1, rows, d), lambda b, i: (b, i, inner // d)),
                  pl.BlockSpec((1, inner), const2), pl.BlockSpec((1, d), const2),
                  pl.BlockSpec(gsum.shape, const2), pl.BlockSpec(gexp.shape, const2),
                  pl.BlockSpec((inner, d), const2)],
        out_specs=pl.BlockSpec((1, rows, d), tok),
        compiler_params=_cparams("arbitrary", "arbitrary"),
        name="ssdbr",
    )(y_f, y_b, zgm, zgm, norm_w.reshape(1, inner), b_merge_ssd.reshape(1, d), gsum, gexp, w_br)


def _pack_bf16_pair(lo, hi):
    lo_bits = lax.bitcast_convert_type(lo.astype(BF16).astype(F32), jnp.uint32)
    hi_bits = lax.bitcast_convert_type(hi.astype(BF16).astype(F32), jnp.uint32)
    return (lo_bits >> 16) | hi_bits


def _unpack_bf16_pair(p):
    lo = lax.bitcast_convert_type(p << 16, F32).astype(BF16)
    hi = lax.bitcast_convert_type(p & jnp.uint32(0xFFFF0000), F32).astype(BF16)
    return jnp.concatenate([lo, hi], axis=1)


def _out_kernel(ul_ref, us_ref, x_ref, g1_ref, sh2_ref, sc2_ref, wout_ref, lg_ref, lb_ref, rw_ref, rb_ref,
                x1_ref, hp_ref, gates_ref, sel_ref, *, alpha):
    u = (ul_ref[0] + us_ref[0]).astype(BF16)
    mix = jnp.dot(u, wout_ref[...], preferred_element_type=F32)
    x1 = _ln_rows(alpha * x_ref[0] + g1_ref[0] * mix) * lg_ref[...] + lb_ref[...]
    x1_ref[0] = x1
    h2 = _ln_rows(x1) * (1.0 + sc2_ref[0]) + sh2_ref[0]
    half = h2.shape[1] // 2
    hp_ref[0] = _pack_bf16_pair(h2[:, :half], h2[:, half:])
    logits = jnp.dot(h2, rw_ref[...], precision=HIGHEST, preferred_element_type=F32) + rb_ref[...]
    lane = lax.broadcasted_iota(jnp.int32, logits.shape, 1)
    work = logits
    sel = jnp.zeros(logits.shape, jnp.bool_)
    top = None
    for k in range(MOE_TOP_K):
        m = jnp.max(work, axis=1, keepdims=True)
        if k == 0:
            top = m
        idx = jnp.min(jnp.where(work == m, lane, LANES), axis=1, keepdims=True)
        pick = lane == idx
        sel = sel | pick
        work = jnp.where(pick, 3.0 * NEG_BIG, work)
    e = jnp.where(sel, jnp.exp(logits - top), 0.0)
    gates_ref[0] = e / jnp.sum(e, axis=1, keepdims=True)
    sel_ref[0] = sel.astype(F32)


def _out(u_l, u_s, x, g1, sh2, sc2, w_out, ln_g, ln_b, rw, rb, rows, alpha):
    bsz, t, d = x.shape
    tok = lambda b, i: (b, i, 0)
    per_b = lambda b, i: (b, 0, 0)
    const2 = lambda b, i: (0, 0)
    return pl.pallas_call(
        functools.partial(_out_kernel, alpha=alpha),
        out_shape=[jax.ShapeDtypeStruct((bsz, t, d), F32), jax.ShapeDtypeStruct((bsz, t, d // 2), jnp.uint32),
                   jax.ShapeDtypeStruct((bsz, t, LANES), F32), jax.ShapeDtypeStruct((bsz, t, LANES), F32)],
        grid=(bsz, t // rows),
        in_specs=[pl.BlockSpec((1, rows, d), tok), pl.BlockSpec((1, rows, d), tok), pl.BlockSpec((1, rows, d), tok),
                  pl.BlockSpec((1, 1, d), per_b), pl.BlockSpec((1, 1, d), per_b), pl.BlockSpec((1, 1, d), per_b),
                  pl.BlockSpec((d, d), const2), pl.BlockSpec((1, d), const2), pl.BlockSpec((1, d), const2),
                  pl.BlockSpec((d, LANES), const2), pl.BlockSpec((1, LANES), const2)],
        out_specs=[pl.BlockSpec((1, rows, d), tok), pl.BlockSpec((1, rows, d // 2), tok),
                   pl.BlockSpec((1, rows, LANES), tok), pl.BlockSpec((1, rows, LANES), tok)],
        compiler_params=_cparams("arbitrary", "arbitrary"),
        name="out",
    )(u_l, u_s, x, g1, sh2, sc2, w_out, ln_g.reshape(1, d), ln_b.reshape(1, d), rw, rb)


def _rank_kernel(sel_ref, rank_ref, cnt_ref, carry_ref):
    @pl.when(pl.program_id(0) == 0)
    def _():
        carry_ref[...] = jnp.zeros(carry_ref.shape, F32)

    s = sel_ref[...]
    tt = s.shape[0]
    ri = lax.broadcasted_iota(jnp.int32, (tt, tt), 0)
    ci = lax.broadcasted_iota(jnp.int32, (tt, tt), 1)
    earlier = (ri > ci).astype(BF16)
    rank_ref[...] = jnp.dot(earlier, s.astype(BF16), preferred_element_type=F32) + carry_ref[0:1, :]
    carry_ref[...] = carry_ref[...] + jnp.sum(s, axis=0, keepdims=True)
    cnt_ref[...] = carry_ref[...]


def _rank(sel, tt):
    n = sel.shape[0]
    return pl.pallas_call(
        _rank_kernel,
        out_shape=[jax.ShapeDtypeStruct((n, LANES), F32), jax.ShapeDtypeStruct((8, LANES), F32)],
        grid=(n // tt,),
        in_specs=[pl.BlockSpec((tt, LANES), lambda i: (i, 0))],
        out_specs=[pl.BlockSpec((tt, LANES), lambda i: (i, 0)), pl.BlockSpec((8, LANES), lambda i: (0, 0))],
        scratch_shapes=[pltpu.VMEM((8, LANES), F32)],
        compiler_params=_cparams("arbitrary"),
        name="moe_rank",
    )(sel)


def _pos_kernel(sel_ref, rank_ref, gates_ref, off_ref, pos_ref, w_ref):
    avail = sel_ref[...] > 0.5
    posf = off_ref[...] + rank_ref[...]
    gates = gates_ref[...]
    lane = lax.broadcasted_iota(jnp.int32, posf.shape, 1)
    cols_p = jnp.zeros(posf.shape, F32)
    cols_w = jnp.zeros(posf.shape, F32)
    for k in range(MOE_TOP_K):
        m = jnp.min(jnp.where(avail, lane, LANES), axis=1, keepdims=True)
        pick = lane == m
        cols_p = jnp.where(lane == k, jnp.sum(jnp.where(pick, posf, 0.0), axis=1, keepdims=True), cols_p)
        cols_w = jnp.where(lane == k, jnp.sum(jnp.where(pick, gates, 0.0), axis=1, keepdims=True), cols_w)
        avail = avail & jnp.logical_not(pick)
    w_ref[...] = cols_w
    pos_ref[0] = cols_p.T[0:8, :].astype(jnp.int32)


def _pos(sel, rank, gates, off, tt):
    n = sel.shape[0]
    tok = lambda i: (i, 0)
    return pl.pallas_call(
        _pos_kernel,
        out_shape=[jax.ShapeDtypeStruct((n // tt, 8, tt), jnp.int32), jax.ShapeDtypeStruct((n, LANES), F32)],
        grid=(n // tt,),
        in_specs=[pl.BlockSpec((tt, LANES), tok), pl.BlockSpec((tt, LANES), tok), pl.BlockSpec((tt, LANES), tok),
                  pl.BlockSpec((1, LANES), lambda i: (0, 0))],
        out_specs=[pl.BlockSpec((1, 8, tt), lambda i: (i, 0, 0)), pl.BlockSpec((tt, LANES), tok)],
        compiler_params=_cparams("arbitrary"),
        name="moe_pos",
    )(sel, rank, gates, off)


def _dispatch_kernel(pos_ref, hp_ref, xs_init_hbm, xs_hbm, sem, *, tt):
    del xs_init_hbm

    def body(j, carry):
        src = hp_ref.at[pl.ds(j, 1)]
        for k in range(MOE_TOP_K):
            pltpu.make_async_copy(src, xs_hbm.at[pl.ds(pos_ref[0, k, j], 1)], sem).start()
        return carry

    lax.fori_loop(0, tt, body, 0, unroll=4)
    for k in range(MOE_TOP_K):
        pltpu.make_async_copy(hp_ref, xs_hbm.at[pl.ds(0, tt)], sem).wait()


def _dispatch(pos, hp, xs_init, tt):
    n, half = hp.shape
    any_spec = pl.BlockSpec(memory_space=pl.ANY)
    return pl.pallas_call(
        functools.partial(_dispatch_kernel, tt=tt),
        out_shape=jax.ShapeDtypeStruct(xs_init.shape, xs_init.dtype),
        grid=(n // tt,),
        in_specs=[pl.BlockSpec((1, 8, tt), lambda i: (i, 0, 0), memory_space=pltpu.SMEM),
                  pl.BlockSpec((tt, half), lambda i: (i, 0)), any_spec],
        out_specs=any_spec,
        scratch_shapes=[pltpu.SemaphoreType.DMA(())],
        input_output_aliases={2: 0},
        compiler_params=pltpu.CompilerParams(dimension_semantics=("arbitrary",), has_side_effects=True),
        name="moe_dispatch",
    )(pos, hp, xs_init)


def _expert_kernel(te_ref, nv_ref, xs_ref, w1_ref, b1_ref, w2_ref, b2_ref, ys_ref, w1b_ref, w2b_ref, *, ff):
    t = pl.program_id(0)
    e = te_ref[t]
    prev = te_ref[jnp.maximum(t - 1, 0)]

    @pl.when((t == 0) | (e != prev))
    def _():
        w1b_ref[...] = w1_ref[0].astype(BF16)
        w2b_ref[...] = w2_ref[0].astype(BF16)

    @pl.when(t < nv_ref[0])
    def _():
        xrow = _unpack_bf16_pair(xs_ref[...])
        gu = jnp.dot(xrow, w1b_ref[...], preferred_element_type=F32) + b1_ref[0]
        g = jnp.minimum(gu[:, :ff], SWIGLU_LIMIT)
        u = jnp.clip(gu[:, ff:], -SWIGLU_LIMIT, SWIGLU_LIMIT)
        act = ((u + 1.0) * g * _sigmoid(SWIGLU_ALPHA * g)).astype(BF16)
        ys_ref[...] = jnp.dot(act, w2b_ref[...], preferred_element_type=F32) + b2_ref[0]

    @pl.when(t >= nv_ref[0])
    def _():
        ys_ref[...] = jnp.zeros(ys_ref.shape, F32)


def _experts(tile_expert, n_valid, xs, w1, b1, w2, b2, tm):
    rows, half = xs.shape
    n_exp, d, ff2 = w1.shape
    ff = ff2 // 2
    per_e = lambda t, te, nv: (te[t], 0, 0)
    return pl.pallas_call(
        functools.partial(_expert_kernel, ff=ff),
        out_shape=jax.ShapeDtypeStruct((rows, d), F32),
        grid_spec=pltpu.PrefetchScalarGridSpec(
            num_scalar_prefetch=2,
            grid=(rows // tm,),
            in_specs=[pl.BlockSpec((tm, half), lambda t, te, nv: (t, 0)),
                      pl.BlockSpec((1, d, ff2), per_e), pl.BlockSpec((1, 1, ff2), per_e),
                      pl.BlockSpec((1, ff, d), per_e), pl.BlockSpec((1, 1, d), per_e)],
            out_specs=pl.BlockSpec((tm, d), lambda t, te, nv: (t, 0)),
            scratch_shapes=[pltpu.VMEM((d, ff2), BF16), pltpu.VMEM((ff, d), BF16)]),
        compiler_params=_cparams("arbitrary"),
        name="moe_experts",
    )(tile_expert, n_valid, xs, w1, b1.reshape(n_exp, 1, ff2), w2, b2.reshape(n_exp, 1, d))


def _combine_kernel(pos_ref, ys_hbm, w_ref, x1_ref, g2_ref, lg_ref, lb_ref, o_ref, buf_ref, sem, *, tt, alpha):
    def body(j, carry):
        for k in range(MOE_TOP_K):
            pltpu.make_async_copy(ys_hbm.at[pl.ds(pos_ref[0, k, j], 1)], buf_ref.at[k, pl.ds(j, 1)], sem).start()
        return carry

    lax.fori_loop(0, tt, body, 0, unroll=4)
    for k in range(MOE_TOP_K):
        pltpu.make_async_copy(ys_hbm.at[pl.ds(0, tt)], buf_ref.at[k], sem).wait()
    w = w_ref[...]
    acc = sum(w[:, k:k + 1] * buf_ref[k] for k in range(MOE_TOP_K))
    o_ref[0] = _ln_rows(alpha * x1_ref[0] + g2_ref[0] * acc) * lg_ref[...] + lb_ref[...]


def _combine(pos, ys, w, x1, g2, ln_g, ln_b, tt, alpha):
    bsz, t, d = x1.shape
    nt = t // tt
    return pl.pallas_call(
        functools.partial(_combine_kernel, tt=tt, alpha=alpha),
        out_shape=jax.ShapeDtypeStruct((bsz, t, d), F32),
        grid=(bsz, nt),
        in_specs=[pl.BlockSpec((1, 8, tt), lambda b, i: (b * nt + i, 0, 0), memory_space=pltpu.SMEM),
                  pl.BlockSpec(memory_space=pl.ANY),
                  pl.BlockSpec((tt, LANES), lambda b, i: (b * nt + i, 0)),
                  pl.BlockSpec((1, tt, d), lambda b, i: (b, i, 0)),
                  pl.BlockSpec((1, 1, d), lambda b, i: (b, 0, 0)),
                  pl.BlockSpec((1, d), lambda b, i: (0, 0)), pl.BlockSpec((1, d), lambda b, i: (0, 0))],
        out_specs=pl.BlockSpec((1, tt, d), lambda b, i: (b, i, 0)),
        scratch_shapes=[pltpu.VMEM((MOE_TOP_K, tt, d), F32), pltpu.SemaphoreType.DMA(())],
        compiler_params=_cparams("arbitrary", "arbitrary"),
        name="moe_combine",
    )(pos, ys, w, x1, g2, ln_g.reshape(1, d), ln_b.reshape(1, d))


def _moe(hp, gates, sel, x1, g2, w1, b1, w2, b2, ln_g, ln_b, alpha, tt, tm):
    bsz, t, d = x1.shape
    n = bsz * t
    n_exp = w1.shape[0]
    sel2, gates2 = sel.reshape(n, LANES), gates.reshape(n, LANES)
    rank, cnt = _rank(sel2, tt)
    counts = cnt[0, :n_exp].astype(jnp.int32)
    tiles_per = (counts + tm - 1) // tm
    tile_end = jnp.cumsum(tiles_per)
    off = jnp.pad(((tile_end - tiles_per) * tm).astype(F32).reshape(1, n_exp), ((0, 0), (0, LANES - n_exp)))
    n_tiles = (n * MOE_TOP_K) // tm + n_exp
    n_valid = tile_end[-1:]
    tile_id = jnp.minimum(jnp.arange(n_tiles), n_valid - 1)
    tile_expert = jnp.sum((tile_end[None, :] <= tile_id[:, None]).astype(jnp.int32), axis=1)
    tile_expert = jnp.minimum(tile_expert, n_exp - 1)
    pos, w = _pos(sel2, rank, gates2, off, tt)
    xs = _dispatch(pos, hp.reshape(n, d // 2), jnp.zeros((n_tiles * tm, d // 2), jnp.uint32), tt)
    ys = _experts(tile_expert, n_valid.astype(jnp.int32), xs, w1, b1, w2, b2, tm)
    return _combine(pos, ys, w, x1, g2, ln_g, ln_b, tt, alpha)


def _to_cols(u, grid_rows):
    b, t, c = u.shape
    return u.reshape(b, grid_rows, GRID_W, c).transpose(0, 2, 1, 3).reshape(b, t, c)


def _from_cols(u, grid_rows):
    b, t, c = u.shape
    return u.reshape(b, GRID_W, grid_rows, c).transpose(0, 2, 1, 3).reshape(b, t, c)


def kernel(x, c, ctx, c_ctx, w_ada, b_ada, w_in, b_merge, conv_lru_w, conv_lru_b, lru_wa, lru_ba, lru_wx, lru_bx, lru_lambda, conv_ssd_w, conv_ssd_b, ssd_dt_bias, ssd_a_log, ssd_d, ssd_norm_w, w_br_lru, w_br_ssd, w_out, ln1_g, ln1_b, router_w, router_b, moe_w1, moe_b1, moe_w2, moe_b2, ln2_g, ln2_b):
    depth = w_ada.shape[0]
    assert depth == 1, "single-layer stack: the context tokens only supply scan states"
    bsz, t, d = x.shape
    t_ctx = ctx.shape[1]
    alpha = (2.0 * depth) ** 0.25
    grid_rows = t // GRID_W
    d_rnn = w_br_lru.shape[1]
    inner = w_br_ssd.shape[1]
    heads = inner // SSD_HEAD_DIM
    gn = SSD_GROUPS * SSD_STATE
    col_gr = d_rnn
    col_z = col_gr + d_rnn
    col_xbc = col_z + inner
    col_dt = col_xbc + inner + 2 * gn
    col_gm = col_dt + 2 * heads
    assert 2 * heads <= LANES and t_ctx % SSD_CHUNK == 0 and t % SSD_CHUNK == 0

    pad = (-(bsz + 1)) % 8
    c_all = jnp.concatenate([c, c_ctx[None, :], jnp.zeros((pad, d), F32)], axis=0)
    mod = _ada(c_all, w_ada[0], b_ada[0])
    sh1, sc1, g1, sh2, sc2, g2 = (mod[:bsz, k * d:(k + 1) * d] for k in range(6))
    csh1, csc1 = mod[bsz:bsz + 1, 0:d], mod[bsz:bsz + 1, d:2 * d]

    w_in_b = w_in[0].astype(BF16)

    tt = 32
    xall_tm = jnp.concatenate([jnp.transpose(ctx, (1, 0, 2)), jnp.transpose(x, (1, 0, 2))], axis=0)
    sh_tm = jnp.stack([jnp.broadcast_to(csh1, (bsz, d)), sh1])
    sc_tm = jnp.stack([jnp.broadcast_to(csc1, (bsz, d)), sc1])
    seg = lambda i: jnp.where(i * tt >= t_ctx, 1, 0)
    xr_all = _inproj(xall_tm, sh_tm, sc_tm, w_in_b[:, :col_gr], (tt, bsz), 1024, sel=seg)
    w_grgm = jnp.concatenate([w_in_b[:, col_gr:col_z], w_in_b[:, col_gm:col_gm + d]], axis=1)
    grgm = _inproj(xall_tm, sh_tm, sc_tm, w_grgm, (tt, bsz), w_grgm.shape[1], sel=seg, first=t_ctx // tt,
                   count=t // tt, out_dtype=BF16)
    p_f = _lru_params(lru_wa[0, 0], lru_ba[0, 0], lru_wx[0, 0], lru_bx[0, 0], lru_lambda[0, 0], 256)
    p_b = _lru_params(lru_wa[0, 1], lru_ba[0, 1], lru_wx[0, 1], lru_bx[0, 1], lru_lambda[0, 1], 256)
    u_lru_tm = _lru(xr_all, grgm, p_f, p_b, conv_lru_w[0], conv_lru_b[0], b_merge[0, :d],
                    w_br_lru[0].astype(BF16), t_ctx, tt)

    x_cm = _to_cols(x, grid_rows)
    sh_b, sc_b = sh1[:, None, :], sc1[:, None, :]
    csh_b, csc_b = csh1[None], csc1[None]
    zero = lambda i: 0
    w_xbc = w_in_b[:, col_xbc:col_dt]
    w_dt = jnp.pad(w_in_b[:, col_dt:col_gm], ((0, 0), (0, LANES - 2 * heads)))
    w_zgm = jnp.concatenate([w_in_b[:, col_z:col_xbc], w_in_b[:, col_gm + d:]], axis=1)
    xbc = _inproj(x_cm, sh_b, sc_b, w_xbc, (1, t), 512, conv_w=conv_ssd_w[0], conv_b=conv_ssd_b[0], out_dtype=BF16)
    xbc_c = _inproj(ctx, csh_b, csc_b, w_xbc, (1, t_ctx), 512, sel=zero, conv_w=conv_ssd_w[0],
                    conv_b=conv_ssd_b[0], out_dtype=BF16)
    dt_raw = _inproj(x_cm, sh_b, sc_b, w_dt, (1, 512), LANES)
    dt_raw_c = _inproj(ctx, csh_b, csc_b, w_dt, (1, t_ctx), LANES, sel=zero)
    zgm = _inproj(x_cm, sh_b, sc_b, w_zgm, (1, 512), w_zgm.shape[1], out_dtype=BF16)

    lane_pad = LANES - 2 * heads
    dtb = jnp.pad(ssd_dt_bias[0].reshape(1, 2 * heads), ((0, 0), (0, lane_pad)))
    alog = jnp.pad(ssd_a_log[0].reshape(1, 2 * heads), ((0, 0), (0, lane_pad)))
    dsk = jnp.repeat(ssd_d[0], SSD_HEAD_DIM).reshape(1, inner)
    head_of_lane = jnp.arange(inner) // SSD_HEAD_DIM

    def expand(lane0):
        e = (jnp.arange(LANES)[:, None] == head_of_lane[None, :] + lane0).astype(BF16)
        return jnp.concatenate([e, e], axis=0)

    gw = inner // SSD_GROUPS
    s0 = jnp.zeros((bsz, SSD_GROUPS, SSD_STATE, gw), F32)
    common = (dtb, alog)
    (st_f,) = _ssd(xbc_c, dt_raw_c, *common, expand(0), dsk, s0, reverse=False, with_y=False, add_skip=False, lane0=0)
    (st_b,) = _ssd(xbc_c, dt_raw_c, *common, expand(heads), dsk, s0, reverse=True, with_y=False, add_skip=False,
                   lane0=heads)
    y_f, _ = _ssd(xbc, dt_raw, *common, expand(0), dsk, st_f, reverse=False, with_y=True, add_skip=True, lane0=0)
    y_b, _ = _ssd(xbc, dt_raw, *common, expand(heads), dsk, st_b, reverse=True, with_y=True, add_skip=False,
                  lane0=heads)

    group_of_lane = jnp.arange(inner) // gw
    gsum = (group_of_lane[:, None] == jnp.arange(LANES)[None, :]).astype(BF16)
    gexp = jnp.concatenate([gsum.T, gsum.T], axis=0)
    u_ssd_cm = _ssdbr(y_f, y_b, zgm, ssd_norm_w[0], b_merge[0, d:], gsum, gexp, w_br_ssd[0].astype(BF16), 256)

    u_lru = jnp.transpose(u_lru_tm, (1, 0, 2))
    u_ssd = _from_cols(u_ssd_cm, grid_rows)
    n_exp = router_w.shape[2]
    rw = jnp.pad(router_w[0], ((0, 0), (0, LANES - n_exp)))
    rb = jnp.pad(router_b[0].reshape(1, n_exp), ((0, 0), (0, LANES - n_exp)), constant_values=NEG_BIG)
    x1, hp, gates, sel = _out(u_lru, u_ssd, x, g1[:, None, :], sh2[:, None, :], sc2[:, None, :],
                              w_out[0].astype(BF16), ln1_g[0], ln1_b[0], rw, rb, 512, alpha)

    return _moe(hp, gates, sel, x1, g2[:, None, :], moe_w1[0], moe_b1[0], moe_w2[0], moe_b2[0],
                ln2_g[0], ln2_b[0], alpha, 512, 512)
```

```python
import functools

import jax
import jax.numpy as jnp
from jax import lax
from jax.experimental import pallas as pl
from jax.experimental.pallas import tpu as pltpu

F32 = jnp.float32
BF16 = jnp.bfloat16
HIGHEST = lax.Precision.HIGHEST

GRID_W = 64
LRU_BLOCK_W = 64
LRU_C = 8.0
CONV_W = 4
SSD_HEAD_DIM = 64
SSD_GROUPS = 8
SSD_STATE = 128
SSD_CHUNK = 128
MOE_TOP_K = 4
SWIGLU_LIMIT = 7.0
SWIGLU_ALPHA = 1.702
LN_EPS = 1e-5
RMS_EPS = 1e-5
LANES = 128
NEG_BIG = -1e30
CONV_SUBTILE = 256
VMEM_LIMIT = 56 * 1024 * 1024


def _cparams(*sem):
    return pltpu.CompilerParams(dimension_semantics=sem, vmem_limit_bytes=VMEM_LIMIT)


def _ln_rows(x):
    mu = jnp.mean(x, axis=-1, keepdims=True)
    xc = x - mu
    var = jnp.mean(xc * xc, axis=-1, keepdims=True)
    return xc * lax.rsqrt(var + LN_EPS)


def _sigmoid(x):
    return 0.5 * (jnp.tanh(0.5 * x) + 1.0)


def _silu(x):
    return x * _sigmoid(x)


def _softplus(x):
    return jnp.maximum(x, 0.0) + jnp.log(1.0 + jnp.exp(-jnp.abs(x)))


def _ada_kernel(c_ref, w_ref, b_ref, o_ref):
    c = c_ref[...]
    o_ref[...] = jnp.dot(_silu(c), w_ref[...], precision=HIGHEST, preferred_element_type=F32) + b_ref[...]


def _ada(c_all, w, b):
    m, d = c_all.shape
    n = w.shape[1]
    tn = 1024
    return pl.pallas_call(
        _ada_kernel,
        out_shape=jax.ShapeDtypeStruct((m, n), F32),
        grid=(n // tn,),
        in_specs=[pl.BlockSpec((m, d), lambda j: (0, 0)),
                  pl.BlockSpec((d, tn), lambda j: (0, j)),
                  pl.BlockSpec((1, tn), lambda j: (0, j))],
        out_specs=pl.BlockSpec((m, tn), lambda j: (0, j)),
        compiler_params=_cparams("arbitrary"),
        name="ada",
    )(c_all, w, b.reshape(1, n))


def _inproj_kernel(x_ref, sh_ref, sc_ref, w_ref, *rest, conv):
    if conv:
        cw_ref, cb_ref, o_ref, h_ref = rest
    else:
        o_ref, h_ref = rest
    rows = h_ref.shape[0]

    @pl.when(pl.program_id(1) == 0)
    def _():
        h = _ln_rows(x_ref[...]) * (1.0 + sc_ref[...]) + sh_ref[...]
        h_ref[...] = h.reshape(h_ref.shape).astype(BF16)

    if not conv:
        acc = jnp.dot(h_ref[...], w_ref[...], preferred_element_type=F32)
        o_ref[...] = acc.reshape(o_ref.shape).astype(o_ref.dtype)
        return

    def taps(a, cw, cb, n, head, tail):
        r = lax.broadcasted_iota(jnp.int32, a.shape, 0)
        t0, t1, t3 = pltpu.roll(a, 2, 0), pltpu.roll(a, 1, 0), pltpu.roll(a, n - 1, 0)
        if head:
            t0, t1 = jnp.where(r >= 2, t0, 0.0), jnp.where(r >= 1, t1, 0.0)
        if tail:
            t3 = jnp.where(r < n - 1, t3, 0.0)
        return _silu(a * cw[2:3, :] + cb + t0 * cw[0:1, :] + t1 * cw[1:2, :] + t3 * cw[3:4, :])

    edge = 32
    half = edge // 2
    for lo in range(0, w_ref.shape[1], CONV_SUBTILE):
        cols = slice(lo, lo + CONV_SUBTILE)
        acc = jnp.dot(h_ref[...], w_ref[:, cols], preferred_element_type=F32)
        cw, cb = cw_ref[:, cols], cb_ref[:, cols]
        o_ref[0, :, cols] = taps(acc, cw, cb, rows, False, False).astype(o_ref.dtype)
        o_ref[0, 0:half, cols] = taps(acc[0:edge], cw, cb, edge, True, False)[0:half].astype(o_ref.dtype)
        o_ref[0, rows - half:rows, cols] = taps(acc[rows - edge:rows], cw, cb, edge, False, True)[half:].astype(
            o_ref.dtype)


def _inproj(x3, sh3, sc3, w, tile, tn, sel=None, conv_w=None, conv_b=None, out_dtype=F32, first=0, count=None):
    gdim, rdim, d = x3.shape
    g, r = tile
    n = w.shape[1]
    rows = g * r
    nblk_g = gdim // g if count is None else count
    nblk_r = rdim // r
    if sel is None:
        sel = lambda i: i
    conv = conv_w is not None
    mg, mr = sh3.shape[1], sh3.shape[2]
    in_specs = [pl.BlockSpec((g, r, d), lambda i, j: ((i // nblk_r) + first, i % nblk_r, 0)),
                pl.BlockSpec((1, mg, mr), lambda i, j: (sel((i // nblk_r) + first), 0, 0)),
                pl.BlockSpec((1, mg, mr), lambda i, j: (sel((i // nblk_r) + first), 0, 0)),
                pl.BlockSpec((d, tn), lambda i, j: (0, j))]
    args = [x3, sh3, sc3, w]
    if conv:
        assert r == rdim and g == 1
        in_specs += [pl.BlockSpec((CONV_W, tn), lambda i, j: (0, j)),
                     pl.BlockSpec((1, tn), lambda i, j: (0, j))]
        args += [conv_w, conv_b.reshape(1, n)]
    out_g = nblk_g * g
    return pl.pallas_call(
        functools.partial(_inproj_kernel, conv=conv),
        out_shape=jax.ShapeDtypeStruct((out_g, rdim, n), out_dtype),
        grid=(nblk_g * nblk_r, n // tn),
        in_specs=in_specs,
        out_specs=pl.BlockSpec((g, r, tn), lambda i, j: (i // nblk_r, i % nblk_r, j)),
        scratch_shapes=[pltpu.VMEM((rows, d), BF16)],
        compiler_params=_cparams("arbitrary", "arbitrary"),
        name="inproj_conv" if conv else "inproj",
    )(*args)


def _lru_gates(win_ref, w_ref, ba_ref, bx_ref, lam_ref, cw_ref, cb_ref, a_ref, b_ref, tt):
    bsz, c = win_ref.shape[1], win_ref.shape[2]
    cw = cw_ref[...]
    u = cb_ref[...].reshape(1, 1, c) + sum(win_ref[pl.ds(j, tt)] * cw[j:j + 1, :].reshape(1, 1, c)
                                           for j in range(CONV_W))
    u2 = u.reshape(tt * bsz, c)
    ub = u2.astype(BF16)
    sp = _softplus(-lam_ref[...])
    pack = w_ref.shape[1]
    for j in range(c // pack):
        lo = j * pack
        pre = jnp.dot(ub[:, lo:lo + pack], w_ref[j], preferred_element_type=F32)
        r = _sigmoid(pre[:, :pack] + ba_ref[:, lo:lo + pack])
        i = _sigmoid(pre[:, pack:] + bx_ref[:, lo:lo + pack])
        log_a = (-LRU_C) * r * sp[:, lo:lo + pack]
        a = jnp.exp(log_a)
        bt = jnp.sqrt(1.0 - jnp.exp(2.0 * log_a)) * (i * u2[:, lo:lo + pack])
        a_ref[:, :, lo:lo + pack] = a.reshape(tt, bsz, pack)
        b_ref[:, :, lo:lo + pack] = bt.reshape(tt, bsz, pack)


def _lru_fill_window(win_ref, x_ref, prev_ref, next_ref, at_start, at_end, tt):
    zero2 = jnp.zeros(prev_ref.shape, F32)
    win_ref[pl.ds(0, 2)] = jnp.where(at_start, zero2, prev_ref[...])
    win_ref[pl.ds(2, tt)] = x_ref[...]
    win_ref[pl.ds(tt + 2, 1)] = jnp.where(at_end, jnp.zeros(next_ref.shape, F32), next_ref[...])


def _lru_bwd_kernel(x_ref, prev_ref, next_ref, w_ref, ba_ref, bx_ref, lam_ref, cw_ref, cb_ref,
                    hb_ref, win_ref, a_ref, b_ref, h_ref, *, tt, ncb, nlb):
    i = pl.program_id(0)
    blk = jnp.where(i < ncb, ncb - 1 - i, ncb + nlb - 1 - (i - ncb))
    at_start = (blk == 0) | (blk == ncb)
    at_end = (blk == ncb - 1) | (blk == ncb + nlb - 1)

    @pl.when(i == 0)
    def _():
        h_ref[...] = jnp.zeros(h_ref.shape, F32)

    _lru_fill_window(win_ref, x_ref, prev_ref, next_ref, at_start, at_end, tt)
    _lru_gates(win_ref, w_ref, ba_ref, bx_ref, lam_ref, cw_ref, cb_ref, a_ref, b_ref, tt)

    def step(k, h):
        t = tt - 1 - k
        h = a_ref[t] * h + b_ref[t]
        a_ref[t] = h
        return h

    h_ref[...] = lax.fori_loop(0, tt, step, h_ref[...])

    @pl.when(i >= ncb)
    def _():
        hb_ref[...] = a_ref[...]


def _lru_fwd_kernel(x_ref, prev_ref, next_ref, w_ref, ba_ref, bx_ref, lam_ref, cw_ref, cb_ref,
                    hb_ref, gr_ref, gm_ref, bm_ref, wbr_ref,
                    o_ref, win_ref, a_ref, b_ref, h_ref, *, tt, ncb, nlb):
    i = pl.program_id(0)
    at_start = (i == 0) | (i == ncb)
    at_end = (i == ncb - 1) | (i == ncb + nlb - 1)

    @pl.when(i == 0)
    def _():
        h_ref[...] = jnp.zeros(h_ref.shape, F32)

    _lru_fill_window(win_ref, x_ref, prev_ref, next_ref, at_start, at_end, tt)
    _lru_gates(win_ref, w_ref, ba_ref, bx_ref, lam_ref, cw_ref, cb_ref, a_ref, b_ref, tt)

    def step(t, h):
        h = a_ref[t] * h + b_ref[t]
        a_ref[t] = h
        return h

    h_ref[...] = lax.fori_loop(0, tt, step, h_ref[...])

    @pl.when(i >= ncb)
    def _():
        bsz, c = h_ref.shape
        rows = tt * bsz
        a_lat = (a_ref[...] + hb_ref[...]) * jax.nn.gelu(gr_ref[...].astype(F32), approximate=True)
        proj = jnp.dot(a_lat.reshape(rows, c).astype(BF16), wbr_ref[...], preferred_element_type=F32)
        gate = _sigmoid(gm_ref[...].astype(F32).reshape(rows, -1) + bm_ref[...])
        o_ref[...] = (gate * proj).reshape(o_ref.shape)


def _lru_specs(tt, bsz, c, blk_of, ttot):
    half = tt // 2
    return [pl.BlockSpec((tt, bsz, c), lambda i: (blk_of(i), 0, 0)),
            pl.BlockSpec((2, bsz, c), lambda i: (jnp.maximum(blk_of(i) * half - 1, 0), 0, 0)),
            pl.BlockSpec((1, bsz, c), lambda i: (jnp.minimum(blk_of(i) * tt + tt, ttot - 1), 0, 0))]


def _lru(xr_all, grgm, p_f, p_b, conv_w, conv_b, b_merge_lru, w_br, t_ctx, tt):
    ttot, bsz, c = xr_all.shape
    ncb, nlb = t_ctx // tt, (ttot - t_ctx) // tt
    nblk = ncb + nlb
    d = w_br.shape[1]
    const2 = lambda i: (0, 0)
    const3 = lambda i: (0, 0, 0)
    par_specs = [pl.BlockSpec(p_f[0].shape, const3), pl.BlockSpec((1, c), const2), pl.BlockSpec((1, c), const2),
                 pl.BlockSpec((1, c), const2), pl.BlockSpec((CONV_W, c), const2), pl.BlockSpec((1, c), const2)]
    scratch = [pltpu.VMEM((tt + 3, bsz, c), F32), pltpu.VMEM((tt, bsz, c), F32),
               pltpu.VMEM((tt, bsz, c), F32), pltpu.VMEM((bsz, c), F32)]

    bwd_blk = lambda i: jnp.where(i < ncb, ncb - 1 - i, ncb + nlb - 1 - (i - ncb))
    hb = pl.pallas_call(
        functools.partial(_lru_bwd_kernel, tt=tt, ncb=ncb, nlb=nlb),
        out_shape=jax.ShapeDtypeStruct((nlb * tt, bsz, c), F32),
        grid=(nblk,),
        in_specs=_lru_specs(tt, bsz, c, bwd_blk, ttot) + par_specs,
        out_specs=pl.BlockSpec((tt, bsz, c), lambda i: (jnp.where(i < ncb, nlb - 1, nblk - 1 - i), 0, 0)),
        scratch_shapes=scratch,
        compiler_params=_cparams("arbitrary"),
        name="lru_bwd",
    )(xr_all, xr_all, xr_all, *p_b, conv_w, conv_b.reshape(1, c))

    assert c == d
    lat = lambda i: (jnp.maximum(i - ncb, 0), 0, 0)
    lat1 = lambda i: (jnp.maximum(i - ncb, 0), 0, 1)
    return pl.pallas_call(
        functools.partial(_lru_fwd_kernel, tt=tt, ncb=ncb, nlb=nlb),
        out_shape=jax.ShapeDtypeStruct((nlb * tt, bsz, d), F32),
        grid=(nblk,),
        in_specs=_lru_specs(tt, bsz, c, lambda i: i, ttot) + par_specs + [
            pl.BlockSpec((tt, bsz, c), lat), pl.BlockSpec((tt, bsz, c), lat), pl.BlockSpec((tt, bsz, d), lat1),
            pl.BlockSpec((1, d), const2), pl.BlockSpec((c, d), const2)],
        out_specs=pl.BlockSpec((tt, bsz, d), lat),
        scratch_shapes=scratch,
        compiler_params=_cparams("arbitrary"),
        name="lru_fwd",
    )(xr_all, xr_all, xr_all, *p_f, conv_w, conv_b.reshape(1, c), hb, grgm, grgm, b_merge_lru.reshape(1, d), w_br)


def _lru_params(wa, ba, wx, bx, lam, pack):
    nb, bw, _ = wa.shape
    per = pack // bw
    c = nb * bw

    def bd(w):
        w4 = w.reshape(nb // per, per, bw, bw)
        eye = jnp.eye(per, dtype=w.dtype)
        return jnp.einsum('gpde,pq->gpdqe', w4, eye).reshape(nb // per, pack, pack)

    w = jnp.concatenate([bd(wa), bd(wx)], axis=-1).astype(BF16)
    return w, ba.reshape(1, c), bx.reshape(1, c), lam.reshape(1, c)


def _hi_lo(v):
    hi = v.astype(BF16)
    lo = (v - hi.astype(F32)).astype(BF16)
    return jnp.concatenate([hi, lo], axis=-1)


def _ssd_kernel(xbc_ref, dt_ref, dtb_ref, alog_ref, e_ref, dsk_ref, h0_ref, *outs,
                reverse, with_y, add_skip, lane0, inner):
    if with_y:
        y_ref, hfin_ref, st_ref = outs
    else:
        hfin_ref, st_ref = outs
    i = pl.program_id(1)
    q = SSD_CHUNK
    n = SSD_STATE
    gw = inner // SSD_GROUPS
    hpg = gw // SSD_HEAD_DIM

    @pl.when(i == 0)
    def _():
        st_ref[...] = h0_ref[0]

    dt = _softplus(dt_ref[0] + dtb_ref[...])
    da = dt * (-jnp.exp(alog_ref[...]))
    ri = lax.broadcasted_iota(jnp.int32, (q, q), 0)
    ci = lax.broadcasted_iota(jnp.int32, (q, q), 1)
    tri = (ri <= ci) if reverse else (ri >= ci)
    acum = jnp.dot(tri.astype(F32), da, precision=HIGHEST, preferred_element_type=F32)
    a_tot = jnp.sum(da, axis=0, keepdims=True)
    w_state = jnp.exp(a_tot - acum) * dt
    e2 = e_ref[...]
    ws_x = jnp.dot(w_state.astype(BF16), e2[0:LANES, :], preferred_element_type=F32)
    dec_x = jnp.dot(_hi_lo(jnp.broadcast_to(jnp.exp(a_tot), (8, LANES))), e2,
                    preferred_element_type=F32)[0:1, :]
    if with_y:
        eac_x = jnp.dot(_hi_lo(jnp.exp(acum)), e2, preferred_element_type=F32)
        acum_t = acum.T
        dt_t = dt.T
        rb = lax.broadcasted_iota(jnp.int32, (hpg * q, gw), 0) // q
        lb = lax.broadcasted_iota(jnp.int32, (hpg * q, gw), 1) // SSD_HEAD_DIM
        bd_mask = rb == lb

    for g in range(SSD_GROUPS):
        lo = g * gw
        xg = xbc_ref[0, :, lo:lo + gw]
        bg = xbc_ref[0, :, inner + g * n:inner + (g + 1) * n]
        cg = xbc_ref[0, :, inner + (SSD_GROUPS + g) * n:inner + (SSD_GROUPS + g + 1) * n]
        xgf = xg.astype(F32)
        xw = (xgf * ws_x[:, lo:lo + gw]).astype(BF16)
        st = st_ref[g]
        upd = lax.dot_general(bg, xw, (((0,), (0,)), ((), ())), preferred_element_type=F32)
        st_ref[g] = dec_x[:, lo:lo + gw] * st + upd
        if with_y:
            y_off = jnp.dot(cg, st.astype(BF16), preferred_element_type=F32) * eac_x[:, lo:lo + gw]
            cb = lax.dot_general(cg, bg, (((1,), (1,)), ((), ())), preferred_element_type=F32)
            ls = []
            for r in range(hpg):
                lane = lane0 + g * hpg + r
                seg = acum[:, lane:lane + 1] - acum_t[lane:lane + 1, :]
                l_h = cb * jnp.exp(jnp.where(tri, seg, NEG_BIG)) * dt_t[lane:lane + 1, :]
                ls.append(l_h.astype(BF16))
            lcat = jnp.concatenate(ls, axis=1)
            xbd = jnp.where(bd_mask, jnp.concatenate([xg] * hpg, axis=0), jnp.zeros((), BF16))
            y = y_off + jnp.dot(lcat, xbd, preferred_element_type=F32)
            if add_skip:
                y = y + dsk_ref[:, lo:lo + gw] * xgf
            y_ref[0, :, lo:lo + gw] = y

    @pl.when(i == pl.num_programs(1) - 1)
    def _():
        hfin_ref[0] = st_ref[...]


def _ssd(xbc, dt_raw, dtb, alog, e2, dsk, h0, *, reverse, with_y, add_skip, lane0):
    bsz, s, width = xbc.shape
    inner = e2.shape[1]
    nc = s // SSD_CHUNK
    gw = inner // SSD_GROUPS
    cidx = (lambda b, i: (b, nc - 1 - i, 0)) if reverse else (lambda b, i: (b, i, 0))
    const2 = lambda b, i: (0, 0)
    st_spec = pl.BlockSpec((1, SSD_GROUPS, SSD_STATE, gw), lambda b, i: (b, 0, 0, 0))
    st_shape = jax.ShapeDtypeStruct((bsz, SSD_GROUPS, SSD_STATE, gw), F32)
    out_shape, out_specs = [st_shape], [st_spec]
    if with_y:
        out_shape = [jax.ShapeDtypeStruct((bsz, s, inner), F32)] + out_shape
        out_specs = [pl.BlockSpec((1, SSD_CHUNK, inner), cidx)] + out_specs
    return pl.pallas_call(
        functools.partial(_ssd_kernel, reverse=reverse, with_y=with_y, add_skip=add_skip, lane0=lane0, inner=inner),
        out_shape=out_shape,
        grid=(bsz, nc),
        in_specs=[pl.BlockSpec((1, SSD_CHUNK, width), cidx),
                  pl.BlockSpec((1, SSD_CHUNK, LANES), cidx),
                  pl.BlockSpec((1, LANES), const2), pl.BlockSpec((1, LANES), const2),
                  pl.BlockSpec(e2.shape, const2), pl.BlockSpec((1, inner), const2), st_spec],
        out_specs=out_specs,
        scratch_shapes=[pltpu.VMEM((SSD_GROUPS, SSD_STATE, gw), F32)],
        compiler_params=_cparams("arbitrary", "arbitrary"),
        name="ssd_y" if with_y else "ssd_state",
    )(xbc, dt_raw, dtb, alog, e2, dsk, h0)


def _ssdbr_kernel(yf_ref, yb_ref, z_ref, gm_ref, nw_ref, bm_ref, gsum_ref, gexp_ref, w_ref, o_ref, *, group_w):
    y = (yf_ref[0] + yb_ref[0]) * _silu(z_ref[0].astype(F32))
    ms = jnp.dot((y * y).astype(BF16), gsum_ref[...], preferred_element_type=F32) * (1.0 / group_w)
    rs = lax.rsqrt(ms + RMS_EPS)
    rs_x = jnp.dot(_hi_lo(rs), gexp_ref[...], preferred_element_type=F32)
    yn = (y * rs_x * nw_ref[...]).astype(BF16)
    proj = jnp.dot(yn, w_ref[...], preferred_element_type=F32)
    o_ref[0] = _sigmoid(gm_ref[0].astype(F32) + bm_ref[...]) * proj


def _ssdbr(y_f, y_b, zgm, norm_w, b_merge_ssd, gsum, gexp, w_br, rows):
    bsz, s, inner = y_f.shape
    d = w_br.shape[1]
    assert inner % d == 0
    tok = lambda b, i: (b, i, 0)
    const2 = lambda b, i: (0, 0)
    return pl.pallas_call(
        functools.partial(_ssdbr_kernel, group_w=inner // SSD_GROUPS),
        out_shape=jax.ShapeDtypeStruct((bsz, s, d), F32),
        grid=(bsz, s // rows),
        in_specs=[pl.BlockSpec((1, rows, inner), tok), pl.BlockSpec((1, rows, inner), tok),
                  pl.BlockSpec((1, rows, inner), tok),
                  pl.BlockSpec((1, rows, d), lambda b, i: (b, i, inner // d)),
                  pl.BlockSpec((1, inner), const2), pl.BlockSpec((1, d), const2),
                  pl.BlockSpec(gsum.shape, const2), pl.BlockSpec(gexp.shape, const2),
                  pl.BlockSpec((inner, d), const2)],
        out_specs=pl.BlockSpec((1, rows, d), tok),
        compiler_params=_cparams("arbitrary", "arbitrary"),
        name="ssdbr",
    )(y_f, y_b, zgm, zgm, norm_w.reshape(1, inner), b_merge_ssd.reshape(1, d), gsum, gexp, w_br)


def _pack_bf16_pair(lo, hi):
    lo_bits = lax.bitcast_convert_type(lo.astype(BF16).astype(F32), jnp.uint32)
    hi_bits = lax.bitcast_convert_type(hi.astype(BF16).astype(F32), jnp.uint32)
    return (lo_bits >> 16) | hi_bits


def _unpack_bf16_pair(p):
    lo = lax.bitcast_convert_type(p << 16, F32).astype(BF16)
    hi = lax.bitcast_convert_type(p & jnp.uint32(0xFFFF0000), F32).astype(BF16)
    return jnp.concatenate([lo, hi], axis=1)


def _out_kernel(ul_ref, us_ref, x_ref, g1_ref, sh2_ref, sc2_ref, wout_ref, lg_ref, lb_ref, rw_ref, rb_ref,
                x1_ref, hp_ref, gates_ref, sel_ref, *, alpha):
    u = (ul_ref[0] + us_ref[0]).astype(BF16)
    mix = jnp.dot(u, wout_ref[...], preferred_element_type=F32)
    x1 = _ln_rows(alpha * x_ref[0] + g1_ref[0] * mix) * lg_ref[...] + lb_ref[...]
    x1_ref[0] = x1
    h2 = _ln_rows(x1) * (1.0 + sc2_ref[0]) + sh2_ref[0]
    half = h2.shape[1] // 2
    hp_ref[0] = _pack_bf16_pair(h2[:, :half], h2[:, half:])
    h_hi = h2.astype(BF16)
    h_lo = (h2 - h_hi.astype(F32)).astype(BF16)
    logits = jnp.dot(jnp.concatenate([h_hi, h_lo, h_hi], axis=1), rw_ref[...],
                     preferred_element_type=F32) + rb_ref[...]
    lane = lax.broadcasted_iota(jnp.int32, logits.shape, 1)
    work = logits
    sel = jnp.zeros(logits.shape, jnp.bool_)
    top = None
    for k in range(MOE_TOP_K):
        m = jnp.max(work, axis=1, keepdims=True)
        if k == 0:
            top = m
        idx = jnp.min(jnp.where(work == m, lane, LANES), axis=1, keepdims=True)
        pick = lane == idx
        sel = sel | pick
        work = jnp.where(pick, 3.0 * NEG_BIG, work)
    e = jnp.where(sel, jnp.exp(logits - top), 0.0)
    gates_ref[0] = e / jnp.sum(e, axis=1, keepdims=True)
    sel_ref[0] = sel.astype(F32)


def _out(u_l, u_s, x, g1, sh2, sc2, w_out, ln_g, ln_b, rw, rb, rows, alpha):
    bsz, t, d = x.shape
    tok = lambda b, i: (b, i, 0)
    per_b = lambda b, i: (b, 0, 0)
    const2 = lambda b, i: (0, 0)
    return pl.pallas_call(
        functools.partial(_out_kernel, alpha=alpha),
        out_shape=[jax.ShapeDtypeStruct((bsz, t, d), F32), jax.ShapeDtypeStruct((bsz, t, d // 2), jnp.uint32),
                   jax.ShapeDtypeStruct((bsz, t, LANES), F32), jax.ShapeDtypeStruct((bsz, t, LANES), F32)],
        grid=(bsz, t // rows),
        in_specs=[pl.BlockSpec((1, rows, d), tok), pl.BlockSpec((1, rows, d), tok), pl.BlockSpec((1, rows, d), tok),
                  pl.BlockSpec((1, 1, d), per_b), pl.BlockSpec((1, 1, d), per_b), pl.BlockSpec((1, 1, d), per_b),
                  pl.BlockSpec((d, d), const2), pl.BlockSpec((1, d), const2), pl.BlockSpec((1, d), const2),
                  pl.BlockSpec((3 * d, LANES), const2), pl.BlockSpec((1, LANES), const2)],
        out_specs=[pl.BlockSpec((1, rows, d), tok), pl.BlockSpec((1, rows, d // 2), tok),
                   pl.BlockSpec((1, rows, LANES), tok), pl.BlockSpec((1, rows, LANES), tok)],
        compiler_params=_cparams("arbitrary", "arbitrary"),
        name="out",
    )(u_l, u_s, x, g1, sh2, sc2, w_out, ln_g.reshape(1, d), ln_b.reshape(1, d), rw, rb)


def _rank_kernel(sel_ref, rank_ref, cnt_ref, carry_ref):
    @pl.when(pl.program_id(0) == 0)
    def _():
        carry_ref[...] = jnp.zeros(carry_ref.shape, F32)

    s = sel_ref[...]
    tt = s.shape[0]
    ri = lax.broadcasted_iota(jnp.int32, (tt, tt), 0)
    ci = lax.broadcasted_iota(jnp.int32, (tt, tt), 1)
    earlier = (ri > ci).astype(BF16)
    rank_ref[...] = jnp.dot(earlier, s.astype(BF16), preferred_element_type=F32) + carry_ref[0:1, :]
    carry_ref[...] = carry_ref[...] + jnp.sum(s, axis=0, keepdims=True)
    cnt_ref[...] = carry_ref[...]


def _rank(sel, tt):
    n = sel.shape[0]
    return pl.pallas_call(
        _rank_kernel,
        out_shape=[jax.ShapeDtypeStruct((n, LANES), F32), jax.ShapeDtypeStruct((8, LANES), F32)],
        grid=(n // tt,),
        in_specs=[pl.BlockSpec((tt, LANES), lambda i: (i, 0))],
        out_specs=[pl.BlockSpec((tt, LANES), lambda i: (i, 0)), pl.BlockSpec((8, LANES), lambda i: (0, 0))],
        scratch_shapes=[pltpu.VMEM((8, LANES), F32)],
        compiler_params=_cparams("arbitrary"),
        name="moe_rank",
    )(sel)


def _pos_kernel(sel_ref, rank_ref, gates_ref, off_ref, pos_ref, w_ref):
    avail = sel_ref[...] > 0.5
    posf = off_ref[...] + rank_ref[...]
    gates = gates_ref[...]
    lane = lax.broadcasted_iota(jnp.int32, posf.shape, 1)
    cols_p = jnp.zeros(posf.shape, F32)
    cols_w = jnp.zeros(posf.shape, F32)
    for k in range(MOE_TOP_K):
        m = jnp.min(jnp.where(avail, lane, LANES), axis=1, keepdims=True)
        pick = lane == m
        cols_p = jnp.where(lane == k, jnp.sum(jnp.where(pick, posf, 0.0), axis=1, keepdims=True), cols_p)
        cols_w = jnp.where(lane == k, jnp.sum(jnp.where(pick, gates, 0.0), axis=1, keepdims=True), cols_w)
        avail = avail & jnp.logical_not(pick)
    w_ref[...] = cols_w
    pos_ref[0] = cols_p.T[0:8, :].astype(jnp.int32)


def _pos(sel, rank, gates, off, tt):
    n = sel.shape[0]
    tok = lambda i: (i, 0)
    return pl.pallas_call(
        _pos_kernel,
        out_shape=[jax.ShapeDtypeStruct((n // tt, 8, tt), jnp.int32), jax.ShapeDtypeStruct((n, LANES), F32)],
        grid=(n // tt,),
        in_specs=[pl.BlockSpec((tt, LANES), tok), pl.BlockSpec((tt, LANES), tok), pl.BlockSpec((tt, LANES), tok),
                  pl.BlockSpec((1, LANES), lambda i: (0, 0))],
        out_specs=[pl.BlockSpec((1, 8, tt), lambda i: (i, 0, 0)), pl.BlockSpec((tt, LANES), tok)],
        compiler_params=_cparams("arbitrary"),
        name="moe_pos",
    )(sel, rank, gates, off)


def _dispatch_kernel(pos_ref, hp_ref, xs_init_hbm, xs_hbm, sem, *, tt):
    del xs_init_hbm

    def body(j, carry):
        src = hp_ref.at[pl.ds(j, 1)]
        for k in range(MOE_TOP_K):
            pltpu.make_async_copy(src, xs_hbm.at[pl.ds(pos_ref[0, k, j], 1)], sem).start(priority=k % 2)
        return carry

    lax.fori_loop(0, tt, body, 0, unroll=4)
    for k in range(MOE_TOP_K):
        pltpu.make_async_copy(hp_ref, xs_hbm.at[pl.ds(0, tt)], sem).wait()


def _dispatch(pos, hp, xs_init, tt):
    n, half = hp.shape
    any_spec = pl.BlockSpec(memory_space=pl.ANY)
    return pl.pallas_call(
        functools.partial(_dispatch_kernel, tt=tt),
        out_shape=jax.ShapeDtypeStruct(xs_init.shape, xs_init.dtype),
        grid=(n // tt,),
        in_specs=[pl.BlockSpec((1, 8, tt), lambda i: (i, 0, 0), memory_space=pltpu.SMEM),
                  pl.BlockSpec((tt, half), lambda i: (i, 0)), any_spec],
        out_specs=any_spec,
        scratch_shapes=[pltpu.SemaphoreType.DMA(())],
        input_output_aliases={2: 0},
        compiler_params=pltpu.CompilerParams(dimension_semantics=("arbitrary",), has_side_effects=True),
        name="moe_dispatch",
    )(pos, hp, xs_init)


def _expert_kernel(te_ref, nv_ref, xs_ref, w1_ref, b1_ref, w2_ref, b2_ref, ys_ref, w1b_ref, w2b_ref, *, ff):
    t = pl.program_id(0)
    e = te_ref[t]
    prev = te_ref[jnp.maximum(t - 1, 0)]

    @pl.when((t == 0) | (e != prev))
    def _():
        w1b_ref[...] = w1_ref[0].astype(BF16)
        w2b_ref[...] = w2_ref[0].astype(BF16)

    @pl.when(t < nv_ref[0])
    def _():
        xrow = _unpack_bf16_pair(xs_ref[...])
        gu = jnp.dot(xrow, w1b_ref[...], preferred_element_type=F32) + b1_ref[0]
        g = jnp.minimum(gu[:, :ff], SWIGLU_LIMIT)
        u = jnp.clip(gu[:, ff:], -SWIGLU_LIMIT, SWIGLU_LIMIT)
        act = ((u + 1.0) * g * _sigmoid(SWIGLU_ALPHA * g)).astype(BF16)
        ys_ref[...] = jnp.dot(act, w2b_ref[...], preferred_element_type=F32) + b2_ref[0]

    @pl.when(t >= nv_ref[0])
    def _():
        ys_ref[...] = jnp.zeros(ys_ref.shape, F32)


def _experts(tile_expert, n_valid, xs, w1, b1, w2, b2, tm):
    rows, half = xs.shape
    n_exp, d, ff2 = w1.shape
    ff = ff2 // 2
    per_e = lambda t, te, nv: (te[t], 0, 0)
    return pl.pallas_call(
        functools.partial(_expert_kernel, ff=ff),
        out_shape=jax.ShapeDtypeStruct((rows, d), F32),
        grid_spec=pltpu.PrefetchScalarGridSpec(
            num_scalar_prefetch=2,
            grid=(rows // tm,),
            in_specs=[pl.BlockSpec((tm, half), lambda t, te, nv: (t, 0)),
                      pl.BlockSpec((1, d, ff2), per_e), pl.BlockSpec((1, 1, ff2), per_e),
                      pl.BlockSpec((1, ff, d), per_e), pl.BlockSpec((1, 1, d), per_e)],
            out_specs=pl.BlockSpec((tm, d), lambda t, te, nv: (t, 0)),
            scratch_shapes=[pltpu.VMEM((d, ff2), BF16), pltpu.VMEM((ff, d), BF16)]),
        compiler_params=_cparams("arbitrary"),
        name="moe_experts",
    )(tile_expert, n_valid, xs, w1, b1.reshape(n_exp, 1, ff2), w2, b2.reshape(n_exp, 1, d))


def _combine_kernel(pos_ref, ys_hbm, w_ref, x1_ref, g2_ref, lg_ref, lb_ref, o_ref, buf_ref, sem, *, tt, alpha):
    def body(j, carry):
        for k in range(MOE_TOP_K):
            pltpu.make_async_copy(ys_hbm.at[pl.ds(pos_ref[0, k, j], 1)], buf_ref.at[k, pl.ds(j, 1)], sem).start(
                priority=k % 2)
        return carry

    lax.fori_loop(0, tt, body, 0, unroll=4)
    for k in range(MOE_TOP_K):
        pltpu.make_async_copy(ys_hbm.at[pl.ds(0, tt)], buf_ref.at[k], sem).wait()
    w = w_ref[...]
    acc = sum(w[:, k:k + 1] * buf_ref[k] for k in range(MOE_TOP_K))
    o_ref[0] = _ln_rows(alpha * x1_ref[0] + g2_ref[0] * acc) * lg_ref[...] + lb_ref[...]


def _combine(pos, ys, w, x1, g2, ln_g, ln_b, tt, alpha):
    bsz, t, d = x1.shape
    nt = t // tt
    return pl.pallas_call(
        functools.partial(_combine_kernel, tt=tt, alpha=alpha),
        out_shape=jax.ShapeDtypeStruct((bsz, t, d), F32),
        grid=(bsz, nt),
        in_specs=[pl.BlockSpec((1, 8, tt), lambda b, i: (b * nt + i, 0, 0), memory_space=pltpu.SMEM),
                  pl.BlockSpec(memory_space=pl.ANY),
                  pl.BlockSpec((tt, LANES), lambda b, i: (b * nt + i, 0)),
                  pl.BlockSpec((1, tt, d), lambda b, i: (b, i, 0)),
                  pl.BlockSpec((1, 1, d), lambda b, i: (b, 0, 0)),
                  pl.BlockSpec((1, d), lambda b, i: (0, 0)), pl.BlockSpec((1, d), lambda b, i: (0, 0))],
        out_specs=pl.BlockSpec((1, tt, d), lambda b, i: (b, i, 0)),
        scratch_shapes=[pltpu.VMEM((MOE_TOP_K, tt, d), F32), pltpu.SemaphoreType.DMA(())],
        compiler_params=_cparams("arbitrary", "arbitrary"),
        name="moe_combine",
    )(pos, ys, w, x1, g2, ln_g.reshape(1, d), ln_b.reshape(1, d))


def _moe(hp, gates, sel, x1, g2, w1, b1, w2, b2, ln_g, ln_b, alpha, tt, tm):
    bsz, t, d = x1.shape
    n = bsz * t
    n_exp = w1.shape[0]
    sel2, gates2 = sel.reshape(n, LANES), gates.reshape(n, LANES)
    rank, cnt = _rank(sel2, tt)
    counts = cnt[0, :n_exp].astype(jnp.int32)
    tiles_per = (counts + tm - 1) // tm
    tile_end = jnp.cumsum(tiles_per)
    off = jnp.pad(((tile_end - tiles_per) * tm).astype(F32).reshape(1, n_exp), ((0, 0), (0, LANES - n_exp)))
    n_tiles = (n * MOE_TOP_K) // tm + n_exp
    n_valid = tile_end[-1:]
    tile_id = jnp.minimum(jnp.arange(n_tiles), n_valid - 1)
    tile_expert = jnp.sum((tile_end[None, :] <= tile_id[:, None]).astype(jnp.int32), axis=1)
    tile_expert = jnp.minimum(tile_expert, n_exp - 1)
    pos, w = _pos(sel2, rank, gates2, off, tt)
    xs = _dispatch(pos, hp.reshape(n, d // 2), jnp.zeros((n_tiles * tm, d // 2), jnp.uint32), tt)
    ys = _experts(tile_expert, n_valid.astype(jnp.int32), xs, w1, b1, w2, b2, tm)
    return _combine(pos, ys, w, x1, g2, ln_g, ln_b, tt, alpha)


def _to_cols(u, grid_rows):
    b, t, c = u.shape
    return u.reshape(b, grid_rows, GRID_W, c).transpose(0, 2, 1, 3).reshape(b, t, c)


def _from_cols(u, grid_rows):
    b, t, c = u.shape
    return u.reshape(b, GRID_W, grid_rows, c).transpose(0, 2, 1, 3).reshape(b, t, c)


def kernel(x, c, ctx, c_ctx, w_ada, b_ada, w_in, b_merge, conv_lru_w, conv_lru_b, lru_wa, lru_ba, lru_wx, lru_bx, lru_lambda, conv_ssd_w, conv_ssd_b, ssd_dt_bias, ssd_a_log, ssd_d, ssd_norm_w, w_br_lru, w_br_ssd, w_out, ln1_g, ln1_b, router_w, router_b, moe_w1, moe_b1, moe_w2, moe_b2, ln2_g, ln2_b):
    depth = w_ada.shape[0]
    assert depth == 1, "single-layer stack: the context tokens only supply scan states"
    bsz, t, d = x.shape
    t_ctx = ctx.shape[1]
    alpha = (2.0 * depth) ** 0.25
    grid_rows = t // GRID_W
    d_rnn = w_br_lru.shape[1]
    inner = w_br_ssd.shape[1]
    heads = inner // SSD_HEAD_DIM
    gn = SSD_GROUPS * SSD_STATE
    col_gr = d_rnn
    col_z = col_gr + d_rnn
    col_xbc = col_z + inner
    col_dt = col_xbc + inner + 2 * gn
    col_gm = col_dt + 2 * heads
    assert 2 * heads <= LANES and t_ctx % SSD_CHUNK == 0 and t % SSD_CHUNK == 0

    pad = (-(bsz + 1)) % 8
    c_all = jnp.concatenate([c, c_ctx[None, :], jnp.zeros((pad, d), F32)], axis=0)
    mod = _ada(c_all, w_ada[0], b_ada[0])
    sh1, sc1, g1, sh2, sc2, g2 = (mod[:bsz, k * d:(k + 1) * d] for k in range(6))
    csh1, csc1 = mod[bsz:bsz + 1, 0:d], mod[bsz:bsz + 1, d:2 * d]

    w_in_b = w_in[0].astype(BF16)

    tt = 32
    xall_tm = jnp.concatenate([jnp.transpose(ctx, (1, 0, 2)), jnp.transpose(x, (1, 0, 2))], axis=0)
    sh_tm = jnp.stack([jnp.broadcast_to(csh1, (bsz, d)), sh1])
    sc_tm = jnp.stack([jnp.broadcast_to(csc1, (bsz, d)), sc1])
    seg = lambda i: jnp.where(i * tt >= t_ctx, 1, 0)
    xr_all = _inproj(xall_tm, sh_tm, sc_tm, w_in_b[:, :col_gr], (tt, bsz), 1024, sel=seg)
    w_grgm = jnp.concatenate([w_in_b[:, col_gr:col_z], w_in_b[:, col_gm:col_gm + d]], axis=1)
    grgm = _inproj(xall_tm, sh_tm, sc_tm, w_grgm, (tt, bsz), w_grgm.shape[1], sel=seg, first=t_ctx // tt,
                   count=t // tt, out_dtype=BF16)
    p_f = _lru_params(lru_wa[0, 0], lru_ba[0, 0], lru_wx[0, 0], lru_bx[0, 0], lru_lambda[0, 0], 256)
    p_b = _lru_params(lru_wa[0, 1], lru_ba[0, 1], lru_wx[0, 1], lru_bx[0, 1], lru_lambda[0, 1], 256)
    u_lru_tm = _lru(xr_all, grgm, p_f, p_b, conv_lru_w[0], conv_lru_b[0], b_merge[0, :d],
                    w_br_lru[0].astype(BF16), t_ctx, tt)

    x_cm = _to_cols(x, grid_rows)
    sh_b, sc_b = sh1[:, None, :], sc1[:, None, :]
    csh_b, csc_b = csh1[None], csc1[None]
    zero = lambda i: 0
    w_xbc = w_in_b[:, col_xbc:col_dt]
    w_dt = jnp.pad(w_in_b[:, col_dt:col_gm], ((0, 0), (0, LANES - 2 * heads)))
    w_zgm = jnp.concatenate([w_in_b[:, col_z:col_xbc], w_in_b[:, col_gm + d:]], axis=1)
    xbc = _inproj(x_cm, sh_b, sc_b, w_xbc, (1, t), 512, conv_w=conv_ssd_w[0], conv_b=conv_ssd_b[0], out_dtype=BF16)
    xbc_c = _inproj(ctx, csh_b, csc_b, w_xbc, (1, t_ctx), 512, sel=zero, conv_w=conv_ssd_w[0],
                    conv_b=conv_ssd_b[0], out_dtype=BF16)
    dt_raw = _inproj(x_cm, sh_b, sc_b, w_dt, (1, 512), LANES)
    dt_raw_c = _inproj(ctx, csh_b, csc_b, w_dt, (1, t_ctx), LANES, sel=zero)
    zgm = _inproj(x_cm, sh_b, sc_b, w_zgm, (1, 512), w_zgm.shape[1], out_dtype=BF16)

    lane_pad = LANES - 2 * heads
    dtb = jnp.pad(ssd_dt_bias[0].reshape(1, 2 * heads), ((0, 0), (0, lane_pad)))
    alog = jnp.pad(ssd_a_log[0].reshape(1, 2 * heads), ((0, 0), (0, lane_pad)))
    dsk = jnp.repeat(ssd_d[0], SSD_HEAD_DIM).reshape(1, inner)
    head_of_lane = jnp.arange(inner) // SSD_HEAD_DIM

    def expand(lane0):
        e = (jnp.arange(LANES)[:, None] == head_of_lane[None, :] + lane0).astype(BF16)
        return jnp.concatenate([e, e], axis=0)

    gw = inner // SSD_GROUPS
    s0 = jnp.zeros((bsz, SSD_GROUPS, SSD_STATE, gw), F32)
    common = (dtb, alog)
    (st_f,) = _ssd(xbc_c, dt_raw_c, *common, expand(0), dsk, s0, reverse=False, with_y=False, add_skip=False, lane0=0)
    (st_b,) = _ssd(xbc_c, dt_raw_c, *common, expand(heads), dsk, s0, reverse=True, with_y=False, add_skip=False,
                   lane0=heads)
    y_f, _ = _ssd(xbc, dt_raw, *common, expand(0), dsk, st_f, reverse=False, with_y=True, add_skip=True, lane0=0)
    y_b, _ = _ssd(xbc, dt_raw, *common, expand(heads), dsk, st_b, reverse=True, with_y=True, add_skip=False,
                  lane0=heads)

    group_of_lane = jnp.arange(inner) // gw
    gsum = (group_of_lane[:, None] == jnp.arange(LANES)[None, :]).astype(BF16)
    gexp = jnp.concatenate([gsum.T, gsum.T], axis=0)
    u_ssd_cm = _ssdbr(y_f, y_b, zgm, ssd_norm_w[0], b_merge[0, d:], gsum, gexp, w_br_ssd[0].astype(BF16), 256)

    u_lru = jnp.transpose(u_lru_tm, (1, 0, 2))
    u_ssd = _from_cols(u_ssd_cm, grid_rows)
    n_exp = router_w.shape[2]
    rw = jnp.pad(router_w[0], ((0, 0), (0, LANES - n_exp)))
    rw_hi = rw.astype(BF16)
    rw_lo = (rw - rw_hi.astype(F32)).astype(BF16)
    rw = jnp.concatenate([rw_hi, rw_hi, rw_lo], axis=0)
    rb = jnp.pad(router_b[0].reshape(1, n_exp), ((0, 0), (0, LANES - n_exp)), constant_values=NEG_BIG)
    x1, hp, gates, sel = _out(u_lru, u_ssd, x, g1[:, None, :], sh2[:, None, :], sc2[:, None, :],
                              w_out[0].astype(BF16), ln1_g[0], ln1_b[0], rw, rb, 512, alpha)

    return _moe(hp, gates, sel, x1, g2[:, None, :], moe_w1[0], moe_b1[0], moe_w2[0], moe_b2[0],
                ln2_g[0], ln2_b[0], alpha, 512, 512)
```

```python
import functools

import jax
import jax.numpy as jnp
from jax import lax
from jax.experimental import pallas as pl
from jax.experimental.pallas import tpu as pltpu

F32 = jnp.float32
BF16 = jnp.bfloat16
HIGHEST = lax.Precision.HIGHEST

GRID_W = 64
LRU_BLOCK_W = 64
LRU_C = 8.0
CONV_W = 4
SSD_HEAD_DIM = 64
SSD_GROUPS = 8
SSD_STATE = 128
SSD_CHUNK = 128
MOE_TOP_K = 4
SWIGLU_LIMIT = 7.0
SWIGLU_ALPHA = 1.702
LN_EPS = 1e-5
RMS_EPS = 1e-5
LANES = 128
NEG_BIG = -1e30
CONV_SUBTILE = 256
VMEM_LIMIT = 56 * 1024 * 1024


def _cparams(*sem):
    return pltpu.CompilerParams(dimension_semantics=sem, vmem_limit_bytes=VMEM_LIMIT)


def _ln_rows(x):
    mu = jnp.mean(x, axis=-1, keepdims=True)
    xc = x - mu
    var = jnp.mean(xc * xc, axis=-1, keepdims=True)
    return xc * lax.rsqrt(var + LN_EPS)


def _sigmoid(x):
    return 0.5 * (jnp.tanh(0.5 * x) + 1.0)


def _silu(x):
    return x * _sigmoid(x)


def _softplus(x):
    return jnp.maximum(x, 0.0) + jnp.log(1.0 + jnp.exp(-jnp.abs(x)))


def _ada_kernel(c_ref, w_ref, b_ref, o_ref):
    c = c_ref[...]
    o_ref[...] = jnp.dot(_silu(c), w_ref[...], precision=HIGHEST, preferred_element_type=F32) + b_ref[...]


def _ada(c_all, w, b):
    m, d = c_all.shape
    n = w.shape[1]
    tn = 1024
    return pl.pallas_call(
        _ada_kernel,
        out_shape=jax.ShapeDtypeStruct((m, n), F32),
        grid=(n // tn,),
        in_specs=[pl.BlockSpec((m, d), lambda j: (0, 0)),
                  pl.BlockSpec((d, tn), lambda j: (0, j)),
                  pl.BlockSpec((1, tn), lambda j: (0, j))],
        out_specs=pl.BlockSpec((m, tn), lambda j: (0, j)),
        compiler_params=_cparams("arbitrary"),
        name="ada",
    )(c_all, w, b.reshape(1, n))


def _inproj_kernel(x_ref, sh_ref, sc_ref, w_ref, *rest, conv):
    if conv:
        cw_ref, cb_ref, o_ref, h_ref = rest
    else:
        o_ref, h_ref = rest
    rows = h_ref.shape[0]

    @pl.when(pl.program_id(1) == 0)
    def _():
        h = _ln_rows(x_ref[...]) * (1.0 + sc_ref[...]) + sh_ref[...]
        h_ref[...] = h.reshape(h_ref.shape).astype(BF16)

    if not conv:
        acc = jnp.dot(h_ref[...], w_ref[...], preferred_element_type=F32)
        o_ref[...] = acc.reshape(o_ref.shape).astype(o_ref.dtype)
        return

    def taps(a, cw, cb, n, head, tail):
        r = lax.broadcasted_iota(jnp.int32, a.shape, 0)
        t0, t1, t3 = pltpu.roll(a, 2, 0), pltpu.roll(a, 1, 0), pltpu.roll(a, n - 1, 0)
        if head:
            t0, t1 = jnp.where(r >= 2, t0, 0.0), jnp.where(r >= 1, t1, 0.0)
        if tail:
            t3 = jnp.where(r < n - 1, t3, 0.0)
        return _silu(a * cw[2:3, :] + cb + t0 * cw[0:1, :] + t1 * cw[1:2, :] + t3 * cw[3:4, :])

    edge = 32
    half = edge // 2
    for lo in range(0, w_ref.shape[1], CONV_SUBTILE):
        cols = slice(lo, lo + CONV_SUBTILE)
        acc = jnp.dot(h_ref[...], w_ref[:, cols], preferred_element_type=F32)
        cw, cb = cw_ref[:, cols], cb_ref[:, cols]
        o_ref[0, :, cols] = taps(acc, cw, cb, rows, False, False).astype(o_ref.dtype)
        o_ref[0, 0:half, cols] = taps(acc[0:edge], cw, cb, edge, True, False)[0:half].astype(o_ref.dtype)
        o_ref[0, rows - half:rows, cols] = taps(acc[rows - edge:rows], cw, cb, edge, False, True)[half:].astype(
            o_ref.dtype)


def _inproj(x3, sh3, sc3, w, tile, tn, sel=None, conv_w=None, conv_b=None, out_dtype=F32, first=0, count=None):
    gdim, rdim, d = x3.shape
    g, r = tile
    n = w.shape[1]
    rows = g * r
    nblk_g = gdim // g if count is None else count
    nblk_r = rdim // r
    if sel is None:
        sel = lambda i: i
    conv = conv_w is not None
    mg, mr = sh3.shape[1], sh3.shape[2]
    in_specs = [pl.BlockSpec((g, r, d), lambda i, j: ((i // nblk_r) + first, i % nblk_r, 0)),
                pl.BlockSpec((1, mg, mr), lambda i, j: (sel((i // nblk_r) + first), 0, 0)),
                pl.BlockSpec((1, mg, mr), lambda i, j: (sel((i // nblk_r) + first), 0, 0)),
                pl.BlockSpec((d, tn), lambda i, j: (0, j))]
    args = [x3, sh3, sc3, w]
    if conv:
        assert r == rdim and g == 1
        in_specs += [pl.BlockSpec((CONV_W, tn), lambda i, j: (0, j)),
                     pl.BlockSpec((1, tn), lambda i, j: (0, j))]
        args += [conv_w, conv_b.reshape(1, n)]
    out_g = nblk_g * g
    return pl.pallas_call(
        functools.partial(_inproj_kernel, conv=conv),
        out_shape=jax.ShapeDtypeStruct((out_g, rdim, n), out_dtype),
        grid=(nblk_g * nblk_r, n // tn),
        in_specs=in_specs,
        out_specs=pl.BlockSpec((g, r, tn), lambda i, j: (i // nblk_r, i % nblk_r, j)),
        scratch_shapes=[pltpu.VMEM((rows, d), BF16)],
        compiler_params=_cparams("arbitrary", "arbitrary"),
        name="inproj_conv" if conv else "inproj",
    )(*args)


def _lru_gates(win_ref, w_ref, ba_ref, bx_ref, lam_ref, cw_ref, cb_ref, a_ref, b_ref, tt):
    bsz, c = win_ref.shape[1], win_ref.shape[2]
    cw = cw_ref[...]
    u = cb_ref[...].reshape(1, 1, c) + sum(win_ref[pl.ds(j, tt)] * cw[j:j + 1, :].reshape(1, 1, c)
                                           for j in range(CONV_W))
    u2 = u.reshape(tt * bsz, c)
    ub = u2.astype(BF16)
    sp = _softplus(-lam_ref[...])
    pack = w_ref.shape[1]
    for j in range(c // pack):
        lo = j * pack
        pre = jnp.dot(ub[:, lo:lo + pack], w_ref[j], preferred_element_type=F32)
        r = _sigmoid(pre[:, :pack] + ba_ref[:, lo:lo + pack])
        i = _sigmoid(pre[:, pack:] + bx_ref[:, lo:lo + pack])
        log_a = (-LRU_C) * r * sp[:, lo:lo + pack]
        a = jnp.exp(log_a)
        bt = jnp.sqrt(1.0 - jnp.exp(2.0 * log_a)) * (i * u2[:, lo:lo + pack])
        a_ref[:, :, lo:lo + pack] = a.reshape(tt, bsz, pack)
        b_ref[:, :, lo:lo + pack] = bt.reshape(tt, bsz, pack)


def _lru_fill_window(win_ref, x_ref, prev_ref, next_ref, at_start, at_end, tt):
    zero2 = jnp.zeros(prev_ref.shape, F32)
    win_ref[pl.ds(0, 2)] = jnp.where(at_start, zero2, prev_ref[...])
    win_ref[pl.ds(2, tt)] = x_ref[...]
    win_ref[pl.ds(tt + 2, 1)] = jnp.where(at_end, jnp.zeros(next_ref.shape, F32), next_ref[...])


def _lru_bwd_kernel(x_ref, prev_ref, next_ref, w_ref, ba_ref, bx_ref, lam_ref, cw_ref, cb_ref,
                    hb_ref, win_ref, a_ref, b_ref, h_ref, *, tt, ncb, nlb):
    i = pl.program_id(0)
    blk = jnp.where(i < ncb, ncb - 1 - i, ncb + nlb - 1 - (i - ncb))
    at_start = (blk == 0) | (blk == ncb)
    at_end = (blk == ncb - 1) | (blk == ncb + nlb - 1)

    @pl.when(i == 0)
    def _():
        h_ref[...] = jnp.zeros(h_ref.shape, F32)

    _lru_fill_window(win_ref, x_ref, prev_ref, next_ref, at_start, at_end, tt)
    _lru_gates(win_ref, w_ref, ba_ref, bx_ref, lam_ref, cw_ref, cb_ref, a_ref, b_ref, tt)

    def step(k, h):
        t = tt - 1 - k
        h = a_ref[t] * h + b_ref[t]
        a_ref[t] = h
        return h

    h_ref[...] = lax.fori_loop(0, tt, step, h_ref[...])

    @pl.when(i >= ncb)
    def _():
        hb_ref[...] = a_ref[...]


def _lru_fwd_kernel(x_ref, prev_ref, next_ref, w_ref, ba_ref, bx_ref, lam_ref, cw_ref, cb_ref,
                    hb_ref, gr_ref, gm_ref, bm_ref, wbr_ref,
                    o_ref, win_ref, a_ref, b_ref, h_ref, *, tt, ncb, nlb):
    i = pl.program_id(0)
    at_start = (i == 0) | (i == ncb)
    at_end = (i == ncb - 1) | (i == ncb + nlb - 1)

    @pl.when(i == 0)
    def _():
        h_ref[...] = jnp.zeros(h_ref.shape, F32)

    _lru_fill_window(win_ref, x_ref, prev_ref, next_ref, at_start, at_end, tt)
    _lru_gates(win_ref, w_ref, ba_ref, bx_ref, lam_ref, cw_ref, cb_ref, a_ref, b_ref, tt)

    def step(t, h):
        h = a_ref[t] * h + b_ref[t]
        a_ref[t] = h
        return h

    h_ref[...] = lax.fori_loop(0, tt, step, h_ref[...])

    @pl.when(i >= ncb)
    def _():
        bsz, c = h_ref.shape
        rows = tt * bsz
        a_lat = (a_ref[...] + hb_ref[...]) * jax.nn.gelu(gr_ref[...].astype(F32), approximate=True)
        proj = jnp.dot(a_lat.reshape(rows, c).astype(BF16), wbr_ref[...], preferred_element_type=F32)
        gate = _sigmoid(gm_ref[...].astype(F32).reshape(rows, -1) + bm_ref[...])
        o_ref[...] = (gate * proj).reshape(o_ref.shape)


def _lru_specs(tt, bsz, c, blk_of, ttot):
    half = tt // 2
    return [pl.BlockSpec((tt, bsz, c), lambda i: (blk_of(i), 0, 0)),
            pl.BlockSpec((2, bsz, c), lambda i: (jnp.maximum(blk_of(i) * half - 1, 0), 0, 0)),
            pl.BlockSpec((1, bsz, c), lambda i: (jnp.minimum(blk_of(i) * tt + tt, ttot - 1), 0, 0))]


def _lru(xr_all, grgm, p_f, p_b, conv_w, conv_b, b_merge_lru, w_br, t_ctx, tt):
    ttot, bsz, c = xr_all.shape
    ncb, nlb = t_ctx // tt, (ttot - t_ctx) // tt
    nblk = ncb + nlb
    d = w_br.shape[1]
    const2 = lambda i: (0, 0)
    const3 = lambda i: (0, 0, 0)
    par_specs = [pl.BlockSpec(p_f[0].shape, const3), pl.BlockSpec((1, c), const2), pl.BlockSpec((1, c), const2),
                 pl.BlockSpec((1, c), const2), pl.BlockSpec((CONV_W, c), const2), pl.BlockSpec((1, c), const2)]
    scratch = [pltpu.VMEM((tt + 3, bsz, c), F32), pltpu.VMEM((tt, bsz, c), F32),
               pltpu.VMEM((tt, bsz, c), F32), pltpu.VMEM((bsz, c), F32)]

    bwd_blk = lambda i: jnp.where(i < ncb, ncb - 1 - i, ncb + nlb - 1 - (i - ncb))
    hb = pl.pallas_call(
        functools.partial(_lru_bwd_kernel, tt=tt, ncb=ncb, nlb=nlb),
        out_shape=jax.ShapeDtypeStruct((nlb * tt, bsz, c), F32),
        grid=(nblk,),
        in_specs=_lru_specs(tt, bsz, c, bwd_blk, ttot) + par_specs,
        out_specs=pl.BlockSpec((tt, bsz, c), lambda i: (jnp.where(i < ncb, nlb - 1, nblk - 1 - i), 0, 0)),
        scratch_shapes=scratch,
        compiler_params=_cparams("arbitrary"),
        name="lru_bwd",
    )(xr_all, xr_all, xr_all, *p_b, conv_w, conv_b.reshape(1, c))

    assert c == d
    lat = lambda i: (jnp.maximum(i - ncb, 0), 0, 0)
    lat1 = lambda i: (jnp.maximum(i - ncb, 0), 0, 1)
    return pl.pallas_call(
        functools.partial(_lru_fwd_kernel, tt=tt, ncb=ncb, nlb=nlb),
        out_shape=jax.ShapeDtypeStruct((nlb * tt, bsz, d), F32),
        grid=(nblk,),
        in_specs=_lru_specs(tt, bsz, c, lambda i: i, ttot) + par_specs + [
            pl.BlockSpec((tt, bsz, c), lat), pl.BlockSpec((tt, bsz, c), lat), pl.BlockSpec((tt, bsz, d), lat1),
            pl.BlockSpec((1, d), const2), pl.BlockSpec((c, d), const2)],
        out_specs=pl.BlockSpec((tt, bsz, d), lat),
        scratch_shapes=scratch,
        compiler_params=_cparams("arbitrary"),
        name="lru_fwd",
    )(xr_all, xr_all, xr_all, *p_f, conv_w, conv_b.reshape(1, c), hb, grgm, grgm, b_merge_lru.reshape(1, d), w_br)


def _lru_params(wa, ba, wx, bx, lam, pack):
    nb, bw, _ = wa.shape
    per = pack // bw
    c = nb * bw

    def bd(w):
        w4 = w.reshape(nb // per, per, bw, bw)
        eye = jnp.eye(per, dtype=w.dtype)
        return jnp.einsum('gpde,pq->gpdqe', w4, eye).reshape(nb // per, pack, pack)

    w = jnp.concatenate([bd(wa), bd(wx)], axis=-1).astype(BF16)
    return w, ba.reshape(1, c), bx.reshape(1, c), lam.reshape(1, c)


def _hi_lo(v):
    hi = v.astype(BF16)
    lo = (v - hi.astype(F32)).astype(BF16)
    return jnp.concatenate([hi, lo], axis=-1)


def _ssd_kernel(xbc_ref, dt_ref, dtb_ref, alog_ref, e_ref, dsk_ref, h0_ref, *outs,
                reverse, with_y, add_skip, lane0, inner):
    if with_y:
        y_ref, hfin_ref, st_ref = outs
    else:
        hfin_ref, st_ref = outs
    i = pl.program_id(1)
    q = SSD_CHUNK
    n = SSD_STATE
    gw = inner // SSD_GROUPS
    hpg = gw // SSD_HEAD_DIM

    @pl.when(i == 0)
    def _():
        st_ref[...] = h0_ref[0]

    dt = _softplus(dt_ref[0] + dtb_ref[...])
    da = dt * (-jnp.exp(alog_ref[...]))
    ri = lax.broadcasted_iota(jnp.int32, (q, q), 0)
    ci = lax.broadcasted_iota(jnp.int32, (q, q), 1)
    tri = (ri <= ci) if reverse else (ri >= ci)
    acum = jnp.dot(tri.astype(F32), da, precision=HIGHEST, preferred_element_type=F32)
    a_tot = jnp.sum(da, axis=0, keepdims=True)
    w_state = jnp.exp(a_tot - acum) * dt
    e2 = e_ref[...]
    ws_x = jnp.dot(w_state.astype(BF16), e2[0:LANES, :], preferred_element_type=F32)
    dec_x = jnp.dot(_hi_lo(jnp.broadcast_to(jnp.exp(a_tot), (8, LANES))), e2,
                    preferred_element_type=F32)[0:1, :]
    if with_y:
        eac_x = jnp.dot(_hi_lo(jnp.exp(acum)), e2, preferred_element_type=F32)
        acum_t = acum.T
        dt_t = dt.T
        rb = lax.broadcasted_iota(jnp.int32, (hpg * q, gw), 0) // q
        lb = lax.broadcasted_iota(jnp.int32, (hpg * q, gw), 1) // SSD_HEAD_DIM
        bd_mask = rb == lb

    for g in range(SSD_GROUPS):
        lo = g * gw
        xg = xbc_ref[0, :, lo:lo + gw]
        bg = xbc_ref[0, :, inner + g * n:inner + (g + 1) * n]
        cg = xbc_ref[0, :, inner + (SSD_GROUPS + g) * n:inner + (SSD_GROUPS + g + 1) * n]
        xgf = xg.astype(F32)
        xw = (xgf * ws_x[:, lo:lo + gw]).astype(BF16)
        st = st_ref[g]
        upd = lax.dot_general(bg, xw, (((0,), (0,)), ((), ())), preferred_element_type=F32)
        st_ref[g] = dec_x[:, lo:lo + gw] * st + upd
        if with_y:
            y_off = jnp.dot(cg, st.astype(BF16), preferred_element_type=F32) * eac_x[:, lo:lo + gw]
            cb = lax.dot_general(cg, bg, (((1,), (1,)), ((), ())), preferred_element_type=F32)
            ls = []
            for r in range(hpg):
                lane = lane0 + g * hpg + r
                seg = acum[:, lane:lane + 1] - acum_t[lane:lane + 1, :]
                l_h = cb * jnp.exp(jnp.where(tri, seg, NEG_BIG)) * dt_t[lane:lane + 1, :]
                ls.append(l_h.astype(BF16))
            lcat = jnp.concatenate(ls, axis=1)
            xbd = jnp.where(bd_mask, jnp.concatenate([xg] * hpg, axis=0), jnp.zeros((), BF16))
            y = y_off + jnp.dot(lcat, xbd, preferred_element_type=F32)
            if add_skip:
                y = y + dsk_ref[:, lo:lo + gw] * xgf
            y_ref[0, :, lo:lo + gw] = y

    @pl.when(i == pl.num_programs(1) - 1)
    def _():
        hfin_ref[0] = st_ref[...]


def _ssd(xbc, dt_raw, dtb, alog, e2, dsk, h0, *, reverse, with_y, add_skip, lane0):
    bsz, s, width = xbc.shape
    inner = e2.shape[1]
    nc = s // SSD_CHUNK
    gw = inner // SSD_GROUPS
    cidx = (lambda b, i: (b, nc - 1 - i, 0)) if reverse else (lambda b, i: (b, i, 0))
    const2 = lambda b, i: (0, 0)
    st_spec = pl.BlockSpec((1, SSD_GROUPS, SSD_STATE, gw), lambda b, i: (b, 0, 0, 0))
    st_shape = jax.ShapeDtypeStruct((bsz, SSD_GROUPS, SSD_STATE, gw), F32)
    out_shape, out_specs = [st_shape], [st_spec]
    if with_y:
        out_shape = [jax.ShapeDtypeStruct((bsz, s, inner), F32)] + out_shape
        out_specs = [pl.BlockSpec((1, SSD_CHUNK, inner), cidx)] + out_specs
    return pl.pallas_call(
        functools.partial(_ssd_kernel, reverse=reverse, with_y=with_y, add_skip=add_skip, lane0=lane0, inner=inner),
        out_shape=out_shape,
        grid=(bsz, nc),
        in_specs=[pl.BlockSpec((1, SSD_CHUNK, width), cidx),
                  pl.BlockSpec((1, SSD_CHUNK, LANES), cidx),
                  pl.BlockSpec((1, LANES), const2), pl.BlockSpec((1, LANES), const2),
                  pl.BlockSpec(e2.shape, const2), pl.BlockSpec((1, inner), const2), st_spec],
        out_specs=out_specs,
        scratch_shapes=[pltpu.VMEM((SSD_GROUPS, SSD_STATE, gw), F32)],
        compiler_params=_cparams("arbitrary", "arbitrary"),
        name="ssd_y" if with_y else "ssd_state",
    )(xbc, dt_raw, dtb, alog, e2, dsk, h0)


def _ssdbr_kernel(yf_ref, yb_ref, z_ref, gm_ref, nw_ref, bm_ref, gsum_ref, gexp_ref, w_ref, o_ref, *, group_w):
    y = (yf_ref[0] + yb_ref[0]) * _silu(z_ref[0].astype(F32))
    ms = jnp.dot((y * y).astype(BF16), gsum_ref[...], preferred_element_type=F32) * (1.0 / group_w)
    rs = lax.rsqrt(ms + RMS_EPS)
    rs_x = jnp.dot(_hi_lo(rs), gexp_ref[...], preferred_element_type=F32)
    yn = (y * rs_x * nw_ref[...]).astype(BF16)
    proj = jnp.dot(yn, w_ref[...], preferred_element_type=F32)
    o_ref[0] = _sigmoid(gm_ref[0].astype(F32) + bm_ref[...]) * proj


def _ssdbr(y_f, y_b, zgm, norm_w, b_merge_ssd, gsum, gexp, w_br, rows):
    bsz, s, inner = y_f.shape
    d = w_br.shape[1]
    assert inner % d == 0
    tok = lambda b, i: (b, i, 0)
    const2 = lambda b, i: (0, 0)
    return pl.pallas_call(
        functools.partial(_ssdbr_kernel, group_w=inner // SSD_GROUPS),
        out_shape=jax.ShapeDtypeStruct((bsz, s, d), F32),
        grid=(bsz, s // rows),
        in_specs=[pl.BlockSpec((1, rows, inner), tok), pl.BlockSpec((1, rows, inner), tok),
                  pl.BlockSpec((1, rows, inner), tok),
                  pl.BlockSpec((1, rows, d), lambda b, i: (b, i, inner // d)),
                  pl.BlockSpec((1, inner), const2), pl.BlockSpec((1, d), const2),
                  pl.BlockSpec(gsum.shape, const2), pl.BlockSpec(gexp.shape, const2),
                  pl.BlockSpec((inner, d), const2)],
        out_specs=pl.BlockSpec((1, rows, d), tok),
        compiler_params=_cparams("arbitrary", "arbitrary"),
        name="ssdbr",
    )(y_f, y_b, zgm, zgm, norm_w.reshape(1, inner), b_merge_ssd.reshape(1, d), gsum, gexp, w_br)


def _pack_bf16_pair(lo, hi):
    lo_bits = lax.bitcast_convert_type(lo.astype(BF16).astype(F32), jnp.uint32)
    hi_bits = lax.bitcast_convert_type(hi.astype(BF16).astype(F32), jnp.uint32)
    return (lo_bits >> 16) | hi_bits


def _unpack_bf16_pair(p):
    lo = lax.bitcast_convert_type(p << 16, F32).astype(BF16)
    hi = lax.bitcast_convert_type(p & jnp.uint32(0xFFFF0000), F32).astype(BF16)
    return jnp.concatenate([lo, hi], axis=1)


def _out_kernel(ul_ref, us_ref, x_ref, g1_ref, sh2_ref, sc2_ref, wout_ref, lg_ref, lb_ref, rw_ref, rb_ref,
                x1_ref, hp_ref, gates_ref, sel_ref, *, alpha):
    u = (ul_ref[0] + us_ref[0]).astype(BF16)
    mix = jnp.dot(u, wout_ref[...], preferred_element_type=F32)
    x1 = _ln_rows(alpha * x_ref[0] + g1_ref[0] * mix) * lg_ref[...] + lb_ref[...]
    x1_ref[0] = x1
    h2 = _ln_rows(x1) * (1.0 + sc2_ref[0]) + sh2_ref[0]
    half = h2.shape[1] // 2
    hp_ref[0] = _pack_bf16_pair(h2[:, :half], h2[:, half:])
    h_hi = h2.astype(BF16)
    h_lo = (h2 - h_hi.astype(F32)).astype(BF16)
    logits = jnp.dot(jnp.concatenate([h_hi, h_lo, h_hi], axis=1), rw_ref[...],
                     preferred_element_type=F32) + rb_ref[...]
    lane = lax.broadcasted_iota(jnp.int32, logits.shape, 1)
    work = logits
    sel = jnp.zeros(logits.shape, jnp.bool_)
    top = None
    for k in range(MOE_TOP_K):
        m = jnp.max(work, axis=1, keepdims=True)
        if k == 0:
            top = m
        idx = jnp.min(jnp.where(work == m, lane, LANES), axis=1, keepdims=True)
        pick = lane == idx
        sel = sel | pick
        work = jnp.where(pick, 3.0 * NEG_BIG, work)
    e = jnp.where(sel, jnp.exp(logits - top), 0.0)
    gates_ref[0] = e / jnp.sum(e, axis=1, keepdims=True)
    sel_ref[0] = sel.astype(F32)


def _out(u_l, u_s, x, g1, sh2, sc2, w_out, ln_g, ln_b, rw, rb, rows, alpha):
    bsz, t, d = x.shape
    tok = lambda b, i: (b, i, 0)
    per_b = lambda b, i: (b, 0, 0)
    const2 = lambda b, i: (0, 0)
    return pl.pallas_call(
        functools.partial(_out_kernel, alpha=alpha),
        out_shape=[jax.ShapeDtypeStruct((bsz, t, d), F32), jax.ShapeDtypeStruct((bsz, t, d // 2), jnp.uint32),
                   jax.ShapeDtypeStruct((bsz, t, LANES), F32), jax.ShapeDtypeStruct((bsz, t, LANES), F32)],
        grid=(bsz, t // rows),
        in_specs=[pl.BlockSpec((1, rows, d), tok), pl.BlockSpec((1, rows, d), tok), pl.BlockSpec((1, rows, d), tok),
                  pl.BlockSpec((1, 1, d), per_b), pl.BlockSpec((1, 1, d), per_b), pl.BlockSpec((1, 1, d), per_b),
                  pl.BlockSpec((d, d), const2), pl.BlockSpec((1, d), const2), pl.BlockSpec((1, d), const2),
                  pl.BlockSpec((3 * d, LANES), const2), pl.BlockSpec((1, LANES), const2)],
        out_specs=[pl.BlockSpec((1, rows, d), tok), pl.BlockSpec((1, rows, d // 2), tok),
                   pl.BlockSpec((1, rows, LANES), tok), pl.BlockSpec((1, rows, LANES), tok)],
        compiler_params=_cparams("arbitrary", "arbitrary"),
        name="out",
    )(u_l, u_s, x, g1, sh2, sc2, w_out, ln_g.reshape(1, d), ln_b.reshape(1, d), rw, rb)


def _rank_kernel(sel_ref, rank_ref, cnt_ref, carry_ref):
    @pl.when(pl.program_id(0) == 0)
    def _():
        carry_ref[...] = jnp.zeros(carry_ref.shape, F32)

    s = sel_ref[...]
    tt = s.shape[0]
    ri = lax.broadcasted_iota(jnp.int32, (tt, tt), 0)
    ci = lax.broadcasted_iota(jnp.int32, (tt, tt), 1)
    earlier = (ri > ci).astype(BF16)
    rank_ref[...] = jnp.dot(earlier, s.astype(BF16), preferred_element_type=F32) + carry_ref[0:1, :]
    carry_ref[...] = carry_ref[...] + jnp.sum(s, axis=0, keepdims=True)
    cnt_ref[...] = carry_ref[...]


def _rank(sel, tt):
    n = sel.shape[0]
    return pl.pallas_call(
        _rank_kernel,
        out_shape=[jax.ShapeDtypeStruct((n, LANES), F32), jax.ShapeDtypeStruct((8, LANES), F32)],
        grid=(n // tt,),
        in_specs=[pl.BlockSpec((tt, LANES), lambda i: (i, 0))],
        out_specs=[pl.BlockSpec((tt, LANES), lambda i: (i, 0)), pl.BlockSpec((8, LANES), lambda i: (0, 0))],
        scratch_shapes=[pltpu.VMEM((8, LANES), F32)],
        compiler_params=_cparams("arbitrary"),
        name="moe_rank",
    )(sel)


def _pos_kernel(sel_ref, rank_ref, gates_ref, off_ref, pos_ref, w_ref):
    avail = sel_ref[...] > 0.5
    posf = off_ref[...] + rank_ref[...]
    gates = gates_ref[...]
    lane = lax.broadcasted_iota(jnp.int32, posf.shape, 1)
    cols_p = jnp.zeros(posf.shape, F32)
    cols_w = jnp.zeros(posf.shape, F32)
    for k in range(MOE_TOP_K):
        m = jnp.min(jnp.where(avail, lane, LANES), axis=1, keepdims=True)
        pick = lane == m
        cols_p = jnp.where(lane == k, jnp.sum(jnp.where(pick, posf, 0.0), axis=1, keepdims=True), cols_p)
        cols_w = jnp.where(lane == k, jnp.sum(jnp.where(pick, gates, 0.0), axis=1, keepdims=True), cols_w)
        avail = avail & jnp.logical_not(pick)
    w_ref[...] = cols_w
    pos_ref[...] = cols_p.astype(jnp.int32)


def _pos(sel, rank, gates, off, tt):
    n = sel.shape[0]
    tok = lambda i: (i, 0)
    return pl.pallas_call(
        _pos_kernel,
        out_shape=[jax.ShapeDtypeStruct((n, LANES), jnp.int32), jax.ShapeDtypeStruct((n, LANES), F32)],
        grid=(n // tt,),
        in_specs=[pl.BlockSpec((tt, LANES), tok), pl.BlockSpec((tt, LANES), tok), pl.BlockSpec((tt, LANES), tok),
                  pl.BlockSpec((1, LANES), lambda i: (0, 0))],
        out_specs=[pl.BlockSpec((tt, LANES), tok), pl.BlockSpec((tt, LANES), tok)],
        compiler_params=_cparams("arbitrary"),
        name="moe_pos",
    )(sel, rank, gates, off)


ROW_GROUP = 8
ROW_TILE = 8


def _dispatch_kernel(pos_ref, hp_ref, xs_init_hbm, xs_hbm, sem, *, tt):
    del xs_init_hbm

    def body(i, carry):
        row0 = pl.multiple_of(i * ROW_GROUP, ROW_GROUP)
        p0 = i * (ROW_GROUP * MOE_TOP_K)
        for u in range(ROW_GROUP):
            src = hp_ref.at[pl.ds(row0 + u, 1)]
            for k in range(MOE_TOP_K):
                dst = xs_hbm.at[pl.ds(pos_ref[p0 + u * MOE_TOP_K + k], 1)]
                pltpu.make_async_copy(src, dst, sem).start(priority=k % 2)
        return carry

    lax.fori_loop(0, tt // ROW_GROUP, body, 0)
    for k in range(MOE_TOP_K):
        pltpu.make_async_copy(hp_ref, xs_hbm.at[pl.ds(0, tt)], sem).wait()


def _dispatch(pos, hp, xs_init, tt):
    n, half = hp.shape
    any_spec = pl.BlockSpec(memory_space=pl.ANY)
    return pl.pallas_call(
        functools.partial(_dispatch_kernel, tt=tt),
        out_shape=jax.ShapeDtypeStruct(xs_init.shape, xs_init.dtype),
        grid=(n // tt,),
        in_specs=[pl.BlockSpec((tt * MOE_TOP_K,), lambda i: (i,), memory_space=pltpu.SMEM),
                  pl.BlockSpec((tt, half), lambda i: (i, 0)), any_spec],
        out_specs=any_spec,
        scratch_shapes=[pltpu.SemaphoreType.DMA(())],
        input_output_aliases={2: 0},
        compiler_params=pltpu.CompilerParams(dimension_semantics=("arbitrary",), has_side_effects=True),
        name="moe_dispatch",
    )(pos, hp, xs_init)


def _expert_kernel(te_ref, nv_ref, xs_ref, w1_ref, b1_ref, w2_ref, b2_ref, ys_ref, w1b_ref, w2b_ref, *, ff):
    t = pl.program_id(0)
    e = te_ref[t]
    prev = te_ref[jnp.maximum(t - 1, 0)]

    @pl.when((t == 0) | (e != prev))
    def _():
        w1b_ref[...] = w1_ref[0].astype(BF16)
        w2b_ref[...] = w2_ref[0].astype(BF16)

    @pl.when(t < nv_ref[0])
    def _():
        xrow = _unpack_bf16_pair(xs_ref[...])
        gu = jnp.dot(xrow, w1b_ref[...], preferred_element_type=F32) + b1_ref[0]
        g = jnp.minimum(gu[:, :ff], SWIGLU_LIMIT)
        u = jnp.clip(gu[:, ff:], -SWIGLU_LIMIT, SWIGLU_LIMIT)
        act = ((u + 1.0) * g * _sigmoid(SWIGLU_ALPHA * g)).astype(BF16)
        y = jnp.dot(act, w2b_ref[...], preferred_element_type=F32) + b2_ref[0]
        for g in range(y.shape[1] // LANES):
            ys_ref[pl.ds(g, y.shape[0], stride=ROW_TILE), :] = y[:, g * LANES:(g + 1) * LANES]

    @pl.when(t >= nv_ref[0])
    def _():
        ys_ref[...] = jnp.zeros(ys_ref.shape, F32)


def _experts(tile_expert, n_valid, xs, w1, b1, w2, b2, tm):
    rows, half = xs.shape
    n_exp, d, ff2 = w1.shape
    ff = ff2 // 2
    per_e = lambda t, te, nv: (te[t], 0, 0)
    assert d == ROW_TILE * LANES
    return pl.pallas_call(
        functools.partial(_expert_kernel, ff=ff),
        out_shape=jax.ShapeDtypeStruct((rows * ROW_TILE, LANES), F32),
        grid_spec=pltpu.PrefetchScalarGridSpec(
            num_scalar_prefetch=2,
            grid=(rows // tm,),
            in_specs=[pl.BlockSpec((tm, half), lambda t, te, nv: (t, 0)),
                      pl.BlockSpec((1, d, ff2), per_e), pl.BlockSpec((1, 1, ff2), per_e),
                      pl.BlockSpec((1, ff, d), per_e), pl.BlockSpec((1, 1, d), per_e)],
            out_specs=pl.BlockSpec((tm * ROW_TILE, LANES), lambda t, te, nv: (t, 0)),
            scratch_shapes=[pltpu.VMEM((d, ff2), BF16), pltpu.VMEM((ff, d), BF16)]),
        compiler_params=_cparams("arbitrary"),
        name="moe_experts",
    )(tile_expert, n_valid, xs, w1, b1.reshape(n_exp, 1, ff2), w2, b2.reshape(n_exp, 1, d))


def _combine_kernel(pos_ref, ys_hbm, ys_flat_hbm, w_ref, x1_ref, g2_ref, lg_ref, lb_ref, o_ref, buf_ref, sem,
                    *, tt, alpha):
    def body(i, carry):
        row0 = pl.multiple_of(i * ROW_GROUP, ROW_GROUP)
        p0 = i * (ROW_GROUP * MOE_TOP_K)
        for u in range(ROW_GROUP):
            for k in range(MOE_TOP_K):
                src = ys_hbm.at[pos_ref[p0 + u * MOE_TOP_K + k]]
                dst = buf_ref.at[pl.ds(pl.multiple_of((k * tt + row0 + u) * ROW_TILE, ROW_TILE), ROW_TILE)]
                pltpu.make_async_copy(src, dst, sem).start(priority=k % 2)
        return carry

    lax.fori_loop(0, tt // ROW_GROUP, body, 0)
    span = tt * ROW_TILE
    for k in range(MOE_TOP_K):
        pltpu.make_async_copy(ys_flat_hbm.at[pl.ds(0, span)], buf_ref.at[pl.ds(k * span, span)], sem).wait()

    chunk = 64

    def rows(c, carry):
        r0 = pl.multiple_of(c * chunk, chunk)
        r = pl.ds(r0, chunk)
        w = w_ref[r, :]

        def slot_rows(k):
            first = (k * tt + r0) * ROW_TILE
            return jnp.concatenate([buf_ref[pl.ds(first + g, chunk, stride=ROW_TILE), :] for g in range(ROW_TILE)],
                                   axis=1)

        acc = sum(w[:, k:k + 1] * slot_rows(k) for k in range(MOE_TOP_K))
        o_ref[0, r, :] = _ln_rows(alpha * x1_ref[0, r, :] + g2_ref[0] * acc) * lg_ref[...] + lb_ref[...]
        return carry

    lax.fori_loop(0, tt // chunk, rows, 0)


def _combine(pos, ys, w, x1, g2, ln_g, ln_b, tt, alpha):
    bsz, t, d = x1.shape
    nt = t // tt
    return pl.pallas_call(
        functools.partial(_combine_kernel, tt=tt, alpha=alpha),
        out_shape=jax.ShapeDtypeStruct((bsz, t, d), F32),
        grid=(bsz, nt),
        in_specs=[pl.BlockSpec((tt * MOE_TOP_K,), lambda b, i: (b * nt + i,), memory_space=pltpu.SMEM),
                  pl.BlockSpec(memory_space=pl.ANY), pl.BlockSpec(memory_space=pl.ANY),
                  pl.BlockSpec((tt, LANES), lambda b, i: (b * nt + i, 0)),
                  pl.BlockSpec((1, tt, d), lambda b, i: (b, i, 0)),
                  pl.BlockSpec((1, 1, d), lambda b, i: (b, 0, 0)),
                  pl.BlockSpec((1, d), lambda b, i: (0, 0)), pl.BlockSpec((1, d), lambda b, i: (0, 0))],
        out_specs=pl.BlockSpec((1, tt, d), lambda b, i: (b, i, 0)),
        scratch_shapes=[pltpu.VMEM((MOE_TOP_K * tt * ROW_TILE, LANES), F32), pltpu.SemaphoreType.DMA(())],
        compiler_params=_cparams("arbitrary", "arbitrary"),
        name="moe_combine",
    )(pos, ys.reshape(-1, ROW_TILE, LANES), ys, w, x1, g2, ln_g.reshape(1, d), ln_b.reshape(1, d))


def _moe(hp, gates, sel, x1, g2, w1, b1, w2, b2, ln_g, ln_b, alpha, tt, tm):
    bsz, t, d = x1.shape
    n = bsz * t
    n_exp = w1.shape[0]
    sel2, gates2 = sel.reshape(n, LANES), gates.reshape(n, LANES)
    rank, cnt = _rank(sel2, tt)
    counts = cnt[0, :n_exp].astype(jnp.int32)
    tiles_per = (counts + tm - 1) // tm
    tile_end = jnp.cumsum(tiles_per)
    off = jnp.pad(((tile_end - tiles_per) * tm).astype(F32).reshape(1, n_exp), ((0, 0), (0, LANES - n_exp)))
    n_tiles = (n * MOE_TOP_K) // tm + n_exp
    n_valid = tile_end[-1:]
    tile_id = jnp.minimum(jnp.arange(n_tiles), n_valid - 1)
    tile_expert = jnp.sum((tile_end[None, :] <= tile_id[:, None]).astype(jnp.int32), axis=1)
    tile_expert = jnp.minimum(tile_expert, n_exp - 1)
    pos_lanes, w = _pos(sel2, rank, gates2, off, tt)
    pos = pos_lanes[:, :MOE_TOP_K].reshape(n * MOE_TOP_K)
    xs = _dispatch(pos, hp.reshape(n, d // 2), jnp.zeros((n_tiles * tm, d // 2), jnp.uint32), tt)
    ys = _experts(tile_expert, n_valid.astype(jnp.int32), xs, w1, b1, w2, b2, tm)
    return _combine(pos, ys, w, x1, g2, ln_g, ln_b, tt, alpha)


def _to_cols(u, grid_rows):
    b, t, c = u.shape
    return u.reshape(b, grid_rows, GRID_W, c).transpose(0, 2, 1, 3).reshape(b, t, c)


def _from_cols(u, grid_rows):
    b, t, c = u.shape
    return u.reshape(b, GRID_W, grid_rows, c).transpose(0, 2, 1, 3).reshape(b, t, c)


def kernel(x, c, ctx, c_ctx, w_ada, b_ada, w_in, b_merge, conv_lru_w, conv_lru_b, lru_wa, lru_ba, lru_wx, lru_bx, lru_lambda, conv_ssd_w, conv_ssd_b, ssd_dt_bias, ssd_a_log, ssd_d, ssd_norm_w, w_br_lru, w_br_ssd, w_out, ln1_g, ln1_b, router_w, router_b, moe_w1, moe_b1, moe_w2, moe_b2, ln2_g, ln2_b):
    depth = w_ada.shape[0]
    assert depth == 1, "single-layer stack: the context tokens only supply scan states"
    bsz, t, d = x.shape
    t_ctx = ctx.shape[1]
    alpha = (2.0 * depth) ** 0.25
    grid_rows = t // GRID_W
    d_rnn = w_br_lru.shape[1]
    inner = w_br_ssd.shape[1]
    heads = inner // SSD_HEAD_DIM
    gn = SSD_GROUPS * SSD_STATE
    col_gr = d_rnn
    col_z = col_gr + d_rnn
    col_xbc = col_z + inner
    col_dt = col_xbc + inner + 2 * gn
    col_gm = col_dt + 2 * heads
    assert 2 * heads <= LANES and t_ctx % SSD_CHUNK == 0 and t % SSD_CHUNK == 0

    pad = (-(bsz + 1)) % 8
    c_all = jnp.concatenate([c, c_ctx[None, :], jnp.zeros((pad, d), F32)], axis=0)
    mod = _ada(c_all, w_ada[0], b_ada[0])
    sh1, sc1, g1, sh2, sc2, g2 = (mod[:bsz, k * d:(k + 1) * d] for k in range(6))
    csh1, csc1 = mod[bsz:bsz + 1, 0:d], mod[bsz:bsz + 1, d:2 * d]

    w_in_b = w_in[0].astype(BF16)

    tt = 32
    xall_tm = jnp.concatenate([jnp.transpose(ctx, (1, 0, 2)), jnp.transpose(x, (1, 0, 2))], axis=0)
    sh_tm = jnp.stack([jnp.broadcast_to(csh1, (bsz, d)), sh1])
    sc_tm = jnp.stack([jnp.broadcast_to(csc1, (bsz, d)), sc1])
    seg = lambda i: jnp.where(i * tt >= t_ctx, 1, 0)
    xr_all = _inproj(xall_tm, sh_tm, sc_tm, w_in_b[:, :col_gr], (tt, bsz), 1024, sel=seg)
    w_grgm = jnp.concatenate([w_in_b[:, col_gr:col_z], w_in_b[:, col_gm:col_gm + d]], axis=1)
    grgm = _inproj(xall_tm, sh_tm, sc_tm, w_grgm, (tt, bsz), w_grgm.shape[1], sel=seg, first=t_ctx // tt,
                   count=t // tt, out_dtype=BF16)
    p_f = _lru_params(lru_wa[0, 0], lru_ba[0, 0], lru_wx[0, 0], lru_bx[0, 0], lru_lambda[0, 0], 256)
    p_b = _lru_params(lru_wa[0, 1], lru_ba[0, 1], lru_wx[0, 1], lru_bx[0, 1], lru_lambda[0, 1], 256)
    u_lru_tm = _lru(xr_all, grgm, p_f, p_b, conv_lru_w[0], conv_lru_b[0], b_merge[0, :d],
                    w_br_lru[0].astype(BF16), t_ctx, tt)

    x_cm = _to_cols(x, grid_rows)
    sh_b, sc_b = sh1[:, None, :], sc1[:, None, :]
    csh_b, csc_b = csh1[None], csc1[None]
    zero = lambda i: 0
    w_xbc = w_in_b[:, col_xbc:col_dt]
    w_dt = jnp.pad(w_in_b[:, col_dt:col_gm], ((0, 0), (0, LANES - 2 * heads)))
    w_zgm = jnp.concatenate([w_in_b[:, col_z:col_xbc], w_in_b[:, col_gm + d:]], axis=1)
    xbc = _inproj(x_cm, sh_b, sc_b, w_xbc, (1, t), 512, conv_w=conv_ssd_w[0], conv_b=conv_ssd_b[0], out_dtype=BF16)
    xbc_c = _inproj(ctx, csh_b, csc_b, w_xbc, (1, t_ctx), 512, sel=zero, conv_w=conv_ssd_w[0],
                    conv_b=conv_ssd_b[0], out_dtype=BF16)
    dt_raw = _inproj(x_cm, sh_b, sc_b, w_dt, (1, 512), LANES)
    dt_raw_c = _inproj(ctx, csh_b, csc_b, w_dt, (1, t_ctx), LANES, sel=zero)
    zgm = _inproj(x_cm, sh_b, sc_b, w_zgm, (1, 512), w_zgm.shape[1], out_dtype=BF16)

    lane_pad = LANES - 2 * heads
    dtb = jnp.pad(ssd_dt_bias[0].reshape(1, 2 * heads), ((0, 0), (0, lane_pad)))
    alog = jnp.pad(ssd_a_log[0].reshape(1, 2 * heads), ((0, 0), (0, lane_pad)))
    dsk = jnp.repeat(ssd_d[0], SSD_HEAD_DIM).reshape(1, inner)
    head_of_lane = jnp.arange(inner) // SSD_HEAD_DIM

    def expand(lane0):
        e = (jnp.arange(LANES)[:, None] == head_of_lane[None, :] + lane0).astype(BF16)
        return jnp.concatenate([e, e], axis=0)

    gw = inner // SSD_GROUPS
    s0 = jnp.zeros((bsz, SSD_GROUPS, SSD_STATE, gw), F32)
    common = (dtb, alog)
    (st_f,) = _ssd(xbc_c, dt_raw_c, *common, expand(0), dsk, s0, reverse=False, with_y=False, add_skip=False, lane0=0)
    (st_b,) = _ssd(xbc_c, dt_raw_c, *common, expand(heads), dsk, s0, reverse=True, with_y=False, add_skip=False,
                   lane0=heads)
    y_f, _ = _ssd(xbc, dt_raw, *common, expand(0), dsk, st_f, reverse=False, with_y=True, add_skip=True, lane0=0)
    y_b, _ = _ssd(xbc, dt_raw, *common, expand(heads), dsk, st_b, reverse=True, with_y=True, add_skip=False,
                  lane0=heads)

    group_of_lane = jnp.arange(inner) // gw
    gsum = (group_of_lane[:, None] == jnp.arange(LANES)[None, :]).astype(BF16)
    gexp = jnp.concatenate([gsum.T, gsum.T], axis=0)
    u_ssd_cm = _ssdbr(y_f, y_b, zgm, ssd_norm_w[0], b_merge[0, d:], gsum, gexp, w_br_ssd[0].astype(BF16), 256)

    u_lru = jnp.transpose(u_lru_tm, (1, 0, 2))
    u_ssd = _from_cols(u_ssd_cm, grid_rows)
    n_exp = router_w.shape[2]
    rw = jnp.pad(router_w[0], ((0, 0), (0, LANES - n_exp)))
    rw_hi = rw.astype(BF16)
    rw_lo = (rw - rw_hi.astype(F32)).astype(BF16)
    rw = jnp.concatenate([rw_hi, rw_hi, rw_lo], axis=0)
    rb = jnp.pad(router_b[0].reshape(1, n_exp), ((0, 0), (0, LANES - n_exp)), constant_values=NEG_BIG)
    x1, hp, gates, sel = _out(u_lru, u_ssd, x, g1[:, None, :], sh2[:, None, :], sc2[:, None, :],
                              w_out[0].astype(BF16), ln1_g[0], ln1_b[0], rw, rb, 512, alpha)

    return _moe(hp, gates, sel, x1, g2[:, None, :], moe_w1[0], moe_b1[0], moe_w2[0], moe_b2[0],
                ln2_g[0], ln2_b[0], alpha, 512, 512)
```

```python
import functools

import jax
import jax.numpy as jnp
from jax import lax
from jax.experimental import pallas as pl
from jax.experimental.pallas import tpu as pltpu

F32 = jnp.float32
BF16 = jnp.bfloat16
HIGHEST = lax.Precision.HIGHEST

GRID_W = 64
LRU_BLOCK_W = 64
LRU_C = 8.0
CONV_W = 4
SSD_HEAD_DIM = 64
SSD_GROUPS = 8
SSD_STATE = 128
SSD_CHUNK = 128
MOE_TOP_K = 4
SWIGLU_LIMIT = 7.0
SWIGLU_ALPHA = 1.702
LN_EPS = 1e-5
RMS_EPS = 1e-5
LANES = 128
NEG_BIG = -1e30
CONV_SUBTILE = 256
VMEM_LIMIT = 56 * 1024 * 1024


def _cparams(*sem):
    return pltpu.CompilerParams(dimension_semantics=sem, vmem_limit_bytes=VMEM_LIMIT)


def _ln_rows(x):
    mu = jnp.mean(x, axis=-1, keepdims=True)
    xc = x - mu
    var = jnp.mean(xc * xc, axis=-1, keepdims=True)
    return xc * lax.rsqrt(var + LN_EPS)


def _sigmoid(x):
    return 0.5 * (jnp.tanh(0.5 * x) + 1.0)


def _silu(x):
    h = 0.5 * x
    return h + h * jnp.tanh(h)


def _softplus(x):
    return jnp.maximum(x, 0.0) + jnp.log(1.0 + jnp.exp(-jnp.abs(x)))


def _ada_kernel(c_ref, w_ref, b_ref, o_ref):
    c = c_ref[...]
    o_ref[...] = jnp.dot(_silu(c), w_ref[...], precision=HIGHEST, preferred_element_type=F32) + b_ref[...]


def _ada(c_all, w, b):
    m, d = c_all.shape
    n = w.shape[1]
    tn = 1024
    return pl.pallas_call(
        _ada_kernel,
        out_shape=jax.ShapeDtypeStruct((m, n), F32),
        grid=(n // tn,),
        in_specs=[pl.BlockSpec((m, d), lambda j: (0, 0)),
                  pl.BlockSpec((d, tn), lambda j: (0, j)),
                  pl.BlockSpec((1, tn), lambda j: (0, j))],
        out_specs=pl.BlockSpec((m, tn), lambda j: (0, j)),
        compiler_params=_cparams("arbitrary"),
        name="ada",
    )(c_all, w, b.reshape(1, n))


def _inproj_kernel(x_ref, sh_ref, sc_ref, w_ref, *rest, conv, tail, head_blocks):
    o2_ref = None
    if head_blocks:
        xh_ref, rest = rest[0], rest[1:]
    if conv:
        cw_ref, cb_ref, o_ref, h_ref = rest
    elif tail:
        o_ref, o2_ref, h_ref = rest
    else:
        o_ref, h_ref = rest
    rows = h_ref.shape[0]

    def modulate(src_ref):
        h = _ln_rows(src_ref[...]) * (1.0 + sc_ref[...]) + sh_ref[...]
        h_ref[...] = h.reshape(h_ref.shape).astype(BF16)

    first_col = pl.program_id(1) == 0
    if head_blocks:
        in_head = pl.program_id(0) < head_blocks
        pl.when(first_col & in_head)(lambda: modulate(xh_ref))
        pl.when(first_col & jnp.logical_not(in_head))(lambda: modulate(x_ref))
    else:
        pl.when(first_col)(lambda: modulate(x_ref))

    if not conv:
        n1 = o_ref.shape[-1]
        acc = jnp.dot(h_ref[...], w_ref[:, 0:n1], preferred_element_type=F32)
        o_ref[...] = acc.reshape(o_ref.shape).astype(o_ref.dtype)
        if tail:
            acc2 = jnp.dot(h_ref[...], w_ref[:, n1:], preferred_element_type=F32)
            o2_ref[...] = acc2.reshape(o2_ref.shape).astype(o2_ref.dtype)
        return

    def taps(a, cw, cb, n, head, tail):
        r = lax.broadcasted_iota(jnp.int32, a.shape, 0)
        t0, t1, t3 = pltpu.roll(a, 2, 0), pltpu.roll(a, 1, 0), pltpu.roll(a, n - 1, 0)
        if head:
            t0, t1 = jnp.where(r >= 2, t0, 0.0), jnp.where(r >= 1, t1, 0.0)
        if tail:
            t3 = jnp.where(r < n - 1, t3, 0.0)
        return _silu(a * cw[2:3, :] + cb + t0 * cw[0:1, :] + t1 * cw[1:2, :] + t3 * cw[3:4, :])

    edge = 32
    half = edge // 2
    for lo in range(0, w_ref.shape[1], CONV_SUBTILE):
        cols = slice(lo, lo + CONV_SUBTILE)
        acc = jnp.dot(h_ref[...], w_ref[:, cols], preferred_element_type=F32)
        cw, cb = cw_ref[:, cols], cb_ref[:, cols]
        o_ref[0, :, cols] = taps(acc, cw, cb, rows, False, False).astype(o_ref.dtype)
        o_ref[0, 0:half, cols] = taps(acc[0:edge], cw, cb, edge, True, False)[0:half].astype(o_ref.dtype)
        o_ref[0, rows - half:rows, cols] = taps(acc[rows - edge:rows], cw, cb, edge, False, True)[half:].astype(
            o_ref.dtype)


def _inproj(x3, sh3, sc3, w, tile, tn, sel=None, conv_w=None, conv_b=None, out_dtype=F32, tail=None, x3_head=None):
    gdim, rdim, d = x3.shape
    g, r = tile
    n = w.shape[1]
    rows = g * r
    nblk_r = rdim // r
    nb0 = 0 if x3_head is None else x3_head.shape[0] // g
    nblk_g = gdim // g + nb0
    if sel is None:
        sel = lambda i: i
    conv = conv_w is not None
    mg, mr = sh3.shape[1], sh3.shape[2]
    in_specs = [pl.BlockSpec((g, r, d), lambda i, j: (jnp.maximum(i // nblk_r - nb0, 0), i % nblk_r, 0)),
                pl.BlockSpec((1, mg, mr), lambda i, j: (sel(i // nblk_r), 0, 0)),
                pl.BlockSpec((1, mg, mr), lambda i, j: (sel(i // nblk_r), 0, 0)),
                pl.BlockSpec((d, tn), lambda i, j: (0, j))]
    args = [x3, sh3, sc3, w]
    if nb0:
        assert nblk_r == 1 and x3_head.shape[1:] == x3.shape[1:]
        in_specs.append(pl.BlockSpec((g, r, d), lambda i, j: (jnp.minimum(i, nb0 - 1), 0, 0)))
        args.append(x3_head)
    if conv:
        assert r == rdim and g == 1
        in_specs += [pl.BlockSpec((CONV_W, tn), lambda i, j: (0, j)),
                     pl.BlockSpec((1, tn), lambda i, j: (0, j))]
        args += [conv_w, conv_b.reshape(1, n)]
    out_g = nblk_g * g
    out_idx = lambda i, j: (i // nblk_r, i % nblk_r, j)
    out_shape = jax.ShapeDtypeStruct((out_g, rdim, n), out_dtype)
    out_specs = pl.BlockSpec((g, r, tn), out_idx)
    if tail is not None:
        n2, dtype2 = tail
        assert tn == n and not conv
        out_shape = [jax.ShapeDtypeStruct((out_g, rdim, n - n2), out_dtype),
                     jax.ShapeDtypeStruct((out_g, rdim, n2), dtype2)]
        out_specs = [pl.BlockSpec((g, r, n - n2), out_idx), pl.BlockSpec((g, r, n2), out_idx)]
    return pl.pallas_call(
        functools.partial(_inproj_kernel, conv=conv, tail=tail is not None, head_blocks=nb0),
        out_shape=out_shape,
        grid=(nblk_g * nblk_r, n // tn),
        in_specs=in_specs,
        out_specs=out_specs,
        scratch_shapes=[pltpu.VMEM((rows, d), BF16)],
        compiler_params=_cparams("arbitrary", "arbitrary"),
        name="inproj_conv" if conv else "inproj",
    )(*args)


def _lru_gates(win_ref, w_ref, ba_ref, bx_ref, lam_ref, cw_ref, cb_ref, a_ref, b_ref, tt):
    bsz, c = win_ref.shape[1], win_ref.shape[2]
    cw = cw_ref[...]
    u = cb_ref[...].reshape(1, 1, c) + sum(win_ref[pl.ds(j, tt)] * cw[j:j + 1, :].reshape(1, 1, c)
                                           for j in range(CONV_W))
    u2 = u.reshape(tt * bsz, c)
    ub = u2.astype(BF16)
    sp = _softplus(-lam_ref[...])
    pack = w_ref.shape[1]
    for j in range(c // pack):
        lo = j * pack
        pre = jnp.dot(ub[:, lo:lo + pack], w_ref[j], preferred_element_type=F32)
        r = _sigmoid(pre[:, :pack] + ba_ref[:, lo:lo + pack])
        i = _sigmoid(pre[:, pack:] + bx_ref[:, lo:lo + pack])
        log_a = (-LRU_C) * r * sp[:, lo:lo + pack]
        a = jnp.exp(log_a)
        bt = jnp.sqrt(1.0 - jnp.exp(2.0 * log_a)) * (i * u2[:, lo:lo + pack])
        a_ref[:, :, lo:lo + pack] = a.reshape(tt, bsz, pack)
        b_ref[:, :, lo:lo + pack] = bt.reshape(tt, bsz, pack)


def _lru_fill_window(win_ref, x_ref, prev_ref, next_ref, at_start, at_end, tt):
    zero2 = jnp.zeros(prev_ref.shape, F32)
    win_ref[pl.ds(0, 2)] = jnp.where(at_start, zero2, prev_ref[...])
    win_ref[pl.ds(2, tt)] = x_ref[...]
    win_ref[pl.ds(tt + 2, 1)] = jnp.where(at_end, jnp.zeros(next_ref.shape, F32), next_ref[...])


def _lru_bwd_kernel(x_ref, prev_ref, next_ref, w_ref, ba_ref, bx_ref, lam_ref, cw_ref, cb_ref,
                    hb_ref, win_ref, a_ref, b_ref, h_ref, *, tt, ncb, nlb):
    i = pl.program_id(0)
    blk = jnp.where(i < ncb, ncb - 1 - i, ncb + nlb - 1 - (i - ncb))
    at_start = (blk == 0) | (blk == ncb)
    at_end = (blk == ncb - 1) | (blk == ncb + nlb - 1)

    @pl.when(i == 0)
    def _():
        h_ref[...] = jnp.zeros(h_ref.shape, F32)

    _lru_fill_window(win_ref, x_ref, prev_ref, next_ref, at_start, at_end, tt)
    _lru_gates(win_ref, w_ref, ba_ref, bx_ref, lam_ref, cw_ref, cb_ref, a_ref, b_ref, tt)

    def step(k, h):
        t = tt - 1 - k
        h = a_ref[t] * h + b_ref[t]
        a_ref[t] = h
        return h

    h_ref[...] = lax.fori_loop(0, tt, step, h_ref[...])

    @pl.when(i >= ncb)
    def _():
        hb_ref[...] = a_ref[...]


def _lru_fwd_kernel(x_ref, prev_ref, next_ref, w_ref, ba_ref, bx_ref, lam_ref, cw_ref, cb_ref,
                    hb_ref, gr_ref, gm_ref, bm_ref, wbr_ref,
                    o_ref, win_ref, a_ref, b_ref, h_ref, *, tt, ncb, nlb):
    i = pl.program_id(0)
    at_start = (i == 0) | (i == ncb)
    at_end = (i == ncb - 1) | (i == ncb + nlb - 1)

    @pl.when(i == 0)
    def _():
        h_ref[...] = jnp.zeros(h_ref.shape, F32)

    _lru_fill_window(win_ref, x_ref, prev_ref, next_ref, at_start, at_end, tt)
    _lru_gates(win_ref, w_ref, ba_ref, bx_ref, lam_ref, cw_ref, cb_ref, a_ref, b_ref, tt)

    def step(t, h):
        h = a_ref[t] * h + b_ref[t]
        a_ref[t] = h
        return h

    h_ref[...] = lax.fori_loop(0, tt, step, h_ref[...])

    @pl.when(i >= ncb)
    def _():
        bsz, c = h_ref.shape
        rows = tt * bsz
        a_lat = (a_ref[...] + hb_ref[...]) * jax.nn.gelu(gr_ref[...].astype(F32), approximate=True)
        proj = jnp.dot(a_lat.reshape(rows, c).astype(BF16), wbr_ref[...], preferred_element_type=F32)
        gate = _sigmoid(gm_ref[...].astype(F32).reshape(rows, -1) + bm_ref[...])
        o_ref[...] = (gate * proj).reshape(o_ref.shape)


def _lru_specs(tt, bsz, c, blk_of, ttot):
    half = tt // 2
    return [pl.BlockSpec((tt, bsz, c), lambda i: (blk_of(i), 0, 0)),
            pl.BlockSpec((2, bsz, c), lambda i: (jnp.maximum(blk_of(i) * half - 1, 0), 0, 0)),
            pl.BlockSpec((1, bsz, c), lambda i: (jnp.minimum(blk_of(i) * tt + tt, ttot - 1), 0, 0))]


def _lru(xr_all, grgm, p_f, p_b, conv_w, conv_b, b_merge_lru, w_br, t_ctx, tt):
    ttot, bsz, c = xr_all.shape
    ncb, nlb = t_ctx // tt, (ttot - t_ctx) // tt
    nblk = ncb + nlb
    d = w_br.shape[1]
    const2 = lambda i: (0, 0)
    const3 = lambda i: (0, 0, 0)
    par_specs = [pl.BlockSpec(p_f[0].shape, const3), pl.BlockSpec((1, c), const2), pl.BlockSpec((1, c), const2),
                 pl.BlockSpec((1, c), const2), pl.BlockSpec((CONV_W, c), const2), pl.BlockSpec((1, c), const2)]
    scratch = [pltpu.VMEM((tt + 3, bsz, c), F32), pltpu.VMEM((tt, bsz, c), F32),
               pltpu.VMEM((tt, bsz, c), F32), pltpu.VMEM((bsz, c), F32)]

    bwd_blk = lambda i: jnp.where(i < ncb, ncb - 1 - i, ncb + nlb - 1 - (i - ncb))
    hb = pl.pallas_call(
        functools.partial(_lru_bwd_kernel, tt=tt, ncb=ncb, nlb=nlb),
        out_shape=jax.ShapeDtypeStruct((nlb * tt, bsz, c), F32),
        grid=(nblk,),
        in_specs=_lru_specs(tt, bsz, c, bwd_blk, ttot) + par_specs,
        out_specs=pl.BlockSpec((tt, bsz, c), lambda i: (jnp.where(i < ncb, nlb - 1, nblk - 1 - i), 0, 0)),
        scratch_shapes=scratch,
        compiler_params=_cparams("arbitrary"),
        name="lru_bwd",
    )(xr_all, xr_all, xr_all, *p_b, conv_w, conv_b.reshape(1, c))

    assert c == d
    lat = lambda i: (jnp.maximum(i - ncb, 0), 0, 0)
    lat1 = lambda i: (jnp.maximum(i - ncb, 0), 0, 1)
    return pl.pallas_call(
        functools.partial(_lru_fwd_kernel, tt=tt, ncb=ncb, nlb=nlb),
        out_shape=jax.ShapeDtypeStruct((nlb * tt, bsz, d), F32),
        grid=(nblk,),
        in_specs=_lru_specs(tt, bsz, c, lambda i: i, ttot) + par_specs + [
            pl.BlockSpec((tt, bsz, c), lat), pl.BlockSpec((tt, bsz, c), lat), pl.BlockSpec((tt, bsz, d), lat1),
            pl.BlockSpec((1, d), const2), pl.BlockSpec((c, d), const2)],
        out_specs=pl.BlockSpec((tt, bsz, d), lat),
        scratch_shapes=scratch,
        compiler_params=_cparams("arbitrary"),
        name="lru_fwd",
    )(xr_all, xr_all, xr_all, *p_f, conv_w, conv_b.reshape(1, c), hb, grgm, grgm, b_merge_lru.reshape(1, d), w_br)


def _lru_params(wa, ba, wx, bx, lam, pack):
    nb, bw, _ = wa.shape
    per = pack // bw
    c = nb * bw

    def bd(w):
        w4 = w.reshape(nb // per, per, bw, bw)
        eye = jnp.eye(per, dtype=w.dtype)
        return jnp.einsum('gpde,pq->gpdqe', w4, eye).reshape(nb // per, pack, pack)

    w = jnp.concatenate([bd(wa), bd(wx)], axis=-1).astype(BF16)
    return w, ba.reshape(1, c), bx.reshape(1, c), lam.reshape(1, c)


def _hi_lo(v):
    hi = v.astype(BF16)
    lo = (v - hi.astype(F32)).astype(BF16)
    return jnp.concatenate([hi, lo], axis=-1)


def _ssd_kernel(xbc_ref, dt_ref, dtb_ref, alog_ref, e_ref, dsk_ref, h0_ref, *outs,
                reverse, with_y, add_skip, lane0, inner):
    if with_y:
        y_ref, hfin_ref, st_ref = outs
    else:
        hfin_ref, st_ref = outs
    i = pl.program_id(1)
    q = SSD_CHUNK
    n = SSD_STATE
    gw = inner // SSD_GROUPS
    hpg = gw // SSD_HEAD_DIM

    @pl.when(i == 0)
    def _():
        st_ref[...] = h0_ref[0]

    dt = _softplus(dt_ref[0] + dtb_ref[...])
    da = dt * (-jnp.exp(alog_ref[...]))
    ri = lax.broadcasted_iota(jnp.int32, (q, q), 0)
    ci = lax.broadcasted_iota(jnp.int32, (q, q), 1)
    tri = (ri <= ci) if reverse else (ri >= ci)
    acum = jnp.dot(tri.astype(F32), da, precision=HIGHEST, preferred_element_type=F32)
    a_tot = jnp.sum(da, axis=0, keepdims=True)
    w_state = jnp.exp(a_tot - acum) * dt
    e2 = e_ref[...]
    ws_x = jnp.dot(w_state.astype(BF16), e2[0:LANES, :], preferred_element_type=F32)
    dec_x = jnp.dot(_hi_lo(jnp.broadcast_to(jnp.exp(a_tot), (8, LANES))), e2,
                    preferred_element_type=F32)[0:1, :]
    if with_y:
        eac_x = jnp.dot(_hi_lo(jnp.exp(acum)), e2, preferred_element_type=F32)
        acum_t = acum.T
        dt_t = dt.T
        rb = lax.broadcasted_iota(jnp.int32, (hpg * q, gw), 0) // q
        lb = lax.broadcasted_iota(jnp.int32, (hpg * q, gw), 1) // SSD_HEAD_DIM
        bd_mask = rb == lb

    for g in range(SSD_GROUPS):
        lo = g * gw
        xg = xbc_ref[0, :, lo:lo + gw]
        bg = xbc_ref[0, :, inner + g * n:inner + (g + 1) * n]
        cg = xbc_ref[0, :, inner + (SSD_GROUPS + g) * n:inner + (SSD_GROUPS + g + 1) * n]
        xgf = xg.astype(F32)
        xw = (xgf * ws_x[:, lo:lo + gw]).astype(BF16)
        st = st_ref[g]
        upd = lax.dot_general(bg, xw, (((0,), (0,)), ((), ())), preferred_element_type=F32)
        st_ref[g] = dec_x[:, lo:lo + gw] * st + upd
        if with_y:
            y_off = jnp.dot(cg, st.astype(BF16), preferred_element_type=F32) * eac_x[:, lo:lo + gw]
            cb = lax.dot_general(cg, bg, (((1,), (1,)), ((), ())), preferred_element_type=F32)
            ls = []
            for r in range(hpg):
                lane = lane0 + g * hpg + r
                seg = acum[:, lane:lane + 1] - acum_t[lane:lane + 1, :]
                l_h = cb * jnp.exp(jnp.where(tri, seg, NEG_BIG)) * dt_t[lane:lane + 1, :]
                ls.append(l_h.astype(BF16))
            lcat = jnp.concatenate(ls, axis=1)
            xbd = jnp.where(bd_mask, jnp.concatenate([xg] * hpg, axis=0), jnp.zeros((), BF16))
            y = y_off + jnp.dot(lcat, xbd, preferred_element_type=F32)
            if add_skip:
                y = y + dsk_ref[:, lo:lo + gw] * xgf
            y_ref[0, :, lo:lo + gw] = y

    @pl.when(i == pl.num_programs(1) - 1)
    def _():
        hfin_ref[0] = st_ref[...]


def _ssd(xbc, dt_raw, dtb, alog, e2, dsk, h0, *, reverse, with_y, add_skip, lane0):
    bsz, s, width = xbc.shape
    inner = e2.shape[1]
    nc = s // SSD_CHUNK
    gw = inner // SSD_GROUPS
    cidx = (lambda b, i: (b, nc - 1 - i, 0)) if reverse else (lambda b, i: (b, i, 0))
    const2 = lambda b, i: (0, 0)
    st_spec = pl.BlockSpec((1, SSD_GROUPS, SSD_STATE, gw), lambda b, i: (b, 0, 0, 0))
    st_shape = jax.ShapeDtypeStruct((bsz, SSD_GROUPS, SSD_STATE, gw), F32)
    out_shape, out_specs = [st_shape], [st_spec]
    if with_y:
        out_shape = [jax.ShapeDtypeStruct((bsz, s, inner), F32)] + out_shape
        out_specs = [pl.BlockSpec((1, SSD_CHUNK, inner), cidx)] + out_specs
    return pl.pallas_call(
        functools.partial(_ssd_kernel, reverse=reverse, with_y=with_y, add_skip=add_skip, lane0=lane0, inner=inner),
        out_shape=out_shape,
        grid=(bsz, nc),
        in_specs=[pl.BlockSpec((1, SSD_CHUNK, width), cidx),
                  pl.BlockSpec((1, SSD_CHUNK, LANES), cidx),
                  pl.BlockSpec((1, LANES), const2), pl.BlockSpec((1, LANES), const2),
                  pl.BlockSpec(e2.shape, const2), pl.BlockSpec((1, inner), const2), st_spec],
        out_specs=out_specs,
        scratch_shapes=[pltpu.VMEM((SSD_GROUPS, SSD_STATE, gw), F32)],
        compiler_params=_cparams("arbitrary", "arbitrary"),
        name="ssd_y" if with_y else "ssd_state",
    )(xbc, dt_raw, dtb, alog, e2, dsk, h0)


def _ssdbr_kernel(yf_ref, yb_ref, z_ref, gm_ref, nw_ref, bm_ref, gsum_ref, gexp_ref, w_ref, o_ref, *, group_w):
    y = (yf_ref[0] + yb_ref[0]) * _silu(z_ref[0].astype(F32))
    ms = jnp.dot((y * y).astype(BF16), gsum_ref[...], preferred_element_type=F32) * (1.0 / group_w)
    rs = lax.rsqrt(ms + RMS_EPS)
    rs_x = jnp.dot(_hi_lo(rs), gexp_ref[...], preferred_element_type=F32)
    yn = (y * rs_x * nw_ref[...]).astype(BF16)
    proj = jnp.dot(yn, w_ref[...], preferred_element_type=F32)
    o_ref[0] = _sigmoid(gm_ref[0].astype(F32) + bm_ref[...]) * proj


def _ssdbr(y_f, y_b, zgm, norm_w, b_merge_ssd, gsum, gexp, w_br, rows):
    bsz, s, inner = y_f.shape
    d = w_br.shape[1]
    assert inner % d == 0
    tok = lambda b, i: (b, i, 0)
    const2 = lambda b, i: (0, 0)
    return pl.pallas_call(
        functools.partial(_ssdbr_kernel, group_w=inner // SSD_GROUPS),
        out_shape=jax.ShapeDtypeStruct((bsz, s, d), F32),
        grid=(bsz, s // rows),
        in_specs=[pl.BlockSpec((1, rows, inner), tok), pl.BlockSpec((1, rows, inner), tok),
                  pl.BlockSpec((1, rows, inner), tok),
                  pl.BlockSpec((1, rows, d), lambda b, i: (b, i, inner // d)),
                  pl.BlockSpec((1, inner), const2), pl.BlockSpec((1, d), const2),
                  pl.BlockSpec(gsum.shape, const2), pl.BlockSpec(gexp.shape, const2),
                  pl.BlockSpec((inner, d), const2)],
        out_specs=pl.BlockSpec((1, rows, d), tok),
        compiler_params=_cparams("arbitrary", "arbitrary"),
        name="ssdbr",
    )(y_f, y_b, zgm, zgm, norm_w.reshape(1, inner), b_merge_ssd.reshape(1, d), gsum, gexp, w_br)


def _pack_bf16_pair(lo, hi):
    lo_bits = lax.bitcast_convert_type(lo.astype(BF16).astype(F32), jnp.uint32)
    hi_bits = lax.bitcast_convert_type(hi.astype(BF16).astype(F32), jnp.uint32)
    return (lo_bits >> 16) | hi_bits


def _unpack_bf16_pair(p):
    lo = lax.bitcast_convert_type(p << 16, F32).astype(BF16)
    hi = lax.bitcast_convert_type(p & jnp.uint32(0xFFFF0000), F32).astype(BF16)
    return jnp.concatenate([lo, hi], axis=1)


def _out_kernel(ul_ref, us_ref, x_ref, g1_ref, sh2_ref, sc2_ref, wout_ref, lg_ref, lb_ref, rw_ref, rb_ref,
                x1_ref, hp_ref, gates_ref, sel_ref, *, alpha):
    u = (ul_ref[0] + us_ref[0]).astype(BF16)
    mix = jnp.dot(u, wout_ref[...], preferred_element_type=F32)
    x1 = _ln_rows(alpha * x_ref[0] + g1_ref[0] * mix) * lg_ref[...] + lb_ref[...]
    x1_ref[0] = x1
    h2 = _ln_rows(x1) * (1.0 + sc2_ref[0]) + sh2_ref[0]
    half = h2.shape[1] // 2
    hp_ref[0] = _pack_bf16_pair(h2[:, :half], h2[:, half:])
    h_hi = h2.astype(BF16)
    h_lo = (h2 - h_hi.astype(F32)).astype(BF16)
    logits = jnp.dot(jnp.concatenate([h_hi, h_lo, h_hi], axis=1), rw_ref[...],
                     preferred_element_type=F32) + rb_ref[...]
    lane = lax.broadcasted_iota(jnp.int32, logits.shape, 1)
    work = logits
    sel = jnp.zeros(logits.shape, jnp.bool_)
    top = None
    for k in range(MOE_TOP_K):
        m = jnp.max(work, axis=1, keepdims=True)
        if k == 0:
            top = m
        idx = jnp.min(jnp.where(work == m, lane, LANES), axis=1, keepdims=True)
        pick = lane == idx
        sel = sel | pick
        work = jnp.where(pick, 3.0 * NEG_BIG, work)
    e = jnp.where(sel, jnp.exp(logits - top), 0.0)
    gates_ref[0] = e / jnp.sum(e, axis=1, keepdims=True)
    sel_ref[0] = sel.astype(F32)


def _out(u_l, u_s, x, g1, sh2, sc2, w_out, ln_g, ln_b, rw, rb, rows, alpha):
    bsz, t, d = x.shape
    tok = lambda b, i: (b, i, 0)
    per_b = lambda b, i: (b, 0, 0)
    const2 = lambda b, i: (0, 0)
    return pl.pallas_call(
        functools.partial(_out_kernel, alpha=alpha),
        out_shape=[jax.ShapeDtypeStruct((bsz, t, d), F32), jax.ShapeDtypeStruct((bsz, t, d // 2), jnp.uint32),
                   jax.ShapeDtypeStruct((bsz, t, LANES), F32), jax.ShapeDtypeStruct((bsz, t, LANES), F32)],
        grid=(bsz, t // rows),
        in_specs=[pl.BlockSpec((1, rows, d), tok), pl.BlockSpec((1, rows, d), tok), pl.BlockSpec((1, rows, d), tok),
                  pl.BlockSpec((1, 1, d), per_b), pl.BlockSpec((1, 1, d), per_b), pl.BlockSpec((1, 1, d), per_b),
                  pl.BlockSpec((d, d), const2), pl.BlockSpec((1, d), const2), pl.BlockSpec((1, d), const2),
                  pl.BlockSpec((3 * d, LANES), const2), pl.BlockSpec((1, LANES), const2)],
        out_specs=[pl.BlockSpec((1, rows, d), tok), pl.BlockSpec((1, rows, d // 2), tok),
                   pl.BlockSpec((1, rows, LANES), tok), pl.BlockSpec((1, rows, LANES), tok)],
        compiler_params=_cparams("arbitrary", "arbitrary"),
        name="out",
    )(u_l, u_s, x, g1, sh2, sc2, w_out, ln_g.reshape(1, d), ln_b.reshape(1, d), rw, rb)


def _rank_kernel(sel_ref, rank_ref, cnt_ref, carry_ref):
    @pl.when(pl.program_id(0) == 0)
    def _():
        carry_ref[...] = jnp.zeros(carry_ref.shape, F32)

    s = sel_ref[...]
    tt = s.shape[0]
    ri = lax.broadcasted_iota(jnp.int32, (tt, tt), 0)
    ci = lax.broadcasted_iota(jnp.int32, (tt, tt), 1)
    earlier = (ri > ci).astype(BF16)
    rank_ref[...] = jnp.dot(earlier, s.astype(BF16), preferred_element_type=F32) + carry_ref[0:1, :]
    carry_ref[...] = carry_ref[...] + jnp.sum(s, axis=0, keepdims=True)
    cnt_ref[...] = carry_ref[...]


def _rank(sel, tt):
    n = sel.shape[0]
    return pl.pallas_call(
        _rank_kernel,
        out_shape=[jax.ShapeDtypeStruct((n, LANES), F32), jax.ShapeDtypeStruct((8, LANES), F32)],
        grid=(n // tt,),
        in_specs=[pl.BlockSpec((tt, LANES), lambda i: (i, 0))],
        out_specs=[pl.BlockSpec((tt, LANES), lambda i: (i, 0)), pl.BlockSpec((8, LANES), lambda i: (0, 0))],
        scratch_shapes=[pltpu.VMEM((8, LANES), F32)],
        compiler_params=_cparams("arbitrary"),
        name="moe_rank",
    )(sel)


def _pos_kernel(sel_ref, rank_ref, gates_ref, off_ref, pos_ref, w_ref):
    avail = sel_ref[...] > 0.5
    posf = off_ref[...] + rank_ref[...]
    gates = gates_ref[...]
    lane = lax.broadcasted_iota(jnp.int32, posf.shape, 1)
    cols_p = jnp.zeros(posf.shape, F32)
    cols_w = jnp.zeros(posf.shape, F32)
    for k in range(MOE_TOP_K):
        m = jnp.min(jnp.where(avail, lane, LANES), axis=1, keepdims=True)
        pick = lane == m
        cols_p = jnp.where(lane == k, jnp.sum(jnp.where(pick, posf, 0.0), axis=1, keepdims=True), cols_p)
        cols_w = jnp.where(lane == k, jnp.sum(jnp.where(pick, gates, 0.0), axis=1, keepdims=True), cols_w)
        avail = avail & jnp.logical_not(pick)
    w_ref[...] = cols_w
    pos_ref[...] = cols_p.astype(jnp.int32)


def _pos(sel, rank, gates, off, tt):
    n = sel.shape[0]
    tok = lambda i: (i, 0)
    return pl.pallas_call(
        _pos_kernel,
        out_shape=[jax.ShapeDtypeStruct((n, LANES), jnp.int32), jax.ShapeDtypeStruct((n, LANES), F32)],
        grid=(n // tt,),
        in_specs=[pl.BlockSpec((tt, LANES), tok), pl.BlockSpec((tt, LANES), tok), pl.BlockSpec((tt, LANES), tok),
                  pl.BlockSpec((1, LANES), lambda i: (0, 0))],
        out_specs=[pl.BlockSpec((tt, LANES), tok), pl.BlockSpec((tt, LANES), tok)],
        compiler_params=_cparams("arbitrary"),
        name="moe_pos",
    )(sel, rank, gates, off)


ROW_GROUP = 8
ROW_TILE = 8


def _dispatch_kernel(pos_ref, hp_ref, xs_init_hbm, xs_hbm, sem, *, tt):
    del xs_init_hbm

    def body(i, carry):
        row0 = pl.multiple_of(i * ROW_GROUP, ROW_GROUP)
        p0 = i * (ROW_GROUP * MOE_TOP_K)
        for u in range(ROW_GROUP):
            src = hp_ref.at[pl.ds(row0 + u, 1)]
            for k in range(MOE_TOP_K):
                dst = xs_hbm.at[pl.ds(pos_ref[p0 + u * MOE_TOP_K + k], 1)]
                pltpu.make_async_copy(src, dst, sem).start(priority=k % 2)
        return carry

    lax.fori_loop(0, tt // ROW_GROUP, body, 0)
    for k in range(MOE_TOP_K):
        pltpu.make_async_copy(hp_ref, xs_hbm.at[pl.ds(0, tt)], sem).wait()


def _dispatch(pos, hp, xs_init, tt):
    n, half = hp.shape
    any_spec = pl.BlockSpec(memory_space=pl.ANY)
    return pl.pallas_call(
        functools.partial(_dispatch_kernel, tt=tt),
        out_shape=jax.ShapeDtypeStruct(xs_init.shape, xs_init.dtype),
        grid=(n // tt,),
        in_specs=[pl.BlockSpec((tt * MOE_TOP_K,), lambda i: (i,), memory_space=pltpu.SMEM),
                  pl.BlockSpec((tt, half), lambda i: (i, 0)), any_spec],
        out_specs=any_spec,
        scratch_shapes=[pltpu.SemaphoreType.DMA(())],
        input_output_aliases={2: 0},
        compiler_params=pltpu.CompilerParams(dimension_semantics=("arbitrary",), has_side_effects=True),
        name="moe_dispatch",
    )(pos, hp, xs_init)


def _expert_kernel(te_ref, nv_ref, xs_ref, w1_ref, b1_ref, w2_ref, b2_ref, ys_ref, w1b_ref, w2b_ref, *, ff):
    t = pl.program_id(0)
    e = te_ref[t]
    prev = te_ref[jnp.maximum(t - 1, 0)]

    @pl.when((t == 0) | (e != prev))
    def _():
        w1b_ref[...] = w1_ref[0].astype(BF16)
        w2b_ref[...] = w2_ref[0].astype(BF16)

    @pl.when(t < nv_ref[0])
    def _():
        xrow = _unpack_bf16_pair(xs_ref[...])
        gu = jnp.dot(xrow, w1b_ref[...], preferred_element_type=F32) + b1_ref[0]
        g = jnp.minimum(gu[:, :ff], SWIGLU_LIMIT)
        u = jnp.clip(gu[:, ff:], -SWIGLU_LIMIT, SWIGLU_LIMIT)
        act = ((u + 1.0) * g * _sigmoid(SWIGLU_ALPHA * g)).astype(BF16)
        y = jnp.dot(act, w2b_ref[...], preferred_element_type=F32) + b2_ref[0]
        for g in range(y.shape[1] // LANES):
            ys_ref[pl.ds(g, y.shape[0], stride=ROW_TILE), :] = y[:, g * LANES:(g + 1) * LANES]

    @pl.when(t >= nv_ref[0])
    def _():
        ys_ref[...] = jnp.zeros(ys_ref.shape, F32)


def _experts(tile_expert, n_valid, xs, w1, b1, w2, b2, tm):
    rows, half = xs.shape
    n_exp, d, ff2 = w1.shape
    ff = ff2 // 2
    per_e = lambda t, te, nv: (te[t], 0, 0)
    assert d == ROW_TILE * LANES
    return pl.pallas_call(
        functools.partial(_expert_kernel, ff=ff),
        out_shape=jax.ShapeDtypeStruct((rows * ROW_TILE, LANES), F32),
        grid_spec=pltpu.PrefetchScalarGridSpec(
            num_scalar_prefetch=2,
            grid=(rows // tm,),
            in_specs=[pl.BlockSpec((tm, half), lambda t, te, nv: (t, 0)),
                      pl.BlockSpec((1, d, ff2), per_e), pl.BlockSpec((1, 1, ff2), per_e),
                      pl.BlockSpec((1, ff, d), per_e), pl.BlockSpec((1, 1, d), per_e)],
            out_specs=pl.BlockSpec((tm * ROW_TILE, LANES), lambda t, te, nv: (t, 0)),
            scratch_shapes=[pltpu.VMEM((d, ff2), BF16), pltpu.VMEM((ff, d), BF16)]),
        compiler_params=_cparams("arbitrary"),
        name="moe_experts",
    )(tile_expert, n_valid, xs, w1, b1.reshape(n_exp, 1, ff2), w2, b2.reshape(n_exp, 1, d))


def _combine_kernel(pos_ref, ys_hbm, ys_flat_hbm, w_ref, x1_ref, g2_ref, lg_ref, lb_ref, o_ref, buf_ref, sem,
                    *, tt, alpha):
    def body(i, carry):
        row0 = pl.multiple_of(i * ROW_GROUP, ROW_GROUP)
        p0 = i * (ROW_GROUP * MOE_TOP_K)
        for u in range(ROW_GROUP):
            for k in range(MOE_TOP_K):
                src = ys_hbm.at[pos_ref[p0 + u * MOE_TOP_K + k]]
                dst = buf_ref.at[pl.ds(pl.multiple_of((k * tt + row0 + u) * ROW_TILE, ROW_TILE), ROW_TILE)]
                pltpu.make_async_copy(src, dst, sem).start(priority=k % 2)
        return carry

    lax.fori_loop(0, tt // ROW_GROUP, body, 0)
    span = tt * ROW_TILE
    for k in range(MOE_TOP_K):
        pltpu.make_async_copy(ys_flat_hbm.at[pl.ds(0, span)], buf_ref.at[pl.ds(k * span, span)], sem).wait()

    chunk = 64

    def rows(c, carry):
        r0 = pl.multiple_of(c * chunk, chunk)
        r = pl.ds(r0, chunk)
        w = w_ref[r, :]

        def slot_rows(k):
            first = (k * tt + r0) * ROW_TILE
            return jnp.concatenate([buf_ref[pl.ds(first + g, chunk, stride=ROW_TILE), :] for g in range(ROW_TILE)],
                                   axis=1)

        acc = sum(w[:, k:k + 1] * slot_rows(k) for k in range(MOE_TOP_K))
        o_ref[0, r, :] = _ln_rows(alpha * x1_ref[0, r, :] + g2_ref[0] * acc) * lg_ref[...] + lb_ref[...]
        return carry

    lax.fori_loop(0, tt // chunk, rows, 0)


def _combine(pos, ys, w, x1, g2, ln_g, ln_b, tt, alpha):
    bsz, t, d = x1.shape
    nt = t // tt
    return pl.pallas_call(
        functools.partial(_combine_kernel, tt=tt, alpha=alpha),
        out_shape=jax.ShapeDtypeStruct((bsz, t, d), F32),
        grid=(bsz, nt),
        in_specs=[pl.BlockSpec((tt * MOE_TOP_K,), lambda b, i: (b * nt + i,), memory_space=pltpu.SMEM),
                  pl.BlockSpec(memory_space=pl.ANY), pl.BlockSpec(memory_space=pl.ANY),
                  pl.BlockSpec((tt, LANES), lambda b, i: (b * nt + i, 0)),
                  pl.BlockSpec((1, tt, d), lambda b, i: (b, i, 0)),
                  pl.BlockSpec((1, 1, d), lambda b, i: (b, 0, 0)),
                  pl.BlockSpec((1, d), lambda b, i: (0, 0)), pl.BlockSpec((1, d), lambda b, i: (0, 0))],
        out_specs=pl.BlockSpec((1, tt, d), lambda b, i: (b, i, 0)),
        scratch_shapes=[pltpu.VMEM((MOE_TOP_K * tt * ROW_TILE, LANES), F32), pltpu.SemaphoreType.DMA(())],
        compiler_params=_cparams("arbitrary", "arbitrary"),
        name="moe_combine",
    )(pos, ys.reshape(-1, ROW_TILE, LANES), ys, w, x1, g2, ln_g.reshape(1, d), ln_b.reshape(1, d))


def _moe(hp, gates, sel, x1, g2, w1, b1, w2, b2, ln_g, ln_b, alpha, tt, tm):
    bsz, t, d = x1.shape
    n = bsz * t
    n_exp = w1.shape[0]
    sel2, gates2 = sel.reshape(n, LANES), gates.reshape(n, LANES)
    rank, cnt = _rank(sel2, tt)
    counts = cnt[0, :n_exp].astype(jnp.int32)
    tiles_per = (counts + tm - 1) // tm
    tile_end = jnp.cumsum(tiles_per)
    off = jnp.pad(((tile_end - tiles_per) * tm).astype(F32).reshape(1, n_exp), ((0, 0), (0, LANES - n_exp)))
    n_tiles = (n * MOE_TOP_K) // tm + n_exp
    n_valid = tile_end[-1:]
    tile_id = jnp.minimum(jnp.arange(n_tiles), n_valid - 1)
    tile_expert = jnp.sum((tile_end[None, :] <= tile_id[:, None]).astype(jnp.int32), axis=1)
    tile_expert = jnp.minimum(tile_expert, n_exp - 1)
    pos_lanes, w = _pos(sel2, rank, gates2, off, tt)
    pos = pos_lanes[:, :MOE_TOP_K].reshape(n * MOE_TOP_K)
    xs = _dispatch(pos, hp.reshape(n, d // 2), jnp.zeros((n_tiles * tm, d // 2), jnp.uint32), tt)
    ys = _experts(tile_expert, n_valid.astype(jnp.int32), xs, w1, b1, w2, b2, tm)
    return _combine(pos, ys, w, x1, g2, ln_g, ln_b, tt, alpha)


def _to_cols(u, grid_rows):
    b, t, c = u.shape
    return u.reshape(b, grid_rows, GRID_W, c).transpose(0, 2, 1, 3).reshape(b, t, c)


def _from_cols(u, grid_rows):
    b, t, c = u.shape
    return u.reshape(b, GRID_W, grid_rows, c).transpose(0, 2, 1, 3).reshape(b, t, c)


def kernel(x, c, ctx, c_ctx, w_ada, b_ada, w_in, b_merge, conv_lru_w, conv_lru_b, lru_wa, lru_ba, lru_wx, lru_bx, lru_lambda, conv_ssd_w, conv_ssd_b, ssd_dt_bias, ssd_a_log, ssd_d, ssd_norm_w, w_br_lru, w_br_ssd, w_out, ln1_g, ln1_b, router_w, router_b, moe_w1, moe_b1, moe_w2, moe_b2, ln2_g, ln2_b):
    depth = w_ada.shape[0]
    assert depth == 1, "single-layer stack: the context tokens only supply scan states"
    bsz, t, d = x.shape
    t_ctx = ctx.shape[1]
    alpha = (2.0 * depth) ** 0.25
    grid_rows = t // GRID_W
    d_rnn = w_br_lru.shape[1]
    inner = w_br_ssd.shape[1]
    heads = inner // SSD_HEAD_DIM
    gn = SSD_GROUPS * SSD_STATE
    col_gr = d_rnn
    col_z = col_gr + d_rnn
    col_xbc = col_z + inner
    col_dt = col_xbc + inner + 2 * gn
    col_gm = col_dt + 2 * heads
    assert 2 * heads <= LANES and t_ctx % SSD_CHUNK == 0 and t % SSD_CHUNK == 0

    pad = (-(bsz + 1)) % 8
    c_all = jnp.concatenate([c, c_ctx[None, :], jnp.zeros((pad, d), F32)], axis=0)
    mod = _ada(c_all, w_ada[0], b_ada[0])
    sh1, sc1, g1, sh2, sc2, g2 = (mod[:bsz, k * d:(k + 1) * d] for k in range(6))
    csh1, csc1 = mod[bsz:bsz + 1, 0:d], mod[bsz:bsz + 1, d:2 * d]

    w_in_b = w_in[0].astype(BF16)

    tt = 32
    zero = lambda i: 0
    x_tm, ctx_tm = jnp.transpose(x, (1, 0, 2)), jnp.transpose(ctx, (1, 0, 2))
    sh_tm = jnp.stack([jnp.broadcast_to(csh1, (bsz, d)), sh1])
    sc_tm = jnp.stack([jnp.broadcast_to(csc1, (bsz, d)), sc1])
    seg = lambda i: jnp.where(i * tt >= t_ctx, 1, 0)
    xr_all = _inproj(x_tm, sh_tm, sc_tm, w_in_b[:, :col_gr], (tt, bsz), 1024, sel=seg, x3_head=ctx_tm)
    w_grgm = jnp.concatenate([w_in_b[:, col_gr:col_z], w_in_b[:, col_gm:col_gm + d]], axis=1)
    grgm = _inproj(x_tm, sh_tm[1:], sc_tm[1:], w_grgm, (tt, bsz), w_grgm.shape[1], sel=zero, out_dtype=BF16)
    p_f = _lru_params(lru_wa[0, 0], lru_ba[0, 0], lru_wx[0, 0], lru_bx[0, 0], lru_lambda[0, 0], 256)
    p_b = _lru_params(lru_wa[0, 1], lru_ba[0, 1], lru_wx[0, 1], lru_bx[0, 1], lru_lambda[0, 1], 256)
    u_lru_tm = _lru(xr_all, grgm, p_f, p_b, conv_lru_w[0], conv_lru_b[0], b_merge[0, :d],
                    w_br_lru[0].astype(BF16), t_ctx, tt)

    x_cm = _to_cols(x, grid_rows)
    sh_b, sc_b = sh1[:, None, :], sc1[:, None, :]
    csh_b, csc_b = csh1[None], csc1[None]
    w_xbc = w_in_b[:, col_xbc:col_dt]
    w_dt = jnp.pad(w_in_b[:, col_dt:col_gm], ((0, 0), (0, LANES - 2 * heads)))
    w_zgm = jnp.concatenate([w_in_b[:, col_z:col_xbc], w_in_b[:, col_gm + d:]], axis=1)
    xbc = _inproj(x_cm, sh_b, sc_b, w_xbc, (1, t), 512, conv_w=conv_ssd_w[0], conv_b=conv_ssd_b[0], out_dtype=BF16)
    xbc_c = _inproj(ctx, csh_b, csc_b, w_xbc, (1, t_ctx), 512, sel=zero, conv_w=conv_ssd_w[0],
                    conv_b=conv_ssd_b[0], out_dtype=BF16)
    dt_raw_c = _inproj(ctx, csh_b, csc_b, w_dt, (1, t_ctx), LANES, sel=zero)
    w_zgm_dt = jnp.concatenate([w_zgm, w_dt], axis=1)
    zgm, dt_raw = _inproj(x_cm, sh_b, sc_b, w_zgm_dt, (1, 512), w_zgm_dt.shape[1], out_dtype=BF16,
                          tail=(LANES, F32))

    lane_pad = LANES - 2 * heads
    dtb = jnp.pad(ssd_dt_bias[0].reshape(1, 2 * heads), ((0, 0), (0, lane_pad)))
    alog = jnp.pad(ssd_a_log[0].reshape(1, 2 * heads), ((0, 0), (0, lane_pad)))
    dsk = jnp.repeat(ssd_d[0], SSD_HEAD_DIM).reshape(1, inner)
    head_of_lane = jnp.arange(inner) // SSD_HEAD_DIM

    def expand(lane0):
        e = (jnp.arange(LANES)[:, None] == head_of_lane[None, :] + lane0).astype(BF16)
        return jnp.concatenate([e, e], axis=0)

    gw = inner // SSD_GROUPS
    s0 = jnp.zeros((bsz, SSD_GROUPS, SSD_STATE, gw), F32)
    common = (dtb, alog)
    (st_f,) = _ssd(xbc_c, dt_raw_c, *common, expand(0), dsk, s0, reverse=False, with_y=False, add_skip=False, lane0=0)
    (st_b,) = _ssd(xbc_c, dt_raw_c, *common, expand(heads), dsk, s0, reverse=True, with_y=False, add_skip=False,
                   lane0=heads)
    y_f, _ = _ssd(xbc, dt_raw, *common, expand(0), dsk, st_f, reverse=False, with_y=True, add_skip=True, lane0=0)
    y_b, _ = _ssd(xbc, dt_raw, *common, expand(heads), dsk, st_b, reverse=True, with_y=True, add_skip=False,
                  lane0=heads)

    group_of_lane = jnp.arange(inner) // gw
    gsum = (group_of_lane[:, None] == jnp.arange(LANES)[None, :]).astype(BF16)
    gexp = jnp.concatenate([gsum.T, gsum.T], axis=0)
    u_ssd_cm = _ssdbr(y_f, y_b, zgm, ssd_norm_w[0], b_merge[0, d:], gsum, gexp, w_br_ssd[0].astype(BF16), 256)

    u_lru = jnp.transpose(u_lru_tm, (1, 0, 2))
    u_ssd = _from_cols(u_ssd_cm, grid_rows)
    n_exp = router_w.shape[2]
    rw = jnp.pad(router_w[0], ((0, 0), (0, LANES - n_exp)))
    rw_hi = rw.astype(BF16)
    rw_lo = (rw - rw_hi.astype(F32)).astype(BF16)
    rw = jnp.concatenate([rw_hi, rw_hi, rw_lo], axis=0)
    rb = jnp.pad(router_b[0].reshape(1, n_exp), ((0, 0), (0, LANES - n_exp)), constant_values=NEG_BIG)
    x1, hp, gates, sel = _out(u_lru, u_ssd, x, g1[:, None, :], sh2[:, None, :], sc2[:, None, :],
                              w_out[0].astype(BF16), ln1_g[0], ln1_b[0], rw, rb, 512, alpha)

    return _moe(hp, gates, sel, x1, g2[:, None, :], moe_w1[0], moe_b1[0], moe_w2[0], moe_b2[0],
                ln2_g[0], ln2_b[0], alpha, 512, 512)
```

```python
import functools

import jax
import jax.numpy as jnp
from jax import lax
from jax.experimental import pallas as pl
from jax.experimental.pallas import tpu as pltpu

F32 = jnp.float32
BF16 = jnp.bfloat16
HIGHEST = lax.Precision.HIGHEST

GRID_W = 64
LRU_BLOCK_W = 64
LRU_C = 8.0
CONV_W = 4
SSD_HEAD_DIM = 64
SSD_GROUPS = 8
SSD_STATE = 128
SSD_CHUNK = 128
MOE_TOP_K = 4
SWIGLU_LIMIT = 7.0
SWIGLU_ALPHA = 1.702
LN_EPS = 1e-5
RMS_EPS = 1e-5
LANES = 128
NEG_BIG = -1e30
CONV_SUBTILE = 256
VMEM_LIMIT = 56 * 1024 * 1024


def _cparams(*sem):
    return pltpu.CompilerParams(dimension_semantics=sem, vmem_limit_bytes=VMEM_LIMIT)


def _ln_rows(x):
    mu = jnp.mean(x, axis=-1, keepdims=True)
    xc = x - mu
    var = jnp.mean(xc * xc, axis=-1, keepdims=True)
    return xc * lax.rsqrt(var + LN_EPS)


def _sigmoid(x):
    return 0.5 * (jnp.tanh(0.5 * x) + 1.0)


def _silu(x):
    h = 0.5 * x
    return h + h * jnp.tanh(h)


def _softplus(x):
    return jnp.maximum(x, 0.0) + jnp.log(1.0 + jnp.exp(-jnp.abs(x)))


def _ada_kernel(c_ref, w_ref, b_ref, o_ref):
    c = c_ref[...]
    o_ref[...] = jnp.dot(_silu(c), w_ref[...], precision=HIGHEST, preferred_element_type=F32) + b_ref[...]


def _ada(c_all, w, b):
    m, d = c_all.shape
    n = w.shape[1]
    tn = 1024
    return pl.pallas_call(
        _ada_kernel,
        out_shape=jax.ShapeDtypeStruct((m, n), F32),
        grid=(n // tn,),
        in_specs=[pl.BlockSpec((m, d), lambda j: (0, 0)),
                  pl.BlockSpec((d, tn), lambda j: (0, j)),
                  pl.BlockSpec((1, tn), lambda j: (0, j))],
        out_specs=pl.BlockSpec((m, tn), lambda j: (0, j)),
        compiler_params=_cparams("arbitrary"),
        name="ada",
    )(c_all, w, b.reshape(1, n))


def _inproj_kernel(x_ref, sh_ref, sc_ref, w_ref, *rest, conv, tail, head_blocks):
    o2_ref = None
    if head_blocks:
        xh_ref, rest = rest[0], rest[1:]
    if conv:
        cw_ref, cb_ref, o_ref, h_ref = rest
    elif tail:
        o_ref, o2_ref, h_ref = rest
    else:
        o_ref, h_ref = rest
    rows = h_ref.shape[0]

    def modulate(src_ref):
        h = _ln_rows(src_ref[...]) * (1.0 + sc_ref[...]) + sh_ref[...]
        h_ref[...] = h.reshape(h_ref.shape).astype(BF16)

    first_col = pl.program_id(1) == 0
    if head_blocks:
        in_head = pl.program_id(0) < head_blocks
        pl.when(first_col & in_head)(lambda: modulate(xh_ref))
        pl.when(first_col & jnp.logical_not(in_head))(lambda: modulate(x_ref))
    else:
        pl.when(first_col)(lambda: modulate(x_ref))

    if not conv:
        n1 = o_ref.shape[-1]
        acc = jnp.dot(h_ref[...], w_ref[:, 0:n1], preferred_element_type=F32)
        o_ref[...] = acc.reshape(o_ref.shape).astype(o_ref.dtype)
        if tail:
            acc2 = jnp.dot(h_ref[...], w_ref[:, n1:], preferred_element_type=F32)
            o2_ref[...] = acc2.reshape(o2_ref.shape).astype(o2_ref.dtype)
        return

    def taps(a, cw, cb, n, head, tail):
        r = lax.broadcasted_iota(jnp.int32, a.shape, 0)
        t0, t1, t3 = pltpu.roll(a, 2, 0), pltpu.roll(a, 1, 0), pltpu.roll(a, n - 1, 0)
        if head:
            t0, t1 = jnp.where(r >= 2, t0, 0.0), jnp.where(r >= 1, t1, 0.0)
        if tail:
            t3 = jnp.where(r < n - 1, t3, 0.0)
        return _silu(a * cw[2:3, :] + cb + t0 * cw[0:1, :] + t1 * cw[1:2, :] + t3 * cw[3:4, :])

    edge = 32
    half = edge // 2
    for lo in range(0, w_ref.shape[1], CONV_SUBTILE):
        cols = slice(lo, lo + CONV_SUBTILE)
        acc = jnp.dot(h_ref[...], w_ref[:, cols], preferred_element_type=F32)
        cw, cb = cw_ref[:, cols], cb_ref[:, cols]
        o_ref[0, :, cols] = taps(acc, cw, cb, rows, False, False).astype(o_ref.dtype)
        o_ref[0, 0:half, cols] = taps(acc[0:edge], cw, cb, edge, True, False)[0:half].astype(o_ref.dtype)
        o_ref[0, rows - half:rows, cols] = taps(acc[rows - edge:rows], cw, cb, edge, False, True)[half:].astype(
            o_ref.dtype)


def _inproj(x3, sh3, sc3, w, tile, tn, sel=None, conv_w=None, conv_b=None, out_dtype=F32, tail=None, x3_head=None):
    gdim, rdim, d = x3.shape
    g, r = tile
    n = w.shape[1]
    rows = g * r
    nblk_r = rdim // r
    nb0 = 0 if x3_head is None else x3_head.shape[0] // g
    nblk_g = gdim // g + nb0
    if sel is None:
        sel = lambda i: i
    conv = conv_w is not None
    mg, mr = sh3.shape[1], sh3.shape[2]
    in_specs = [pl.BlockSpec((g, r, d), lambda i, j: (jnp.maximum(i // nblk_r - nb0, 0), i % nblk_r, 0)),
                pl.BlockSpec((1, mg, mr), lambda i, j: (sel(i // nblk_r), 0, 0)),
                pl.BlockSpec((1, mg, mr), lambda i, j: (sel(i // nblk_r), 0, 0)),
                pl.BlockSpec((d, tn), lambda i, j: (0, j))]
    args = [x3, sh3, sc3, w]
    if nb0:
        assert nblk_r == 1 and x3_head.shape[1:] == x3.shape[1:]
        in_specs.append(pl.BlockSpec((g, r, d), lambda i, j: (jnp.minimum(i, nb0 - 1), 0, 0)))
        args.append(x3_head)
    if conv:
        assert r == rdim and g == 1
        in_specs += [pl.BlockSpec((CONV_W, tn), lambda i, j: (0, j)),
                     pl.BlockSpec((1, tn), lambda i, j: (0, j))]
        args += [conv_w, conv_b.reshape(1, n)]
    out_g = nblk_g * g
    out_idx = lambda i, j: (i // nblk_r, i % nblk_r, j)
    out_shape = jax.ShapeDtypeStruct((out_g, rdim, n), out_dtype)
    out_specs = pl.BlockSpec((g, r, tn), out_idx)
    if tail is not None:
        n2, dtype2 = tail
        assert tn == n and not conv
        out_shape = [jax.ShapeDtypeStruct((out_g, rdim, n - n2), out_dtype),
                     jax.ShapeDtypeStruct((out_g, rdim, n2), dtype2)]
        out_specs = [pl.BlockSpec((g, r, n - n2), out_idx), pl.BlockSpec((g, r, n2), out_idx)]
    return pl.pallas_call(
        functools.partial(_inproj_kernel, conv=conv, tail=tail is not None, head_blocks=nb0),
        out_shape=out_shape,
        grid=(nblk_g * nblk_r, n // tn),
        in_specs=in_specs,
        out_specs=out_specs,
        scratch_shapes=[pltpu.VMEM((rows, d), BF16)],
        compiler_params=_cparams("arbitrary", "arbitrary"),
        name="inproj_conv" if conv else "inproj",
    )(*args)


def _lru_gates(win_ref, w_ref, ba_ref, bx_ref, lam_ref, cw_ref, cb_ref, a_ref, b_ref, tt):
    bsz, c = win_ref.shape[1], win_ref.shape[2]
    cw = cw_ref[...]
    u = cb_ref[...].reshape(1, 1, c) + sum(win_ref[pl.ds(j, tt)] * cw[j:j + 1, :].reshape(1, 1, c)
                                           for j in range(CONV_W))
    u2 = u.reshape(tt * bsz, c)
    ub = u2.astype(BF16)
    sp = _softplus(-lam_ref[...])
    pack = w_ref.shape[1]
    for j in range(c // pack):
        lo = j * pack
        pre = jnp.dot(ub[:, lo:lo + pack], w_ref[j], preferred_element_type=F32)
        r = _sigmoid(pre[:, :pack] + ba_ref[:, lo:lo + pack])
        i = _sigmoid(pre[:, pack:] + bx_ref[:, lo:lo + pack])
        log_a = (-LRU_C) * r * sp[:, lo:lo + pack]
        a = jnp.exp(log_a)
        bt = jnp.sqrt(1.0 - jnp.exp(2.0 * log_a)) * (i * u2[:, lo:lo + pack])
        a_ref[:, :, lo:lo + pack] = a.reshape(tt, bsz, pack)
        b_ref[:, :, lo:lo + pack] = bt.reshape(tt, bsz, pack)


def _lru_fill_window(win_ref, x_ref, prev_ref, next_ref, at_start, at_end, tt):
    zero2 = jnp.zeros(prev_ref.shape, F32)
    win_ref[pl.ds(0, 2)] = jnp.where(at_start, zero2, prev_ref[...])
    win_ref[pl.ds(2, tt)] = x_ref[...]
    win_ref[pl.ds(tt + 2, 1)] = jnp.where(at_end, jnp.zeros(next_ref.shape, F32), next_ref[...])


def _lru_bwd_kernel(x_ref, prev_ref, next_ref, w_ref, ba_ref, bx_ref, lam_ref, cw_ref, cb_ref,
                    hb_ref, win_ref, a_ref, b_ref, h_ref, *, tt, ncb, nlb):
    i = pl.program_id(0)
    blk = jnp.where(i < ncb, ncb - 1 - i, ncb + nlb - 1 - (i - ncb))
    at_start = (blk == 0) | (blk == ncb)
    at_end = (blk == ncb - 1) | (blk == ncb + nlb - 1)

    @pl.when(i == 0)
    def _():
        h_ref[...] = jnp.zeros(h_ref.shape, F32)

    _lru_fill_window(win_ref, x_ref, prev_ref, next_ref, at_start, at_end, tt)
    _lru_gates(win_ref, w_ref, ba_ref, bx_ref, lam_ref, cw_ref, cb_ref, a_ref, b_ref, tt)

    def step(k, h):
        t = tt - 1 - k
        h = a_ref[t] * h + b_ref[t]
        a_ref[t] = h
        return h

    h_ref[...] = lax.fori_loop(0, tt, step, h_ref[...])

    @pl.when(i >= ncb)
    def _():
        hb_ref[...] = a_ref[...].astype(hb_ref.dtype)


def _lru_fwd_kernel(x_ref, prev_ref, next_ref, w_ref, ba_ref, bx_ref, lam_ref, cw_ref, cb_ref,
                    hb_ref, gr_ref, gm_ref, bm_ref, wbr_ref,
                    o_ref, win_ref, a_ref, b_ref, h_ref, *, tt, ncb, nlb):
    i = pl.program_id(0)
    at_start = (i == 0) | (i == ncb)
    at_end = (i == ncb - 1) | (i == ncb + nlb - 1)

    @pl.when(i == 0)
    def _():
        h_ref[...] = jnp.zeros(h_ref.shape, F32)

    _lru_fill_window(win_ref, x_ref, prev_ref, next_ref, at_start, at_end, tt)
    _lru_gates(win_ref, w_ref, ba_ref, bx_ref, lam_ref, cw_ref, cb_ref, a_ref, b_ref, tt)

    def step(t, h):
        h = a_ref[t] * h + b_ref[t]
        a_ref[t] = h
        return h

    h_ref[...] = lax.fori_loop(0, tt, step, h_ref[...])

    @pl.when(i >= ncb)
    def _():
        bsz, c = h_ref.shape
        rows = tt * bsz
        a_lat = (a_ref[...] + hb_ref[...].astype(F32)) * jax.nn.gelu(gr_ref[...].astype(F32), approximate=True)
        proj = jnp.dot(a_lat.reshape(rows, c).astype(BF16), wbr_ref[...], preferred_element_type=F32)
        gate = _sigmoid(gm_ref[...].astype(F32).reshape(rows, -1) + bm_ref[...])
        o_ref[...] = (gate * proj).reshape(o_ref.shape).astype(o_ref.dtype)


def _lru_specs(tt, bsz, c, blk_of, ttot):
    half = tt // 2
    return [pl.BlockSpec((tt, bsz, c), lambda i: (blk_of(i), 0, 0)),
            pl.BlockSpec((2, bsz, c), lambda i: (jnp.maximum(blk_of(i) * half - 1, 0), 0, 0)),
            pl.BlockSpec((1, bsz, c), lambda i: (jnp.minimum(blk_of(i) * tt + tt, ttot - 1), 0, 0))]


def _lru(xr_all, grgm, p_f, p_b, conv_w, conv_b, b_merge_lru, w_br, t_ctx, tt):
    ttot, bsz, c = xr_all.shape
    ncb, nlb = t_ctx // tt, (ttot - t_ctx) // tt
    nblk = ncb + nlb
    d = w_br.shape[1]
    const2 = lambda i: (0, 0)
    const3 = lambda i: (0, 0, 0)
    par_specs = [pl.BlockSpec(p_f[0].shape, const3), pl.BlockSpec((1, c), const2), pl.BlockSpec((1, c), const2),
                 pl.BlockSpec((1, c), const2), pl.BlockSpec((CONV_W, c), const2), pl.BlockSpec((1, c), const2)]
    scratch = [pltpu.VMEM((tt + 3, bsz, c), F32), pltpu.VMEM((tt, bsz, c), F32),
               pltpu.VMEM((tt, bsz, c), F32), pltpu.VMEM((bsz, c), F32)]

    bwd_blk = lambda i: jnp.where(i < ncb, ncb - 1 - i, ncb + nlb - 1 - (i - ncb))
    hb = pl.pallas_call(
        functools.partial(_lru_bwd_kernel, tt=tt, ncb=ncb, nlb=nlb),
        out_shape=jax.ShapeDtypeStruct((nlb * tt, bsz, c), BF16),
        grid=(nblk,),
        in_specs=_lru_specs(tt, bsz, c, bwd_blk, ttot) + par_specs,
        out_specs=pl.BlockSpec((tt, bsz, c), lambda i: (jnp.where(i < ncb, nlb - 1, nblk - 1 - i), 0, 0)),
        scratch_shapes=scratch,
        compiler_params=_cparams("arbitrary"),
        name="lru_bwd",
    )(xr_all, xr_all, xr_all, *p_b, conv_w, conv_b.reshape(1, c))

    assert c == d
    lat = lambda i: (jnp.maximum(i - ncb, 0), 0, 0)
    lat1 = lambda i: (jnp.maximum(i - ncb, 0), 0, 1)
    return pl.pallas_call(
        functools.partial(_lru_fwd_kernel, tt=tt, ncb=ncb, nlb=nlb),
        out_shape=jax.ShapeDtypeStruct((nlb * tt, bsz, d), BF16),
        grid=(nblk,),
        in_specs=_lru_specs(tt, bsz, c, lambda i: i, ttot) + par_specs + [
            pl.BlockSpec((tt, bsz, c), lat), pl.BlockSpec((tt, bsz, c), lat), pl.BlockSpec((tt, bsz, d), lat1),
            pl.BlockSpec((1, d), const2), pl.BlockSpec((c, d), const2)],
        out_specs=pl.BlockSpec((tt, bsz, d), lat),
        scratch_shapes=scratch,
        compiler_params=_cparams("arbitrary"),
        name="lru_fwd",
    )(xr_all, xr_all, xr_all, *p_f, conv_w, conv_b.reshape(1, c), hb, grgm, grgm, b_merge_lru.reshape(1, d), w_br)


def _lru_params(wa, ba, wx, bx, lam, pack):
    nb, bw, _ = wa.shape
    per = pack // bw
    c = nb * bw

    def bd(w):
        w4 = w.reshape(nb // per, per, bw, bw)
        eye = jnp.eye(per, dtype=w.dtype)
        return jnp.einsum('gpde,pq->gpdqe', w4, eye).reshape(nb // per, pack, pack)

    w = jnp.concatenate([bd(wa), bd(wx)], axis=-1).astype(BF16)
    return w, ba.reshape(1, c), bx.reshape(1, c), lam.reshape(1, c)


def _hi_lo(v):
    hi = v.astype(BF16)
    lo = (v - hi.astype(F32)).astype(BF16)
    return jnp.concatenate([hi, lo], axis=-1)


def _ssd_kernel(xbc_ref, dt_ref, dtb_ref, alog_ref, e_ref, dsk_ref, h0_ref, *outs,
                reverse, with_y, add_skip, lane0, inner):
    if with_y:
        y_ref, hfin_ref, st_ref = outs
    else:
        hfin_ref, st_ref = outs
    i = pl.program_id(1)
    q = SSD_CHUNK
    n = SSD_STATE
    gw = inner // SSD_GROUPS
    hpg = gw // SSD_HEAD_DIM

    @pl.when(i == 0)
    def _():
        st_ref[...] = h0_ref[0]

    dt = _softplus(dt_ref[0] + dtb_ref[...])
    da = dt * (-jnp.exp(alog_ref[...]))
    ri = lax.broadcasted_iota(jnp.int32, (q, q), 0)
    ci = lax.broadcasted_iota(jnp.int32, (q, q), 1)
    tri = (ri <= ci) if reverse else (ri >= ci)
    acum = jnp.dot(tri.astype(F32), da, precision=HIGHEST, preferred_element_type=F32)
    a_tot = jnp.sum(da, axis=0, keepdims=True)
    w_state = jnp.exp(a_tot - acum) * dt
    e2 = e_ref[...]
    ws_x = jnp.dot(w_state.astype(BF16), e2[0:LANES, :], preferred_element_type=F32)
    dec_x = jnp.dot(_hi_lo(jnp.broadcast_to(jnp.exp(a_tot), (8, LANES))), e2,
                    preferred_element_type=F32)[0:1, :]
    if with_y:
        eac_x = jnp.dot(_hi_lo(jnp.exp(acum)), e2, preferred_element_type=F32)
        acum_t = acum.T
        dt_t = dt.T
        rb = lax.broadcasted_iota(jnp.int32, (hpg * q, gw), 0) // q
        lb = lax.broadcasted_iota(jnp.int32, (hpg * q, gw), 1) // SSD_HEAD_DIM
        bd_mask = rb == lb

    for g in range(SSD_GROUPS):
        lo = g * gw
        xg = xbc_ref[0, :, lo:lo + gw]
        bg = xbc_ref[0, :, inner + g * n:inner + (g + 1) * n]
        cg = xbc_ref[0, :, inner + (SSD_GROUPS + g) * n:inner + (SSD_GROUPS + g + 1) * n]
        xgf = xg.astype(F32)
        xw = (xgf * ws_x[:, lo:lo + gw]).astype(BF16)
        st = st_ref[g]
        upd = lax.dot_general(bg, xw, (((0,), (0,)), ((), ())), preferred_element_type=F32)
        st_ref[g] = dec_x[:, lo:lo + gw] * st + upd
        if with_y:
            y_off = jnp.dot(cg, st.astype(BF16), preferred_element_type=F32) * eac_x[:, lo:lo + gw]
            cb = lax.dot_general(cg, bg, (((1,), (1,)), ((), ())), preferred_element_type=F32)
            ls = []
            for r in range(hpg):
                lane = lane0 + g * hpg + r
                seg = acum[:, lane:lane + 1] - acum_t[lane:lane + 1, :]
                l_h = cb * jnp.exp(jnp.where(tri, seg, NEG_BIG)) * dt_t[lane:lane + 1, :]
                ls.append(l_h.astype(BF16))
            lcat = jnp.concatenate(ls, axis=1)
            xbd = jnp.where(bd_mask, jnp.concatenate([xg] * hpg, axis=0), jnp.zeros((), BF16))
            y = y_off + jnp.dot(lcat, xbd, preferred_element_type=F32)
            if add_skip:
                y = y + dsk_ref[:, lo:lo + gw] * xgf
            y_ref[0, :, lo:lo + gw] = y.astype(y_ref.dtype)

    @pl.when(i == pl.num_programs(1) - 1)
    def _():
        hfin_ref[0] = st_ref[...]


def _ssd(xbc, dt_raw, dtb, alog, e2, dsk, h0, *, reverse, with_y, add_skip, lane0):
    bsz, s, width = xbc.shape
    inner = e2.shape[1]
    nc = s // SSD_CHUNK
    gw = inner // SSD_GROUPS
    cidx = (lambda b, i: (b, nc - 1 - i, 0)) if reverse else (lambda b, i: (b, i, 0))
    const2 = lambda b, i: (0, 0)
    st_spec = pl.BlockSpec((1, SSD_GROUPS, SSD_STATE, gw), lambda b, i: (b, 0, 0, 0))
    st_shape = jax.ShapeDtypeStruct((bsz, SSD_GROUPS, SSD_STATE, gw), F32)
    out_shape, out_specs = [st_shape], [st_spec]
    if with_y:
        out_shape = [jax.ShapeDtypeStruct((bsz, s, inner), BF16)] + out_shape
        out_specs = [pl.BlockSpec((1, SSD_CHUNK, inner), cidx)] + out_specs
    return pl.pallas_call(
        functools.partial(_ssd_kernel, reverse=reverse, with_y=with_y, add_skip=add_skip, lane0=lane0, inner=inner),
        out_shape=out_shape,
        grid=(bsz, nc),
        in_specs=[pl.BlockSpec((1, SSD_CHUNK, width), cidx),
                  pl.BlockSpec((1, SSD_CHUNK, LANES), cidx),
                  pl.BlockSpec((1, LANES), const2), pl.BlockSpec((1, LANES), const2),
                  pl.BlockSpec(e2.shape, const2), pl.BlockSpec((1, inner), const2), st_spec],
        out_specs=out_specs,
        scratch_shapes=[pltpu.VMEM((SSD_GROUPS, SSD_STATE, gw), F32)],
        compiler_params=_cparams("arbitrary", "arbitrary"),
        name="ssd_y" if with_y else "ssd_state",
    )(xbc, dt_raw, dtb, alog, e2, dsk, h0)


def _ssdbr_kernel(yf_ref, yb_ref, z_ref, gm_ref, nw_ref, bm_ref, gsum_ref, gexp_ref, w_ref, o_ref, *, group_w):
    y = (yf_ref[0].astype(F32) + yb_ref[0].astype(F32)) * _silu(z_ref[0].astype(F32))
    ms = jnp.dot((y * y).astype(BF16), gsum_ref[...], preferred_element_type=F32) * (1.0 / group_w)
    rs = lax.rsqrt(ms + RMS_EPS)
    rs_x = jnp.dot(_hi_lo(rs), gexp_ref[...], preferred_element_type=F32)
    yn = (y * rs_x * nw_ref[...]).astype(BF16)
    proj = jnp.dot(yn, w_ref[...], preferred_element_type=F32)
    o_ref[0] = (_sigmoid(gm_ref[0].astype(F32) + bm_ref[...]) * proj).astype(o_ref.dtype)


def _ssdbr(y_f, y_b, zgm, norm_w, b_merge_ssd, gsum, gexp, w_br, rows):
    bsz, s, inner = y_f.shape
    d = w_br.shape[1]
    assert inner % d == 0
    tok = lambda b, i: (b, i, 0)
    const2 = lambda b, i: (0, 0)
    return pl.pallas_call(
        functools.partial(_ssdbr_kernel, group_w=inner // SSD_GROUPS),
        out_shape=jax.ShapeDtypeStruct((bsz, s, d), BF16),
        grid=(bsz, s // rows),
        in_specs=[pl.BlockSpec((1, rows, inner), tok), pl.BlockSpec((1, rows, inner), tok),
                  pl.BlockSpec((1, rows, inner), tok),
                  pl.BlockSpec((1, rows, d), lambda b, i: (b, i, inner // d)),
                  pl.BlockSpec((1, inner), const2), pl.BlockSpec((1, d), const2),
                  pl.BlockSpec(gsum.shape, const2), pl.BlockSpec(gexp.shape, const2),
                  pl.BlockSpec((inner, d), const2)],
        out_specs=pl.BlockSpec((1, rows, d), tok),
        compiler_params=_cparams("arbitrary", "arbitrary"),
        name="ssdbr",
    )(y_f, y_b, zgm, zgm, norm_w.reshape(1, inner), b_merge_ssd.reshape(1, d), gsum, gexp, w_br)


def _pack_bf16_pair(lo, hi):
    lo_bits = lax.bitcast_convert_type(lo.astype(BF16).astype(F32), jnp.uint32)
    hi_bits = lax.bitcast_convert_type(hi.astype(BF16).astype(F32), jnp.uint32)
    return (lo_bits >> 16) | hi_bits


def _unpack_bf16_pair(p):
    lo = lax.bitcast_convert_type(p << 16, F32).astype(BF16)
    hi = lax.bitcast_convert_type(p & jnp.uint32(0xFFFF0000), F32).astype(BF16)
    return jnp.concatenate([lo, hi], axis=1)


def _out_kernel(ul_ref, us_ref, x_ref, g1_ref, sh2_ref, sc2_ref, wout_ref, lg_ref, lb_ref, rw_ref, rb_ref,
                x1_ref, hp_ref, gates_ref, sel_ref, *, alpha):
    u = (ul_ref[0].astype(F32) + us_ref[0].astype(F32)).astype(BF16)
    mix = jnp.dot(u, wout_ref[...], preferred_element_type=F32)
    x1 = _ln_rows(alpha * x_ref[0] + g1_ref[0] * mix) * lg_ref[...] + lb_ref[...]
    x1_ref[0] = x1
    h2 = _ln_rows(x1) * (1.0 + sc2_ref[0]) + sh2_ref[0]
    half = h2.shape[1] // 2
    hp_ref[0] = _pack_bf16_pair(h2[:, :half], h2[:, half:])
    h_hi = h2.astype(BF16)
    h_lo = (h2 - h_hi.astype(F32)).astype(BF16)
    logits = jnp.dot(jnp.concatenate([h_hi, h_lo, h_hi], axis=1), rw_ref[...],
                     preferred_element_type=F32) + rb_ref[...]
    lane = lax.broadcasted_iota(jnp.int32, logits.shape, 1)
    work = logits
    sel = jnp.zeros(logits.shape, jnp.bool_)
    top = None
    for k in range(MOE_TOP_K):
        m = jnp.max(work, axis=1, keepdims=True)
        if k == 0:
            top = m
        idx = jnp.min(jnp.where(work == m, lane, LANES), axis=1, keepdims=True)
        pick = lane == idx
        sel = sel | pick
        work = jnp.where(pick, 3.0 * NEG_BIG, work)
    e = jnp.where(sel, jnp.exp(logits - top), 0.0)
    gates_ref[0] = e / jnp.sum(e, axis=1, keepdims=True)
    sel_ref[0] = sel.astype(F32)


def _out(u_l, u_s, x, g1, sh2, sc2, w_out, ln_g, ln_b, rw, rb, rows, alpha):
    bsz, t, d = x.shape
    tok = lambda b, i: (b, i, 0)
    per_b = lambda b, i: (b, 0, 0)
    const2 = lambda b, i: (0, 0)
    return pl.pallas_call(
        functools.partial(_out_kernel, alpha=alpha),
        out_shape=[jax.ShapeDtypeStruct((bsz, t, d), F32), jax.ShapeDtypeStruct((bsz, t, d // 2), jnp.uint32),
                   jax.ShapeDtypeStruct((bsz, t, LANES), F32), jax.ShapeDtypeStruct((bsz, t, LANES), F32)],
        grid=(bsz, t // rows),
        in_specs=[pl.BlockSpec((1, rows, d), tok), pl.BlockSpec((1, rows, d), tok), pl.BlockSpec((1, rows, d), tok),
                  pl.BlockSpec((1, 1, d), per_b), pl.BlockSpec((1, 1, d), per_b), pl.BlockSpec((1, 1, d), per_b),
                  pl.BlockSpec((d, d), const2), pl.BlockSpec((1, d), const2), pl.BlockSpec((1, d), const2),
                  pl.BlockSpec((3 * d, LANES), const2), pl.BlockSpec((1, LANES), const2)],
        out_specs=[pl.BlockSpec((1, rows, d), tok), pl.BlockSpec((1, rows, d // 2), tok),
                   pl.BlockSpec((1, rows, LANES), tok), pl.BlockSpec((1, rows, LANES), tok)],
        compiler_params=_cparams("arbitrary", "arbitrary"),
        name="out",
    )(u_l, u_s, x, g1, sh2, sc2, w_out, ln_g.reshape(1, d), ln_b.reshape(1, d), rw, rb)


def _rank_kernel(sel_ref, rank_ref, cnt_ref, carry_ref):
    @pl.when(pl.program_id(0) == 0)
    def _():
        carry_ref[...] = jnp.zeros(carry_ref.shape, F32)

    s = sel_ref[...]
    tt = s.shape[0]
    ri = lax.broadcasted_iota(jnp.int32, (tt, tt), 0)
    ci = lax.broadcasted_iota(jnp.int32, (tt, tt), 1)
    earlier = (ri > ci).astype(BF16)
    rank_ref[...] = jnp.dot(earlier, s.astype(BF16), preferred_element_type=F32) + carry_ref[0:1, :]
    carry_ref[...] = carry_ref[...] + jnp.sum(s, axis=0, keepdims=True)
    cnt_ref[...] = carry_ref[...]


def _rank(sel, tt):
    n = sel.shape[0]
    return pl.pallas_call(
        _rank_kernel,
        out_shape=[jax.ShapeDtypeStruct((n, LANES), F32), jax.ShapeDtypeStruct((8, LANES), F32)],
        grid=(n // tt,),
        in_specs=[pl.BlockSpec((tt, LANES), lambda i: (i, 0))],
        out_specs=[pl.BlockSpec((tt, LANES), lambda i: (i, 0)), pl.BlockSpec((8, LANES), lambda i: (0, 0))],
        scratch_shapes=[pltpu.VMEM((8, LANES), F32)],
        compiler_params=_cparams("arbitrary"),
        name="moe_rank",
    )(sel)


def _pos_kernel(sel_ref, rank_ref, gates_ref, off_ref, pos_ref, w_ref):
    avail = sel_ref[...] > 0.5
    posf = off_ref[...] + rank_ref[...]
    gates = gates_ref[...]
    lane = lax.broadcasted_iota(jnp.int32, posf.shape, 1)
    cols_p = jnp.zeros(posf.shape, F32)
    cols_w = jnp.zeros(posf.shape, F32)
    for k in range(MOE_TOP_K):
        m = jnp.min(jnp.where(avail, lane, LANES), axis=1, keepdims=True)
        pick = lane == m
        cols_p = jnp.where(lane == k, jnp.sum(jnp.where(pick, posf, 0.0), axis=1, keepdims=True), cols_p)
        cols_w = jnp.where(lane == k, jnp.sum(jnp.where(pick, gates, 0.0), axis=1, keepdims=True), cols_w)
        avail = avail & jnp.logical_not(pick)
    w_ref[...] = cols_w
    pos_ref[...] = cols_p.astype(jnp.int32)


def _pos(sel, rank, gates, off, tt):
    n = sel.shape[0]
    tok = lambda i: (i, 0)
    return pl.pallas_call(
        _pos_kernel,
        out_shape=[jax.ShapeDtypeStruct((n, LANES), jnp.int32), jax.ShapeDtypeStruct((n, LANES), F32)],
        grid=(n // tt,),
        in_specs=[pl.BlockSpec((tt, LANES), tok), pl.BlockSpec((tt, LANES), tok), pl.BlockSpec((tt, LANES), tok),
                  pl.BlockSpec((1, LANES), lambda i: (0, 0))],
        out_specs=[pl.BlockSpec((tt, LANES), tok), pl.BlockSpec((tt, LANES), tok)],
        compiler_params=_cparams("arbitrary"),
        name="moe_pos",
    )(sel, rank, gates, off)


ROW_GROUP = 8
ROW_TILE = 8


def _dispatch_kernel(pos_ref, hp_ref, xs_init_hbm, xs_hbm, sem, *, tt):
    del xs_init_hbm

    def body(i, carry):
        row0 = pl.multiple_of(i * ROW_GROUP, ROW_GROUP)
        p0 = i * (ROW_GROUP * MOE_TOP_K)
        for u in range(ROW_GROUP):
            src = hp_ref.at[pl.ds(row0 + u, 1)]
            for k in range(MOE_TOP_K):
                dst = xs_hbm.at[pl.ds(pos_ref[p0 + u * MOE_TOP_K + k], 1)]
                pltpu.make_async_copy(src, dst, sem).start(priority=k % 2)
        return carry

    lax.fori_loop(0, tt // ROW_GROUP, body, 0)
    for k in range(MOE_TOP_K):
        pltpu.make_async_copy(hp_ref, xs_hbm.at[pl.ds(0, tt)], sem).wait()


def _dispatch(pos, hp, xs_init, tt):
    n, half = hp.shape
    any_spec = pl.BlockSpec(memory_space=pl.ANY)
    return pl.pallas_call(
        functools.partial(_dispatch_kernel, tt=tt),
        out_shape=jax.ShapeDtypeStruct(xs_init.shape, xs_init.dtype),
        grid=(n // tt,),
        in_specs=[pl.BlockSpec((tt * MOE_TOP_K,), lambda i: (i,), memory_space=pltpu.SMEM),
                  pl.BlockSpec((tt, half), lambda i: (i, 0)), any_spec],
        out_specs=any_spec,
        scratch_shapes=[pltpu.SemaphoreType.DMA(())],
        input_output_aliases={2: 0},
        compiler_params=pltpu.CompilerParams(dimension_semantics=("arbitrary",), has_side_effects=True),
        name="moe_dispatch",
    )(pos, hp, xs_init)


def _expert_kernel(te_ref, nv_ref, xs_ref, w1_ref, b1_ref, w2_ref, b2_ref, ys_ref, w1b_ref, w2b_ref, *, ff):
    t = pl.program_id(0)
    e = te_ref[t]
    prev = te_ref[jnp.maximum(t - 1, 0)]

    @pl.when((t == 0) | (e != prev))
    def _():
        w1b_ref[...] = w1_ref[0].astype(BF16)
        w2b_ref[...] = w2_ref[0].astype(BF16)

    @pl.when(t < nv_ref[0])
    def _():
        xrow = _unpack_bf16_pair(xs_ref[...])
        gu = jnp.dot(xrow, w1b_ref[...], preferred_element_type=F32) + b1_ref[0]
        g = jnp.minimum(gu[:, :ff], SWIGLU_LIMIT)
        u = jnp.clip(gu[:, ff:], -SWIGLU_LIMIT, SWIGLU_LIMIT)
        act = ((u + 1.0) * g * _sigmoid(SWIGLU_ALPHA * g)).astype(BF16)
        y = jnp.dot(act, w2b_ref[...], preferred_element_type=F32) + b2_ref[0]
        for g in range(y.shape[1] // LANES):
            ys_ref[pl.ds(g, y.shape[0], stride=ROW_TILE), :] = y[:, g * LANES:(g + 1) * LANES]

    @pl.when(t >= nv_ref[0])
    def _():
        ys_ref[...] = jnp.zeros(ys_ref.shape, F32)


def _experts(tile_expert, n_valid, xs, w1, b1, w2, b2, tm):
    rows, half = xs.shape
    n_exp, d, ff2 = w1.shape
    ff = ff2 // 2
    per_e = lambda t, te, nv: (te[t], 0, 0)
    assert d == ROW_TILE * LANES
    return pl.pallas_call(
        functools.partial(_expert_kernel, ff=ff),
        out_shape=jax.ShapeDtypeStruct((rows * ROW_TILE, LANES), F32),
        grid_spec=pltpu.PrefetchScalarGridSpec(
            num_scalar_prefetch=2,
            grid=(rows // tm,),
            in_specs=[pl.BlockSpec((tm, half), lambda t, te, nv: (t, 0)),
                      pl.BlockSpec((1, d, ff2), per_e), pl.BlockSpec((1, 1, ff2), per_e),
                      pl.BlockSpec((1, ff, d), per_e), pl.BlockSpec((1, 1, d), per_e)],
            out_specs=pl.BlockSpec((tm * ROW_TILE, LANES), lambda t, te, nv: (t, 0)),
            scratch_shapes=[pltpu.VMEM((d, ff2), BF16), pltpu.VMEM((ff, d), BF16)]),
        compiler_params=_cparams("arbitrary"),
        name="moe_experts",
    )(tile_expert, n_valid, xs, w1, b1.reshape(n_exp, 1, ff2), w2, b2.reshape(n_exp, 1, d))


def _combine_kernel(pos_ref, ys_hbm, ys_flat_hbm, w_ref, x1_ref, g2_ref, lg_ref, lb_ref, o_ref, buf_ref, sem,
                    *, tt, alpha):
    def body(i, carry):
        row0 = pl.multiple_of(i * ROW_GROUP, ROW_GROUP)
        p0 = i * (ROW_GROUP * MOE_TOP_K)
        for u in range(ROW_GROUP):
            for k in range(MOE_TOP_K):
                src = ys_hbm.at[pos_ref[p0 + u * MOE_TOP_K + k]]
                dst = buf_ref.at[pl.ds(pl.multiple_of((k * tt + row0 + u) * ROW_TILE, ROW_TILE), ROW_TILE)]
                pltpu.make_async_copy(src, dst, sem).start(priority=k % 2)
        return carry

    lax.fori_loop(0, tt // ROW_GROUP, body, 0)
    span = tt * ROW_TILE
    for k in range(MOE_TOP_K):
        pltpu.make_async_copy(ys_flat_hbm.at[pl.ds(0, span)], buf_ref.at[pl.ds(k * span, span)], sem).wait()

    chunk = 64

    def rows(c, carry):
        r0 = pl.multiple_of(c * chunk, chunk)
        r = pl.ds(r0, chunk)
        w = w_ref[r, :]

        def slot_rows(k):
            first = (k * tt + r0) * ROW_TILE
            return jnp.concatenate([buf_ref[pl.ds(first + g, chunk, stride=ROW_TILE), :] for g in range(ROW_TILE)],
                                   axis=1)

        acc = sum(w[:, k:k + 1] * slot_rows(k) for k in range(MOE_TOP_K))
        o_ref[0, r, :] = _ln_rows(alpha * x1_ref[0, r, :] + g2_ref[0] * acc) * lg_ref[...] + lb_ref[...]
        return carry

    lax.fori_loop(0, tt // chunk, rows, 0)


def _combine(pos, ys, w, x1, g2, ln_g, ln_b, tt, alpha):
    bsz, t, d = x1.shape
    nt = t // tt
    return pl.pallas_call(
        functools.partial(_combine_kernel, tt=tt, alpha=alpha),
        out_shape=jax.ShapeDtypeStruct((bsz, t, d), F32),
        grid=(bsz, nt),
        in_specs=[pl.BlockSpec((tt * MOE_TOP_K,), lambda b, i: (b * nt + i,), memory_space=pltpu.SMEM),
                  pl.BlockSpec(memory_space=pl.ANY), pl.BlockSpec(memory_space=pl.ANY),
                  pl.BlockSpec((tt, LANES), lambda b, i: (b * nt + i, 0)),
                  pl.BlockSpec((1, tt, d), lambda b, i: (b, i, 0)),
                  pl.BlockSpec((1, 1, d), lambda b, i: (b, 0, 0)),
                  pl.BlockSpec((1, d), lambda b, i: (0, 0)), pl.BlockSpec((1, d), lambda b, i: (0, 0))],
        out_specs=pl.BlockSpec((1, tt, d), lambda b, i: (b, i, 0)),
        scratch_shapes=[pltpu.VMEM((MOE_TOP_K * tt * ROW_TILE, LANES), F32), pltpu.SemaphoreType.DMA(())],
        compiler_params=_cparams("arbitrary", "arbitrary"),
        name="moe_combine",
    )(pos, ys.reshape(-1, ROW_TILE, LANES), ys, w, x1, g2, ln_g.reshape(1, d), ln_b.reshape(1, d))


def _moe(hp, gates, sel, x1, g2, w1, b1, w2, b2, ln_g, ln_b, alpha, tt, tm):
    bsz, t, d = x1.shape
    n = bsz * t
    n_exp = w1.shape[0]
    sel2, gates2 = sel.reshape(n, LANES), gates.reshape(n, LANES)
    rank, cnt = _rank(sel2, tt)
    counts = cnt[0, :n_exp].astype(jnp.int32)
    tiles_per = (counts + tm - 1) // tm
    tile_end = jnp.cumsum(tiles_per)
    off = jnp.pad(((tile_end - tiles_per) * tm).astype(F32).reshape(1, n_exp), ((0, 0), (0, LANES - n_exp)))
    n_tiles = (n * MOE_TOP_K) // tm + n_exp
    n_valid = tile_end[-1:]
    tile_id = jnp.minimum(jnp.arange(n_tiles), n_valid - 1)
    tile_expert = jnp.sum((tile_end[None, :] <= tile_id[:, None]).astype(jnp.int32), axis=1)
    tile_expert = jnp.minimum(tile_expert, n_exp - 1)
    pos_lanes, w = _pos(sel2, rank, gates2, off, tt)
    pos = pos_lanes[:, :MOE_TOP_K].reshape(n * MOE_TOP_K)
    xs = _dispatch(pos, hp.reshape(n, d // 2), jnp.zeros((n_tiles * tm, d // 2), jnp.uint32), tt)
    ys = _experts(tile_expert, n_valid.astype(jnp.int32), xs, w1, b1, w2, b2, tm)
    return _combine(pos, ys, w, x1, g2, ln_g, ln_b, tt, alpha)


def _to_cols(u, grid_rows):
    b, t, c = u.shape
    return u.reshape(b, grid_rows, GRID_W, c).transpose(0, 2, 1, 3).reshape(b, t, c)


def _from_cols(u, grid_rows):
    b, t, c = u.shape
    return u.reshape(b, GRID_W, grid_rows, c).transpose(0, 2, 1, 3).reshape(b, t, c)


def kernel(x, c, ctx, c_ctx, w_ada, b_ada, w_in, b_merge, conv_lru_w, conv_lru_b, lru_wa, lru_ba, lru_wx, lru_bx, lru_lambda, conv_ssd_w, conv_ssd_b, ssd_dt_bias, ssd_a_log, ssd_d, ssd_norm_w, w_br_lru, w_br_ssd, w_out, ln1_g, ln1_b, router_w, router_b, moe_w1, moe_b1, moe_w2, moe_b2, ln2_g, ln2_b):
    depth = w_ada.shape[0]
    assert depth == 1, "single-layer stack: the context tokens only supply scan states"
    bsz, t, d = x.shape
    t_ctx = ctx.shape[1]
    alpha = (2.0 * depth) ** 0.25
    grid_rows = t // GRID_W
    d_rnn = w_br_lru.shape[1]
    inner = w_br_ssd.shape[1]
    heads = inner // SSD_HEAD_DIM
    gn = SSD_GROUPS * SSD_STATE
    col_gr = d_rnn
    col_z = col_gr + d_rnn
    col_xbc = col_z + inner
    col_dt = col_xbc + inner + 2 * gn
    col_gm = col_dt + 2 * heads
    assert 2 * heads <= LANES and t_ctx % SSD_CHUNK == 0 and t % SSD_CHUNK == 0

    pad = (-(bsz + 1)) % 8
    c_all = jnp.concatenate([c, c_ctx[None, :], jnp.zeros((pad, d), F32)], axis=0)
    mod = _ada(c_all, w_ada[0], b_ada[0])
    sh1, sc1, g1, sh2, sc2, g2 = (mod[:bsz, k * d:(k + 1) * d] for k in range(6))
    csh1, csc1 = mod[bsz:bsz + 1, 0:d], mod[bsz:bsz + 1, d:2 * d]

    w_in_b = w_in[0].astype(BF16)

    tt = 32
    zero = lambda i: 0
    x_tm, ctx_tm = jnp.transpose(x, (1, 0, 2)), jnp.transpose(ctx, (1, 0, 2))
    sh_tm = jnp.stack([jnp.broadcast_to(csh1, (bsz, d)), sh1])
    sc_tm = jnp.stack([jnp.broadcast_to(csc1, (bsz, d)), sc1])
    seg = lambda i: jnp.where(i * tt >= t_ctx, 1, 0)
    xr_all = _inproj(x_tm, sh_tm, sc_tm, w_in_b[:, :col_gr], (tt, bsz), 1024, sel=seg, x3_head=ctx_tm)
    w_grgm = jnp.concatenate([w_in_b[:, col_gr:col_z], w_in_b[:, col_gm:col_gm + d]], axis=1)
    grgm = _inproj(x_tm, sh_tm[1:], sc_tm[1:], w_grgm, (tt, bsz), w_grgm.shape[1], sel=zero, out_dtype=BF16)
    p_f = _lru_params(lru_wa[0, 0], lru_ba[0, 0], lru_wx[0, 0], lru_bx[0, 0], lru_lambda[0, 0], 256)
    p_b = _lru_params(lru_wa[0, 1], lru_ba[0, 1], lru_wx[0, 1], lru_bx[0, 1], lru_lambda[0, 1], 256)
    u_lru_tm = _lru(xr_all, grgm, p_f, p_b, conv_lru_w[0], conv_lru_b[0], b_merge[0, :d],
                    w_br_lru[0].astype(BF16), t_ctx, tt)

    x_cm = _to_cols(x, grid_rows)
    sh_b, sc_b = sh1[:, None, :], sc1[:, None, :]
    csh_b, csc_b = csh1[None], csc1[None]
    w_xbc = w_in_b[:, col_xbc:col_dt]
    w_dt = jnp.pad(w_in_b[:, col_dt:col_gm], ((0, 0), (0, LANES - 2 * heads)))
    w_zgm = jnp.concatenate([w_in_b[:, col_z:col_xbc], w_in_b[:, col_gm + d:]], axis=1)
    xbc = _inproj(x_cm, sh_b, sc_b, w_xbc, (1, t), 512, conv_w=conv_ssd_w[0], conv_b=conv_ssd_b[0], out_dtype=BF16)
    xbc_c = _inproj(ctx, csh_b, csc_b, w_xbc, (1, t_ctx), 512, sel=zero, conv_w=conv_ssd_w[0],
                    conv_b=conv_ssd_b[0], out_dtype=BF16)
    dt_raw_c = _inproj(ctx, csh_b, csc_b, w_dt, (1, t_ctx), LANES, sel=zero)
    w_zgm_dt = jnp.concatenate([w_zgm, w_dt], axis=1)
    zgm, dt_raw = _inproj(x_cm, sh_b, sc_b, w_zgm_dt, (1, 512), w_zgm_dt.shape[1], out_dtype=BF16,
                          tail=(LANES, F32))

    lane_pad = LANES - 2 * heads
    dtb = jnp.pad(ssd_dt_bias[0].reshape(1, 2 * heads), ((0, 0), (0, lane_pad)))
    alog = jnp.pad(ssd_a_log[0].reshape(1, 2 * heads), ((0, 0), (0, lane_pad)))
    dsk = jnp.repeat(ssd_d[0], SSD_HEAD_DIM).reshape(1, inner)
    head_of_lane = jnp.arange(inner) // SSD_HEAD_DIM

    def expand(lane0):
        e = (jnp.arange(LANES)[:, None] == head_of_lane[None, :] + lane0).astype(BF16)
        return jnp.concatenate([e, e], axis=0)

    gw = inner // SSD_GROUPS
    s0 = jnp.zeros((bsz, SSD_GROUPS, SSD_STATE, gw), F32)
    common = (dtb, alog)
    (st_f,) = _ssd(xbc_c, dt_raw_c, *common, expand(0), dsk, s0, reverse=False, with_y=False, add_skip=False, lane0=0)
    (st_b,) = _ssd(xbc_c, dt_raw_c, *common, expand(heads), dsk, s0, reverse=True, with_y=False, add_skip=False,
                   lane0=heads)
    y_f, _ = _ssd(xbc, dt_raw, *common, expand(0), dsk, st_f, reverse=False, with_y=True, add_skip=True, lane0=0)
    y_b, _ = _ssd(xbc, dt_raw, *common, expand(heads), dsk, st_b, reverse=True, with_y=True, add_skip=False,
                  lane0=heads)

    group_of_lane = jnp.arange(inner) // gw
    gsum = (group_of_lane[:, None] == jnp.arange(LANES)[None, :]).astype(BF16)
    gexp = jnp.concatenate([gsum.T, gsum.T], axis=0)
    u_ssd_cm = _ssdbr(y_f, y_b, zgm, ssd_norm_w[0], b_merge[0, d:], gsum, gexp, w_br_ssd[0].astype(BF16), 256)

    u_lru = jnp.transpose(u_lru_tm, (1, 0, 2))
    u_ssd = _from_cols(u_ssd_cm, grid_rows)
    n_exp = router_w.shape[2]
    rw = jnp.pad(router_w[0], ((0, 0), (0, LANES - n_exp)))
    rw_hi = rw.astype(BF16)
    rw_lo = (rw - rw_hi.astype(F32)).astype(BF16)
    rw = jnp.concatenate([rw_hi, rw_hi, rw_lo], axis=0)
    rb = jnp.pad(router_b[0].reshape(1, n_exp), ((0, 0), (0, LANES - n_exp)), constant_values=NEG_BIG)
    x1, hp, gates, sel = _out(u_lru, u_ssd, x, g1[:, None, :], sh2[:, None, :], sc2[:, None, :],
                              w_out[0].astype(BF16), ln1_g[0], ln1_b[0], rw, rb, 512, alpha)

    return _moe(hp, gates, sel, x1, g2[:, None, :], moe_w1[0], moe_b1[0], moe_w2[0], moe_b2[0],
                ln2_g[0], ln2_b[0], alpha, 512, 512)
```

```python
import functools

import jax
import jax.numpy as jnp
from jax import lax
from jax.experimental import pallas as pl
from jax.experimental.pallas import tpu as pltpu

F32 = jnp.float32
BF16 = jnp.bfloat16
HIGHEST = lax.Precision.HIGHEST

GRID_W = 64
LRU_BLOCK_W = 64
LRU_C = 8.0
CONV_W = 4
SSD_HEAD_DIM = 64
SSD_GROUPS = 8
SSD_STATE = 128
SSD_CHUNK = 128
MOE_TOP_K = 4
SWIGLU_LIMIT = 7.0
SWIGLU_ALPHA = 1.702
LN_EPS = 1e-5
RMS_EPS = 1e-5
LANES = 128
NEG_BIG = -1e30
CONV_SUBTILE = 256
VMEM_LIMIT = 56 * 1024 * 1024


def _cparams(*sem):
    return pltpu.CompilerParams(dimension_semantics=sem, vmem_limit_bytes=VMEM_LIMIT)


def _ln_rows(x):
    mu = jnp.mean(x, axis=-1, keepdims=True)
    xc = x - mu
    var = jnp.mean(xc * xc, axis=-1, keepdims=True)
    return xc * lax.rsqrt(var + LN_EPS)


def _sigmoid(x):
    return 0.5 * (jnp.tanh(0.5 * x) + 1.0)


def _silu(x):
    h = 0.5 * x
    return h + h * jnp.tanh(h)


def _softplus(x):
    return jnp.maximum(x, 0.0) + jnp.log(1.0 + jnp.exp(-jnp.abs(x)))


def _ada_kernel(c_ref, w_ref, b_ref, o_ref):
    c = c_ref[...]
    o_ref[...] = jnp.dot(_silu(c), w_ref[...], precision=HIGHEST, preferred_element_type=F32) + b_ref[...]


def _ada(c_all, w, b):
    m, d = c_all.shape
    n = w.shape[1]
    tn = 1024
    return pl.pallas_call(
        _ada_kernel,
        out_shape=jax.ShapeDtypeStruct((m, n), F32),
        grid=(n // tn,),
        in_specs=[pl.BlockSpec((m, d), lambda j: (0, 0)),
                  pl.BlockSpec((d, tn), lambda j: (0, j)),
                  pl.BlockSpec((1, tn), lambda j: (0, j))],
        out_specs=pl.BlockSpec((m, tn), lambda j: (0, j)),
        compiler_params=_cparams("arbitrary"),
        name="ada",
    )(c_all, w, b.reshape(1, n))


def _inproj_kernel(x_ref, sh_ref, sc_ref, w_ref, *rest, conv, tail, head_blocks):
    o2_ref = None
    if head_blocks:
        xh_ref, rest = rest[0], rest[1:]
    if conv:
        cw_ref, cb_ref, o_ref, h_ref = rest
    elif tail:
        o_ref, o2_ref, h_ref = rest
    else:
        o_ref, h_ref = rest
    rows = h_ref.shape[0]

    def modulate(src_ref):
        h = _ln_rows(src_ref[...]) * (1.0 + sc_ref[...]) + sh_ref[...]
        h_ref[...] = h.reshape(h_ref.shape).astype(BF16)

    first_col = pl.program_id(1) == 0
    if head_blocks:
        in_head = pl.program_id(0) < head_blocks
        pl.when(first_col & in_head)(lambda: modulate(xh_ref))
        pl.when(first_col & jnp.logical_not(in_head))(lambda: modulate(x_ref))
    else:
        pl.when(first_col)(lambda: modulate(x_ref))

    if not conv:
        n1 = o_ref.shape[-1]
        acc = jnp.dot(h_ref[...], w_ref[:, 0:n1], preferred_element_type=F32)
        o_ref[...] = acc.reshape(o_ref.shape).astype(o_ref.dtype)
        if tail:
            acc2 = jnp.dot(h_ref[...], w_ref[:, n1:], preferred_element_type=F32)
            o2_ref[...] = acc2.reshape(o2_ref.shape).astype(o2_ref.dtype)
        return

    def taps(a, cw, cb, n, head, tail):
        r = lax.broadcasted_iota(jnp.int32, a.shape, 0)
        t0, t1, t3 = pltpu.roll(a, 2, 0), pltpu.roll(a, 1, 0), pltpu.roll(a, n - 1, 0)
        if head:
            t0, t1 = jnp.where(r >= 2, t0, 0.0), jnp.where(r >= 1, t1, 0.0)
        if tail:
            t3 = jnp.where(r < n - 1, t3, 0.0)
        return _silu(a * cw[2:3, :] + cb + t0 * cw[0:1, :] + t1 * cw[1:2, :] + t3 * cw[3:4, :])

    edge = 32
    half = edge // 2
    for lo in range(0, w_ref.shape[1], CONV_SUBTILE):
        cols = slice(lo, lo + CONV_SUBTILE)
        acc = jnp.dot(h_ref[...], w_ref[:, cols], preferred_element_type=F32)
        cw, cb = cw_ref[:, cols], cb_ref[:, cols]
        o_ref[0, :, cols] = taps(acc, cw, cb, rows, False, False).astype(o_ref.dtype)
        o_ref[0, 0:half, cols] = taps(acc[0:edge], cw, cb, edge, True, False)[0:half].astype(o_ref.dtype)
        o_ref[0, rows - half:rows, cols] = taps(acc[rows - edge:rows], cw, cb, edge, False, True)[half:].astype(
            o_ref.dtype)


def _inproj(x3, sh3, sc3, w, tile, tn, sel=None, conv_w=None, conv_b=None, out_dtype=F32, tail=None, x3_head=None):
    gdim, rdim, d = x3.shape
    g, r = tile
    n = w.shape[1]
    rows = g * r
    nblk_r = rdim // r
    nb0 = 0 if x3_head is None else x3_head.shape[0] // g
    nblk_g = gdim // g + nb0
    if sel is None:
        sel = lambda i: i
    conv = conv_w is not None
    mg, mr = sh3.shape[1], sh3.shape[2]
    in_specs = [pl.BlockSpec((g, r, d), lambda i, j: (jnp.maximum(i // nblk_r - nb0, 0), i % nblk_r, 0)),
                pl.BlockSpec((1, mg, mr), lambda i, j: (sel(i // nblk_r), 0, 0)),
                pl.BlockSpec((1, mg, mr), lambda i, j: (sel(i // nblk_r), 0, 0)),
                pl.BlockSpec((d, tn), lambda i, j: (0, j))]
    args = [x3, sh3, sc3, w]
    if nb0:
        assert nblk_r == 1 and x3_head.shape[1:] == x3.shape[1:]
        in_specs.append(pl.BlockSpec((g, r, d), lambda i, j: (jnp.minimum(i, nb0 - 1), 0, 0)))
        args.append(x3_head)
    if conv:
        assert r == rdim and g == 1
        in_specs += [pl.BlockSpec((CONV_W, tn), lambda i, j: (0, j)),
                     pl.BlockSpec((1, tn), lambda i, j: (0, j))]
        args += [conv_w, conv_b.reshape(1, n)]
    out_g = nblk_g * g
    out_idx = lambda i, j: (i // nblk_r, i % nblk_r, j)
    out_shape = jax.ShapeDtypeStruct((out_g, rdim, n), out_dtype)
    out_specs = pl.BlockSpec((g, r, tn), out_idx)
    if tail is not None:
        n2, dtype2 = tail
        assert tn == n and not conv
        out_shape = [jax.ShapeDtypeStruct((out_g, rdim, n - n2), out_dtype),
                     jax.ShapeDtypeStruct((out_g, rdim, n2), dtype2)]
        out_specs = [pl.BlockSpec((g, r, n - n2), out_idx), pl.BlockSpec((g, r, n2), out_idx)]
    return pl.pallas_call(
        functools.partial(_inproj_kernel, conv=conv, tail=tail is not None, head_blocks=nb0),
        out_shape=out_shape,
        grid=(nblk_g * nblk_r, n // tn),
        in_specs=in_specs,
        out_specs=out_specs,
        scratch_shapes=[pltpu.VMEM((rows, d), BF16)],
        compiler_params=_cparams("arbitrary", "arbitrary"),
        name="inproj_conv" if conv else "inproj",
    )(*args)


def _lru_gates(win_ref, w_ref, ba_ref, bx_ref, lam_ref, cw_ref, cb_ref, a_ref, b_ref, tt):
    bsz, c = win_ref.shape[1], win_ref.shape[2]
    cw = cw_ref[...]
    u = cb_ref[...].reshape(1, 1, c) + sum(win_ref[pl.ds(j, tt)] * cw[j:j + 1, :].reshape(1, 1, c)
                                           for j in range(CONV_W))
    u2 = u.reshape(tt * bsz, c)
    ub = u2.astype(BF16)
    sp = _softplus(-lam_ref[...])
    pack = w_ref.shape[1]
    for j in range(c // pack):
        lo = j * pack
        pre = jnp.dot(ub[:, lo:lo + pack], w_ref[j], preferred_element_type=F32)
        r = _sigmoid(pre[:, :pack] + ba_ref[:, lo:lo + pack])
        i = _sigmoid(pre[:, pack:] + bx_ref[:, lo:lo + pack])
        log_a = (-LRU_C) * r * sp[:, lo:lo + pack]
        a = jnp.exp(log_a)
        bt = jnp.sqrt(1.0 - jnp.exp(2.0 * log_a)) * (i * u2[:, lo:lo + pack])
        a_ref[:, :, lo:lo + pack] = a.reshape(tt, bsz, pack)
        b_ref[:, :, lo:lo + pack] = bt.reshape(tt, bsz, pack)


def _lru_fill_window(win_ref, x_ref, prev_ref, next_ref, at_start, at_end, tt):
    zero2 = jnp.zeros(prev_ref.shape, F32)
    win_ref[pl.ds(0, 2)] = jnp.where(at_start, zero2, prev_ref[...])
    win_ref[pl.ds(2, tt)] = x_ref[...]
    win_ref[pl.ds(tt + 2, 1)] = jnp.where(at_end, jnp.zeros(next_ref.shape, F32), next_ref[...])


def _lru_bwd_kernel(x_ref, prev_ref, next_ref, w_ref, ba_ref, bx_ref, lam_ref, cw_ref, cb_ref,
                    hb_ref, win_ref, a_ref, b_ref, h_ref, *, tt, ncb, nlb):
    i = pl.program_id(0)
    blk = jnp.where(i < ncb, ncb - 1 - i, ncb + nlb - 1 - (i - ncb))
    at_start = (blk == 0) | (blk == ncb)
    at_end = (blk == ncb - 1) | (blk == ncb + nlb - 1)

    @pl.when(i == 0)
    def _():
        h_ref[...] = jnp.zeros(h_ref.shape, F32)

    _lru_fill_window(win_ref, x_ref, prev_ref, next_ref, at_start, at_end, tt)
    _lru_gates(win_ref, w_ref, ba_ref, bx_ref, lam_ref, cw_ref, cb_ref, a_ref, b_ref, tt)

    def step(k, h):
        t = tt - 1 - k
        h = a_ref[t] * h + b_ref[t]
        a_ref[t] = h
        return h

    h_ref[...] = lax.fori_loop(0, tt, step, h_ref[...])

    @pl.when(i >= ncb)
    def _():
        hb_ref[...] = a_ref[...].astype(hb_ref.dtype)


def _lru_fwd_kernel(x_ref, prev_ref, next_ref, w_ref, ba_ref, bx_ref, lam_ref, cw_ref, cb_ref,
                    hb_ref, gr_ref, gm_ref, bm_ref, wbr_ref,
                    o_ref, win_ref, a_ref, b_ref, h_ref, *, tt, ncb, nlb):
    i = pl.program_id(0)
    at_start = (i == 0) | (i == ncb)
    at_end = (i == ncb - 1) | (i == ncb + nlb - 1)

    @pl.when(i == 0)
    def _():
        h_ref[...] = jnp.zeros(h_ref.shape, F32)

    _lru_fill_window(win_ref, x_ref, prev_ref, next_ref, at_start, at_end, tt)
    _lru_gates(win_ref, w_ref, ba_ref, bx_ref, lam_ref, cw_ref, cb_ref, a_ref, b_ref, tt)

    def step(t, h):
        h = a_ref[t] * h + b_ref[t]
        a_ref[t] = h
        return h

    h_ref[...] = lax.fori_loop(0, tt, step, h_ref[...])

    @pl.when(i >= ncb)
    def _():
        bsz, c = h_ref.shape
        rows = tt * bsz
        a_lat = (a_ref[...] + hb_ref[...].astype(F32)) * jax.nn.gelu(gr_ref[...].astype(F32), approximate=True)
        proj = jnp.dot(a_lat.reshape(rows, c).astype(BF16), wbr_ref[...], preferred_element_type=F32)
        gate = _sigmoid(gm_ref[...].astype(F32).reshape(rows, -1) + bm_ref[...])
        o_ref[...] = (gate * proj).reshape(o_ref.shape).astype(o_ref.dtype)


def _lru_specs(tt, bsz, c, blk_of, ttot):
    half = tt // 2
    return [pl.BlockSpec((tt, bsz, c), lambda i: (blk_of(i), 0, 0)),
            pl.BlockSpec((2, bsz, c), lambda i: (jnp.maximum(blk_of(i) * half - 1, 0), 0, 0)),
            pl.BlockSpec((1, bsz, c), lambda i: (jnp.minimum(blk_of(i) * tt + tt, ttot - 1), 0, 0))]


def _lru(xr_all, grgm, p_f, p_b, conv_w, conv_b, b_merge_lru, w_br, t_ctx, tt):
    ttot, bsz, c = xr_all.shape
    ncb, nlb = t_ctx // tt, (ttot - t_ctx) // tt
    nblk = ncb + nlb
    d = w_br.shape[1]
    const2 = lambda i: (0, 0)
    const3 = lambda i: (0, 0, 0)
    par_specs = [pl.BlockSpec(p_f[0].shape, const3), pl.BlockSpec((1, c), const2), pl.BlockSpec((1, c), const2),
                 pl.BlockSpec((1, c), const2), pl.BlockSpec((CONV_W, c), const2), pl.BlockSpec((1, c), const2)]
    scratch = [pltpu.VMEM((tt + 3, bsz, c), F32), pltpu.VMEM((tt, bsz, c), F32),
               pltpu.VMEM((tt, bsz, c), F32), pltpu.VMEM((bsz, c), F32)]

    bwd_blk = lambda i: jnp.where(i < ncb, ncb - 1 - i, ncb + nlb - 1 - (i - ncb))
    hb = pl.pallas_call(
        functools.partial(_lru_bwd_kernel, tt=tt, ncb=ncb, nlb=nlb),
        out_shape=jax.ShapeDtypeStruct((nlb * tt, bsz, c), BF16),
        grid=(nblk,),
        in_specs=_lru_specs(tt, bsz, c, bwd_blk, ttot) + par_specs,
        out_specs=pl.BlockSpec((tt, bsz, c), lambda i: (jnp.where(i < ncb, nlb - 1, nblk - 1 - i), 0, 0)),
        scratch_shapes=scratch,
        compiler_params=_cparams("arbitrary"),
        name="lru_bwd",
    )(xr_all, xr_all, xr_all, *p_b, conv_w, conv_b.reshape(1, c))

    assert c == d
    lat = lambda i: (jnp.maximum(i - ncb, 0), 0, 0)
    lat1 = lambda i: (jnp.maximum(i - ncb, 0), 0, 1)
    return pl.pallas_call(
        functools.partial(_lru_fwd_kernel, tt=tt, ncb=ncb, nlb=nlb),
        out_shape=jax.ShapeDtypeStruct((nlb * tt, bsz, d), BF16),
        grid=(nblk,),
        in_specs=_lru_specs(tt, bsz, c, lambda i: i, ttot) + par_specs + [
            pl.BlockSpec((tt, bsz, c), lat), pl.BlockSpec((tt, bsz, c), lat), pl.BlockSpec((tt, bsz, d), lat1),
            pl.BlockSpec((1, d), const2), pl.BlockSpec((c, d), const2)],
        out_specs=pl.BlockSpec((tt, bsz, d), lat),
        scratch_shapes=scratch,
        compiler_params=_cparams("arbitrary"),
        name="lru_fwd",
    )(xr_all, xr_all, xr_all, *p_f, conv_w, conv_b.reshape(1, c), hb, grgm, grgm, b_merge_lru.reshape(1, d), w_br)


def _lru_params(wa, ba, wx, bx, lam, pack):
    nb, bw, _ = wa.shape
    per = pack // bw
    c = nb * bw

    def bd(w):
        w4 = w.reshape(nb // per, per, bw, bw)
        eye = jnp.eye(per, dtype=w.dtype)
        return jnp.einsum('gpde,pq->gpdqe', w4, eye).reshape(nb // per, pack, pack)

    w = jnp.concatenate([bd(wa), bd(wx)], axis=-1).astype(BF16)
    return w, ba.reshape(1, c), bx.reshape(1, c), lam.reshape(1, c)


def _hi_lo(v):
    hi = v.astype(BF16)
    lo = (v - hi.astype(F32)).astype(BF16)
    return jnp.concatenate([hi, lo], axis=-1)


def _ssd_kernel(xbc_ref, dt_ref, dtb_ref, alog_ref, e_ref, dsk_ref, h0_ref, *outs,
                reverse, with_y, add_skip, lane0, inner):
    if with_y:
        y_ref, hfin_ref, st_ref = outs
    else:
        hfin_ref, st_ref = outs
    i = pl.program_id(1)
    q = SSD_CHUNK
    n = SSD_STATE
    gw = inner // SSD_GROUPS
    hpg = gw // SSD_HEAD_DIM

    @pl.when(i == 0)
    def _():
        st_ref[...] = h0_ref[0]

    dt = _softplus(dt_ref[0] + dtb_ref[...])
    da = dt * (-jnp.exp(alog_ref[...]))
    ri = lax.broadcasted_iota(jnp.int32, (q, q), 0)
    ci = lax.broadcasted_iota(jnp.int32, (q, q), 1)
    tri = (ri <= ci) if reverse else (ri >= ci)
    acum = jnp.dot(tri.astype(F32), da, precision=HIGHEST, preferred_element_type=F32)
    a_tot = jnp.sum(da, axis=0, keepdims=True)
    w_state = jnp.exp(a_tot - acum) * dt
    e2 = e_ref[...]
    ws_x = jnp.dot(w_state.astype(BF16), e2[0:LANES, :], preferred_element_type=F32)
    dec_x = jnp.dot(_hi_lo(jnp.broadcast_to(jnp.exp(a_tot), (8, LANES))), e2,
                    preferred_element_type=F32)[0:1, :]
    if with_y:
        eac_x = jnp.dot(_hi_lo(jnp.exp(acum)), e2, preferred_element_type=F32)
        acum_t = acum.T
        dt_t = dt.T
        rb = lax.broadcasted_iota(jnp.int32, (hpg * q, gw), 0) // q
        lb = lax.broadcasted_iota(jnp.int32, (hpg * q, gw), 1) // SSD_HEAD_DIM
        bd_mask = rb == lb

    for g in range(SSD_GROUPS):
        lo = g * gw
        xg = xbc_ref[0, :, lo:lo + gw]
        bg = xbc_ref[0, :, inner + g * n:inner + (g + 1) * n]
        cg = xbc_ref[0, :, inner + (SSD_GROUPS + g) * n:inner + (SSD_GROUPS + g + 1) * n]
        xgf = xg.astype(F32)
        xw = (xgf * ws_x[:, lo:lo + gw]).astype(BF16)
        st = st_ref[g]
        upd = lax.dot_general(bg, xw, (((0,), (0,)), ((), ())), preferred_element_type=F32)
        st_ref[g] = dec_x[:, lo:lo + gw] * st + upd
        if with_y:
            y_off = jnp.dot(cg, st.astype(BF16), preferred_element_type=F32) * eac_x[:, lo:lo + gw]
            cb = lax.dot_general(cg, bg, (((1,), (1,)), ((), ())), preferred_element_type=F32)
            ls = []
            for r in range(hpg):
                lane = lane0 + g * hpg + r
                seg = acum[:, lane:lane + 1] - acum_t[lane:lane + 1, :]
                l_h = cb * jnp.exp(jnp.where(tri, seg, NEG_BIG)) * dt_t[lane:lane + 1, :]
                ls.append(l_h.astype(BF16))
            lcat = jnp.concatenate(ls, axis=1)
            xbd = jnp.where(bd_mask, jnp.concatenate([xg] * hpg, axis=0), jnp.zeros((), BF16))
            y = y_off + jnp.dot(lcat, xbd, preferred_element_type=F32)
            if add_skip:
                y = y + dsk_ref[:, lo:lo + gw] * xgf
            y_ref[0, :, lo:lo + gw] = y.astype(y_ref.dtype)

    @pl.when(i == pl.num_programs(1) - 1)
    def _():
        hfin_ref[0] = st_ref[...]


def _ssd(xbc, dt_raw, dtb, alog, e2, dsk, h0, *, reverse, with_y, add_skip, lane0):
    bsz, s, width = xbc.shape
    inner = e2.shape[1]
    nc = s // SSD_CHUNK
    gw = inner // SSD_GROUPS
    cidx = (lambda b, i: (b, nc - 1 - i, 0)) if reverse else (lambda b, i: (b, i, 0))
    const2 = lambda b, i: (0, 0)
    st_spec = pl.BlockSpec((1, SSD_GROUPS, SSD_STATE, gw), lambda b, i: (b, 0, 0, 0))
    st_shape = jax.ShapeDtypeStruct((bsz, SSD_GROUPS, SSD_STATE, gw), F32)
    out_shape, out_specs = [st_shape], [st_spec]
    if with_y:
        out_shape = [jax.ShapeDtypeStruct((bsz, s, inner), BF16)] + out_shape
        out_specs = [pl.BlockSpec((1, SSD_CHUNK, inner), cidx)] + out_specs
    return pl.pallas_call(
        functools.partial(_ssd_kernel, reverse=reverse, with_y=with_y, add_skip=add_skip, lane0=lane0, inner=inner),
        out_shape=out_shape,
        grid=(bsz, nc),
        in_specs=[pl.BlockSpec((1, SSD_CHUNK, width), cidx),
                  pl.BlockSpec((1, SSD_CHUNK, LANES), cidx),
                  pl.BlockSpec((1, LANES), const2), pl.BlockSpec((1, LANES), const2),
                  pl.BlockSpec(e2.shape, const2), pl.BlockSpec((1, inner), const2), st_spec],
        out_specs=out_specs,
        scratch_shapes=[pltpu.VMEM((SSD_GROUPS, SSD_STATE, gw), F32)],
        compiler_params=_cparams("arbitrary", "arbitrary"),
        name="ssd_y" if with_y else "ssd_state",
    )(xbc, dt_raw, dtb, alog, e2, dsk, h0)


def _ssdbr_kernel(yf_ref, yb_ref, z_ref, gm_ref, nw_ref, bm_ref, gsum_ref, gexp_ref, w_ref, o_ref, *, group_w):
    y = (yf_ref[0].astype(F32) + yb_ref[0].astype(F32)) * _silu(z_ref[0].astype(F32))
    ms = jnp.dot((y * y).astype(BF16), gsum_ref[...], preferred_element_type=F32) * (1.0 / group_w)
    rs = lax.rsqrt(ms + RMS_EPS)
    rs_x = jnp.dot(_hi_lo(rs), gexp_ref[...], preferred_element_type=F32)
    yn = (y * rs_x * nw_ref[...]).astype(BF16)
    proj = jnp.dot(yn, w_ref[...], preferred_element_type=F32)
    o_ref[0] = (_sigmoid(gm_ref[0].astype(F32) + bm_ref[...]) * proj).astype(o_ref.dtype)


def _ssdbr(y_f, y_b, zgm, norm_w, b_merge_ssd, gsum, gexp, w_br, rows):
    bsz, s, inner = y_f.shape
    d = w_br.shape[1]
    assert inner % d == 0
    tok = lambda b, i: (b, i, 0)
    const2 = lambda b, i: (0, 0)
    return pl.pallas_call(
        functools.partial(_ssdbr_kernel, group_w=inner // SSD_GROUPS),
        out_shape=jax.ShapeDtypeStruct((bsz, s, d), BF16),
        grid=(bsz, s // rows),
        in_specs=[pl.BlockSpec((1, rows, inner), tok), pl.BlockSpec((1, rows, inner), tok),
                  pl.BlockSpec((1, rows, inner), tok),
                  pl.BlockSpec((1, rows, d), lambda b, i: (b, i, inner // d)),
                  pl.BlockSpec((1, inner), const2), pl.BlockSpec((1, d), const2),
                  pl.BlockSpec(gsum.shape, const2), pl.BlockSpec(gexp.shape, const2),
                  pl.BlockSpec((inner, d), const2)],
        out_specs=pl.BlockSpec((1, rows, d), tok),
        compiler_params=_cparams("arbitrary", "arbitrary"),
        name="ssdbr",
    )(y_f, y_b, zgm, zgm, norm_w.reshape(1, inner), b_merge_ssd.reshape(1, d), gsum, gexp, w_br)


def _pack_bf16_pair(lo, hi):
    lo_bits = lax.bitcast_convert_type(lo.astype(BF16).astype(F32), jnp.uint32)
    hi_bits = lax.bitcast_convert_type(hi.astype(BF16).astype(F32), jnp.uint32)
    return (lo_bits >> 16) | hi_bits


def _unpack_bf16_pair(p):
    lo = lax.bitcast_convert_type(p << 16, F32).astype(BF16)
    hi = lax.bitcast_convert_type(p & jnp.uint32(0xFFFF0000), F32).astype(BF16)
    return jnp.concatenate([lo, hi], axis=1)


def _out_kernel(ul_ref, us_ref, x_ref, g1_ref, sh2_ref, sc2_ref, wout_ref, lg_ref, lb_ref, rw_ref, rb_ref,
                x1_ref, hp_ref, gates_ref, sel_ref, *, alpha):
    u = (ul_ref[0].astype(F32) + us_ref[0].astype(F32)).astype(BF16)
    mix = jnp.dot(u, wout_ref[...], preferred_element_type=F32)
    x1 = _ln_rows(alpha * x_ref[0] + g1_ref[0] * mix) * lg_ref[...] + lb_ref[...]
    x1_ref[0] = x1
    h2 = _ln_rows(x1) * (1.0 + sc2_ref[0]) + sh2_ref[0]
    half = h2.shape[1] // 2
    hp_ref[0] = _pack_bf16_pair(h2[:, :half], h2[:, half:])
    h_hi = h2.astype(BF16)
    h_lo = (h2 - h_hi.astype(F32)).astype(BF16)
    logits = jnp.dot(jnp.concatenate([h_hi, h_lo, h_hi], axis=1), rw_ref[...],
                     preferred_element_type=F32) + rb_ref[...]
    lane = lax.broadcasted_iota(jnp.int32, logits.shape, 1)
    work = logits
    sel = jnp.zeros(logits.shape, jnp.bool_)
    top = None
    for k in range(MOE_TOP_K):
        m = jnp.max(work, axis=1, keepdims=True)
        if k == 0:
            top = m
        idx = jnp.min(jnp.where(work == m, lane, LANES), axis=1, keepdims=True)
        pick = lane == idx
        sel = sel | pick
        work = jnp.where(pick, 3.0 * NEG_BIG, work)
    e = jnp.where(sel, jnp.exp(logits - top), 0.0)
    gates_ref[0] = e / jnp.sum(e, axis=1, keepdims=True)
    sel_ref[0] = sel.astype(F32)


def _out(u_l, u_s, x, g1, sh2, sc2, w_out, ln_g, ln_b, rw, rb, rows, alpha):
    bsz, t, d = x.shape
    tok = lambda b, i: (b, i, 0)
    per_b = lambda b, i: (b, 0, 0)
    const2 = lambda b, i: (0, 0)
    return pl.pallas_call(
        functools.partial(_out_kernel, alpha=alpha),
        out_shape=[jax.ShapeDtypeStruct((bsz, t, d), F32), jax.ShapeDtypeStruct((bsz, t, d // 2), jnp.uint32),
                   jax.ShapeDtypeStruct((bsz, t, LANES), F32), jax.ShapeDtypeStruct((bsz, t, LANES), F32)],
        grid=(bsz, t // rows),
        in_specs=[pl.BlockSpec((1, rows, d), tok), pl.BlockSpec((1, rows, d), tok), pl.BlockSpec((1, rows, d), tok),
                  pl.BlockSpec((1, 1, d), per_b), pl.BlockSpec((1, 1, d), per_b), pl.BlockSpec((1, 1, d), per_b),
                  pl.BlockSpec((d, d), const2), pl.BlockSpec((1, d), const2), pl.BlockSpec((1, d), const2),
                  pl.BlockSpec((3 * d, LANES), const2), pl.BlockSpec((1, LANES), const2)],
        out_specs=[pl.BlockSpec((1, rows, d), tok), pl.BlockSpec((1, rows, d // 2), tok),
                   pl.BlockSpec((1, rows, LANES), tok), pl.BlockSpec((1, rows, LANES), tok)],
        compiler_params=_cparams("arbitrary", "arbitrary"),
        name="out",
    )(u_l, u_s, x, g1, sh2, sc2, w_out, ln_g.reshape(1, d), ln_b.reshape(1, d), rw, rb)


def _rank_kernel(sel_ref, rank_ref, cnt_ref, carry_ref):
    @pl.when(pl.program_id(0) == 0)
    def _():
        carry_ref[...] = jnp.zeros(carry_ref.shape, F32)

    s = sel_ref[...]
    tt = s.shape[0]
    ri = lax.broadcasted_iota(jnp.int32, (tt, tt), 0)
    ci = lax.broadcasted_iota(jnp.int32, (tt, tt), 1)
    earlier = (ri > ci).astype(BF16)
    rank_ref[...] = jnp.dot(earlier, s.astype(BF16), preferred_element_type=F32) + carry_ref[0:1, :]
    carry_ref[...] = carry_ref[...] + jnp.sum(s, axis=0, keepdims=True)
    cnt_ref[...] = carry_ref[...]


def _rank(sel, tt):
    n = sel.shape[0]
    return pl.pallas_call(
        _rank_kernel,
        out_shape=[jax.ShapeDtypeStruct((n, LANES), F32), jax.ShapeDtypeStruct((8, LANES), F32)],
        grid=(n // tt,),
        in_specs=[pl.BlockSpec((tt, LANES), lambda i: (i, 0))],
        out_specs=[pl.BlockSpec((tt, LANES), lambda i: (i, 0)), pl.BlockSpec((8, LANES), lambda i: (0, 0))],
        scratch_shapes=[pltpu.VMEM((8, LANES), F32)],
        compiler_params=_cparams("arbitrary"),
        name="moe_rank",
    )(sel)


def _pos_kernel(sel_ref, rank_ref, gates_ref, off_ref, pos_ref, w_ref):
    avail = sel_ref[...] > 0.5
    posf = off_ref[...] + rank_ref[...]
    gates = gates_ref[...]
    lane = lax.broadcasted_iota(jnp.int32, posf.shape, 1)
    cols_p = jnp.zeros(posf.shape, F32)
    cols_w = jnp.zeros(posf.shape, F32)
    for k in range(MOE_TOP_K):
        m = jnp.min(jnp.where(avail, lane, LANES), axis=1, keepdims=True)
        pick = lane == m
        cols_p = jnp.where(lane == k, jnp.sum(jnp.where(pick, posf, 0.0), axis=1, keepdims=True), cols_p)
        cols_w = jnp.where(lane == k, jnp.sum(jnp.where(pick, gates, 0.0), axis=1, keepdims=True), cols_w)
        avail = avail & jnp.logical_not(pick)
    w_ref[...] = cols_w
    pos_ref[...] = cols_p.astype(jnp.int32)


def _pos(sel, rank, gates, off, tt):
    n = sel.shape[0]
    tok = lambda i: (i, 0)
    return pl.pallas_call(
        _pos_kernel,
        out_shape=[jax.ShapeDtypeStruct((n, LANES), jnp.int32), jax.ShapeDtypeStruct((n, LANES), F32)],
        grid=(n // tt,),
        in_specs=[pl.BlockSpec((tt, LANES), tok), pl.BlockSpec((tt, LANES), tok), pl.BlockSpec((tt, LANES), tok),
                  pl.BlockSpec((1, LANES), lambda i: (0, 0))],
        out_specs=[pl.BlockSpec((tt, LANES), tok), pl.BlockSpec((tt, LANES), tok)],
        compiler_params=_cparams("arbitrary"),
        name="moe_pos",
    )(sel, rank, gates, off)


ROW_GROUP = 8
ROW_TILE = 8


def _dispatch_kernel(pos_ref, hp_ref, xs_init_hbm, xs_hbm, sem, *, tt):
    del xs_init_hbm

    def body(i, carry):
        row0 = pl.multiple_of(i * ROW_GROUP, ROW_GROUP)
        p0 = i * (ROW_GROUP * MOE_TOP_K)
        for u in range(ROW_GROUP):
            src = hp_ref.at[pl.ds(row0 + u, 1)]
            for k in range(MOE_TOP_K):
                dst = xs_hbm.at[pl.ds(pos_ref[p0 + u * MOE_TOP_K + k], 1)]
                pltpu.make_async_copy(src, dst, sem).start(priority=k % 2)
        return carry

    lax.fori_loop(0, tt // ROW_GROUP, body, 0)
    for k in range(MOE_TOP_K):
        pltpu.make_async_copy(hp_ref, xs_hbm.at[pl.ds(0, tt)], sem).wait()


def _dispatch(pos, hp, xs_init, tt):
    n, half = hp.shape
    any_spec = pl.BlockSpec(memory_space=pl.ANY)
    return pl.pallas_call(
        functools.partial(_dispatch_kernel, tt=tt),
        out_shape=jax.ShapeDtypeStruct(xs_init.shape, xs_init.dtype),
        grid=(n // tt,),
        in_specs=[pl.BlockSpec((tt * MOE_TOP_K,), lambda i: (i,), memory_space=pltpu.SMEM),
                  pl.BlockSpec((tt, half), lambda i: (i, 0)), any_spec],
        out_specs=any_spec,
        scratch_shapes=[pltpu.SemaphoreType.DMA(())],
        input_output_aliases={2: 0},
        compiler_params=pltpu.CompilerParams(dimension_semantics=("arbitrary",), has_side_effects=True),
        name="moe_dispatch",
    )(pos, hp, xs_init)


def _expert_kernel(te_ref, nv_ref, xs_ref, w1_ref, b1_ref, w2_ref, b2_ref, ys_ref, w1b_ref, w2b_ref, *, ff):
    t = pl.program_id(0)
    e = te_ref[t]
    prev = te_ref[jnp.maximum(t - 1, 0)]

    @pl.when((t == 0) | (e != prev))
    def _():
        w1b_ref[...] = w1_ref[0].astype(BF16)
        w2b_ref[...] = w2_ref[0].astype(BF16)

    @pl.when(t < nv_ref[0])
    def _():
        xrow = _unpack_bf16_pair(xs_ref[...])
        gu = jnp.dot(xrow, w1b_ref[...], preferred_element_type=F32) + b1_ref[0]
        g = jnp.minimum(gu[:, :ff], SWIGLU_LIMIT)
        u = jnp.clip(gu[:, ff:], -SWIGLU_LIMIT, SWIGLU_LIMIT)
        act = ((u + 1.0) * g * _sigmoid(SWIGLU_ALPHA * g)).astype(BF16)
        y = jnp.dot(act, w2b_ref[...], preferred_element_type=F32) + b2_ref[0]
        for g in range(y.shape[1] // LANES):
            ys_ref[pl.ds(g, y.shape[0], stride=ROW_TILE), :] = y[:, g * LANES:(g + 1) * LANES]

    @pl.when(t >= nv_ref[0])
    def _():
        ys_ref[...] = jnp.zeros(ys_ref.shape, F32)


def _experts(tile_expert, n_valid, xs, w1, b1, w2, b2, tm):
    rows, half = xs.shape
    n_exp, d, ff2 = w1.shape
    ff = ff2 // 2
    per_e = lambda t, te, nv: (te[t], 0, 0)
    assert d == ROW_TILE * LANES
    return pl.pallas_call(
        functools.partial(_expert_kernel, ff=ff),
        out_shape=jax.ShapeDtypeStruct((rows * ROW_TILE, LANES), F32),
        grid_spec=pltpu.PrefetchScalarGridSpec(
            num_scalar_prefetch=2,
            grid=(rows // tm,),
            in_specs=[pl.BlockSpec((tm, half), lambda t, te, nv: (t, 0)),
                      pl.BlockSpec((1, d, ff2), per_e), pl.BlockSpec((1, 1, ff2), per_e),
                      pl.BlockSpec((1, ff, d), per_e), pl.BlockSpec((1, 1, d), per_e)],
            out_specs=pl.BlockSpec((tm * ROW_TILE, LANES), lambda t, te, nv: (t, 0)),
            scratch_shapes=[pltpu.VMEM((d, ff2), BF16), pltpu.VMEM((ff, d), BF16)]),
        compiler_params=_cparams("arbitrary"),
        name="moe_experts",
    )(tile_expert, n_valid, xs, w1, b1.reshape(n_exp, 1, ff2), w2, b2.reshape(n_exp, 1, d))


def _combine_kernel(pos_ref, ys_hbm, ys_flat_hbm, w_ref, x1_ref, g2_ref, lg_ref, lb_ref, o_ref, buf_ref, sem,
                    *, tt, alpha):
    def body(i, carry):
        row0 = pl.multiple_of(i * ROW_GROUP, ROW_GROUP)
        p0 = i * (ROW_GROUP * MOE_TOP_K)
        for u in range(ROW_GROUP):
            for k in range(MOE_TOP_K):
                src = ys_hbm.at[pos_ref[p0 + u * MOE_TOP_K + k]]
                dst = buf_ref.at[pl.ds(pl.multiple_of((k * tt + row0 + u) * ROW_TILE, ROW_TILE), ROW_TILE)]
                pltpu.make_async_copy(src, dst, sem).start(priority=k % 2)
        return carry

    lax.fori_loop(0, tt // ROW_GROUP, body, 0)
    span = tt * ROW_TILE
    for k in range(MOE_TOP_K):
        pltpu.make_async_copy(ys_flat_hbm.at[pl.ds(0, span)], buf_ref.at[pl.ds(k * span, span)], sem).wait()

    chunk = 64

    def rows(c, carry):
        r0 = pl.multiple_of(c * chunk, chunk)
        r = pl.ds(r0, chunk)
        w = w_ref[r, :]

        def slot_rows(k):
            first = (k * tt + r0) * ROW_TILE
            return jnp.concatenate([buf_ref[pl.ds(first + g, chunk, stride=ROW_TILE), :] for g in range(ROW_TILE)],
                                   axis=1)

        acc = sum(w[:, k:k + 1] * slot_rows(k) for k in range(MOE_TOP_K))
        o_ref[0, r, :] = _ln_rows(alpha * x1_ref[0, r, :] + g2_ref[0] * acc) * lg_ref[...] + lb_ref[...]
        return carry

    lax.fori_loop(0, tt // chunk, rows, 0)


def _combine(pos, ys, w, x1, g2, ln_g, ln_b, tt, alpha):
    bsz, t, d = x1.shape
    nt = t // tt
    return pl.pallas_call(
        functools.partial(_combine_kernel, tt=tt, alpha=alpha),
        out_shape=jax.ShapeDtypeStruct((bsz, t, d), F32),
        grid=(bsz, nt),
        in_specs=[pl.BlockSpec((tt * MOE_TOP_K,), lambda b, i: (b * nt + i,), memory_space=pltpu.SMEM),
                  pl.BlockSpec(memory_space=pl.ANY), pl.BlockSpec(memory_space=pl.ANY),
                  pl.BlockSpec((tt, LANES), lambda b, i: (b * nt + i, 0)),
                  pl.BlockSpec((1, tt, d), lambda b, i: (b, i, 0)),
                  pl.BlockSpec((1, 1, d), lambda b, i: (b, 0, 0)),
                  pl.BlockSpec((1, d), lambda b, i: (0, 0)), pl.BlockSpec((1, d), lambda b, i: (0, 0))],
        out_specs=pl.BlockSpec((1, tt, d), lambda b, i: (b, i, 0)),
        scratch_shapes=[pltpu.VMEM((MOE_TOP_K * tt * ROW_TILE, LANES), F32), pltpu.SemaphoreType.DMA(())],
        compiler_params=_cparams("arbitrary", "arbitrary"),
        name="moe_combine",
    )(pos, ys.reshape(-1, ROW_TILE, LANES), ys, w, x1, g2, ln_g.reshape(1, d), ln_b.reshape(1, d))


def _moe(hp, gates, sel, x1, g2, w1, b1, w2, b2, ln_g, ln_b, alpha, tt, tm):
    bsz, t, d = x1.shape
    n = bsz * t
    n_exp = w1.shape[0]
    sel2, gates2 = sel.reshape(n, LANES), gates.reshape(n, LANES)
    rank, cnt = _rank(sel2, tt)
    counts = cnt[0, :n_exp].astype(jnp.int32)
    tiles_per = (counts + tm - 1) // tm
    tile_end = jnp.cumsum(tiles_per)
    off = jnp.pad(((tile_end - tiles_per) * tm).astype(F32).reshape(1, n_exp), ((0, 0), (0, LANES - n_exp)))
    n_tiles = (n * MOE_TOP_K) // tm + n_exp
    n_valid = tile_end[-1:]
    tile_id = jnp.minimum(jnp.arange(n_tiles), n_valid - 1)
    tile_expert = jnp.sum((tile_end[None, :] <= tile_id[:, None]).astype(jnp.int32), axis=1)
    tile_expert = jnp.minimum(tile_expert, n_exp - 1)
    pos_lanes, w = _pos(sel2, rank, gates2, off, tt)
    pos = pos_lanes[:, :MOE_TOP_K].reshape(n * MOE_TOP_K)
    xs = _dispatch(pos, hp.reshape(n, d // 2), jnp.zeros((n_tiles * tm, d // 2), jnp.uint32), tt)
    ys = _experts(tile_expert, n_valid.astype(jnp.int32), xs, w1, b1, w2, b2, tm)
    return _combine(pos, ys, w, x1, g2, ln_g, ln_b, tt, alpha)


def _to_cols(u, grid_rows):
    b, t, c = u.shape
    return u.reshape(b, grid_rows, GRID_W, c).transpose(0, 2, 1, 3).reshape(b, t, c)


def _from_cols(u, grid_rows):
    b, t, c = u.shape
    return u.reshape(b, GRID_W, grid_rows, c).transpose(0, 2, 1, 3).reshape(b, t, c)


def kernel(x, c, ctx, c_ctx, w_ada, b_ada, w_in, b_merge, conv_lru_w, conv_lru_b, lru_wa, lru_ba, lru_wx, lru_bx, lru_lambda, conv_ssd_w, conv_ssd_b, ssd_dt_bias, ssd_a_log, ssd_d, ssd_norm_w, w_br_lru, w_br_ssd, w_out, ln1_g, ln1_b, router_w, router_b, moe_w1, moe_b1, moe_w2, moe_b2, ln2_g, ln2_b):
    depth = w_ada.shape[0]
    assert depth == 1, "single-layer stack: the context tokens only supply scan states"
    bsz, t, d = x.shape
    t_ctx = ctx.shape[1]
    alpha = (2.0 * depth) ** 0.25
    grid_rows = t // GRID_W
    d_rnn = w_br_lru.shape[1]
    inner = w_br_ssd.shape[1]
    heads = inner // SSD_HEAD_DIM
    gn = SSD_GROUPS * SSD_STATE
    col_gr = d_rnn
    col_z = col_gr + d_rnn
    col_xbc = col_z + inner
    col_dt = col_xbc + inner + 2 * gn
    col_gm = col_dt + 2 * heads
    assert 2 * heads <= LANES and t_ctx % SSD_CHUNK == 0 and t % SSD_CHUNK == 0

    pad = (-(bsz + 1)) % 8
    c_all = jnp.concatenate([c, c_ctx[None, :], jnp.zeros((pad, d), F32)], axis=0)
    mod = _ada(c_all, w_ada[0], b_ada[0])
    sh1, sc1, g1, sh2, sc2, g2 = (mod[:bsz, k * d:(k + 1) * d] for k in range(6))
    csh1, csc1 = mod[bsz:bsz + 1, 0:d], mod[bsz:bsz + 1, d:2 * d]

    w_in_b = w_in[0].astype(BF16)

    tt = 32
    zero = lambda i: 0
    x_tm, ctx_tm = jnp.transpose(x, (1, 0, 2)), jnp.transpose(ctx, (1, 0, 2))
    sh_tm = jnp.stack([jnp.broadcast_to(csh1, (bsz, d)), sh1])
    sc_tm = jnp.stack([jnp.broadcast_to(csc1, (bsz, d)), sc1])
    seg = lambda i: jnp.where(i * tt >= t_ctx, 1, 0)
    xr_all = _inproj(x_tm, sh_tm, sc_tm, w_in_b[:, :col_gr], (tt, bsz), 1024, sel=seg, x3_head=ctx_tm)
    w_grgm = jnp.concatenate([w_in_b[:, col_gr:col_z], w_in_b[:, col_gm:col_gm + d]], axis=1)
    grgm = _inproj(x_tm, sh_tm[1:], sc_tm[1:], w_grgm, (tt, bsz), w_grgm.shape[1], sel=zero, out_dtype=BF16)
    p_f = _lru_params(lru_wa[0, 0], lru_ba[0, 0], lru_wx[0, 0], lru_bx[0, 0], lru_lambda[0, 0], 256)
    p_b = _lru_params(lru_wa[0, 1], lru_ba[0, 1], lru_wx[0, 1], lru_bx[0, 1], lru_lambda[0, 1], 256)
    u_lru_tm = _lru(xr_all, grgm, p_f, p_b, conv_lru_w[0], conv_lru_b[0], b_merge[0, :d],
                    w_br_lru[0].astype(BF16), t_ctx, tt)

    x_cm = _to_cols(x, grid_rows)
    sh_b, sc_b = sh1[:, None, :], sc1[:, None, :]
    csh_b, csc_b = csh1[None], csc1[None]
    w_xbc = w_in_b[:, col_xbc:col_dt]
    w_dt = jnp.pad(w_in_b[:, col_dt:col_gm], ((0, 0), (0, LANES - 2 * heads)))
    w_zgm = jnp.concatenate([w_in_b[:, col_z:col_xbc], w_in_b[:, col_gm + d:]], axis=1)
    xbc = _inproj(x_cm, sh_b, sc_b, w_xbc, (1, t), 512, conv_w=conv_ssd_w[0], conv_b=conv_ssd_b[0], out_dtype=BF16)
    xbc_c = _inproj(ctx, csh_b, csc_b, w_xbc, (1, t_ctx), 512, sel=zero, conv_w=conv_ssd_w[0],
                    conv_b=conv_ssd_b[0], out_dtype=BF16)
    dt_raw_c = _inproj(ctx, csh_b, csc_b, w_dt, (1, t_ctx), LANES, sel=zero)
    w_zgm_dt = jnp.concatenate([w_zgm, w_dt], axis=1)
    zgm, dt_raw = _inproj(x_cm, sh_b, sc_b, w_zgm_dt, (1, 512), w_zgm_dt.shape[1], out_dtype=BF16,
                          tail=(LANES, F32))

    lane_pad = LANES - 2 * heads
    dtb = jnp.pad(ssd_dt_bias[0].reshape(1, 2 * heads), ((0, 0), (0, lane_pad)))
    alog = jnp.pad(ssd_a_log[0].reshape(1, 2 * heads), ((0, 0), (0, lane_pad)))
    dsk = jnp.repeat(ssd_d[0], SSD_HEAD_DIM).reshape(1, inner)
    head_of_lane = jnp.arange(inner) // SSD_HEAD_DIM

    def expand(lane0):
        e = (jnp.arange(LANES)[:, None] == head_of_lane[None, :] + lane0).astype(BF16)
        return jnp.concatenate([e, e], axis=0)

    gw = inner // SSD_GROUPS
    s0 = jnp.zeros((bsz, SSD_GROUPS, SSD_STATE, gw), F32)
    common = (dtb, alog)
    (st_f,) = _ssd(xbc_c, dt_raw_c, *common, expand(0), dsk, s0, reverse=False, with_y=False, add_skip=False, lane0=0)
    (st_b,) = _ssd(xbc_c, dt_raw_c, *common, expand(heads), dsk, s0, reverse=True, with_y=False, add_skip=False,
                   lane0=heads)
    y_f, _ = _ssd(xbc, dt_raw, *common, expand(0), dsk, st_f, reverse=False, with_y=True, add_skip=True, lane0=0)
    y_b, _ = _ssd(xbc, dt_raw, *common, expand(heads), dsk, st_b, reverse=True, with_y=True, add_skip=False,
                  lane0=heads)

    group_of_lane = jnp.arange(inner) // gw
    gsum = (group_of_lane[:, None] == jnp.arange(LANES)[None, :]).astype(BF16)
    gexp = jnp.concatenate([gsum.T, gsum.T], axis=0)
    u_ssd_cm = _ssdbr(y_f, y_b, zgm, ssd_norm_w[0], b_merge[0, d:], gsum, gexp, w_br_ssd[0].astype(BF16), 256)

    u_lru = jnp.transpose(u_lru_tm, (1, 0, 2))
    u_ssd = _from_cols(u_ssd_cm, grid_rows)
    n_exp = router_w.shape[2]
    rw = jnp.pad(router_w[0], ((0, 0), (0, LANES - n_exp)))
    rw_hi = rw.astype(BF16)
    rw_lo = (rw - rw_hi.astype(F32)).astype(BF16)
    rw = jnp.concatenate([rw_hi, rw_hi, rw_lo], axis=0)
    rb = jnp.pad(router_b[0].reshape(1, n_exp), ((0, 0), (0, LANES - n_exp)), constant_values=NEG_BIG)
    x1, hp, gates, sel = _out(u_lru, u_ssd, x, g1[:, None, :], sh2[:, None, :], sc2[:, None, :],
                              w_out[0].astype(BF16), ln1_g[0], ln1_b[0], rw, rb, 512, alpha)

    return _moe(hp, gates, sel, x1, g2[:, None, :], moe_w1[0], moe_b1[0], moe_w2[0], moe_b2[0],
                ln2_g[0], ln2_b[0], alpha, 1024, 512)
```

```python
import functools

import jax
import jax.numpy as jnp
from jax import lax
from jax.experimental import pallas as pl
from jax.experimental.pallas import tpu as pltpu

F32 = jnp.float32
BF16 = jnp.bfloat16
HIGHEST = lax.Precision.HIGHEST

GRID_W = 64
LRU_BLOCK_W = 64
LRU_C = 8.0
CONV_W = 4
SSD_HEAD_DIM = 64
SSD_GROUPS = 8
SSD_STATE = 128
SSD_CHUNK = 128
MOE_TOP_K = 4
SWIGLU_LIMIT = 7.0
SWIGLU_ALPHA = 1.702
LN_EPS = 1e-5
RMS_EPS = 1e-5
LANES = 128
NEG_BIG = -1e30
CONV_SUBTILE = 256
VMEM_LIMIT = 56 * 1024 * 1024


def _cparams(*sem):
    return pltpu.CompilerParams(dimension_semantics=sem, vmem_limit_bytes=VMEM_LIMIT)


def _ln_rows(x):
    mu = jnp.mean(x, axis=-1, keepdims=True)
    xc = x - mu
    var = jnp.mean(xc * xc, axis=-1, keepdims=True)
    return xc * lax.rsqrt(var + LN_EPS)


def _sigmoid(x):
    return 0.5 * (jnp.tanh(0.5 * x) + 1.0)


def _silu(x):
    h = 0.5 * x
    return h + h * jnp.tanh(h)


def _softplus(x):
    return jnp.maximum(x, 0.0) + jnp.log(1.0 + jnp.exp(-jnp.abs(x)))


def _ada_kernel(c_ref, w_ref, b_ref, o_ref):
    c = c_ref[...]
    o_ref[...] = jnp.dot(_silu(c), w_ref[...], precision=HIGHEST, preferred_element_type=F32) + b_ref[...]


def _ada(c_all, w, b):
    m, d = c_all.shape
    n = w.shape[1]
    tn = 1024
    return pl.pallas_call(
        _ada_kernel,
        out_shape=jax.ShapeDtypeStruct((m, n), F32),
        grid=(n // tn,),
        in_specs=[pl.BlockSpec((m, d), lambda j: (0, 0)),
                  pl.BlockSpec((d, tn), lambda j: (0, j)),
                  pl.BlockSpec((1, tn), lambda j: (0, j))],
        out_specs=pl.BlockSpec((m, tn), lambda j: (0, j)),
        compiler_params=_cparams("arbitrary"),
        name="ada",
    )(c_all, w, b.reshape(1, n))


def _inproj_kernel(x_ref, sh_ref, sc_ref, w_ref, *rest, conv, tail, head_blocks):
    o2_ref = None
    if head_blocks:
        xh_ref, rest = rest[0], rest[1:]
    if conv:
        cw_ref, cb_ref, o_ref, h_ref = rest
    elif tail:
        o_ref, o2_ref, h_ref = rest
    else:
        o_ref, h_ref = rest
    rows = h_ref.shape[0]

    def modulate(src_ref):
        h = _ln_rows(src_ref[...]) * (1.0 + sc_ref[...]) + sh_ref[...]
        h_ref[...] = h.reshape(h_ref.shape).astype(BF16)

    first_col = pl.program_id(1) == 0
    if head_blocks:
        in_head = pl.program_id(0) < head_blocks
        pl.when(first_col & in_head)(lambda: modulate(xh_ref))
        pl.when(first_col & jnp.logical_not(in_head))(lambda: modulate(x_ref))
    else:
        pl.when(first_col)(lambda: modulate(x_ref))

    if not conv:
        n1 = o_ref.shape[-1]
        acc = jnp.dot(h_ref[...], w_ref[:, 0:n1], preferred_element_type=F32)
        o_ref[...] = acc.reshape(o_ref.shape).astype(o_ref.dtype)
        if tail:
            acc2 = jnp.dot(h_ref[...], w_ref[:, n1:], preferred_element_type=F32)
            o2_ref[...] = acc2.reshape(o2_ref.shape).astype(o2_ref.dtype)
        return

    def taps(a, cw, cb, n, head, tail):
        r = lax.broadcasted_iota(jnp.int32, a.shape, 0)
        t0, t1, t3 = pltpu.roll(a, 2, 0), pltpu.roll(a, 1, 0), pltpu.roll(a, n - 1, 0)
        if head:
            t0, t1 = jnp.where(r >= 2, t0, 0.0), jnp.where(r >= 1, t1, 0.0)
        if tail:
            t3 = jnp.where(r < n - 1, t3, 0.0)
        return _silu(a * cw[2:3, :] + cb + t0 * cw[0:1, :] + t1 * cw[1:2, :] + t3 * cw[3:4, :])

    edge = 32
    half = edge // 2
    for lo in range(0, w_ref.shape[1], CONV_SUBTILE):
        cols = slice(lo, lo + CONV_SUBTILE)
        acc = jnp.dot(h_ref[...], w_ref[:, cols], preferred_element_type=F32)
        cw, cb = cw_ref[:, cols], cb_ref[:, cols]
        o_ref[0, :, cols] = taps(acc, cw, cb, rows, False, False).astype(o_ref.dtype)
        o_ref[0, 0:half, cols] = taps(acc[0:edge], cw, cb, edge, True, False)[0:half].astype(o_ref.dtype)
        o_ref[0, rows - half:rows, cols] = taps(acc[rows - edge:rows], cw, cb, edge, False, True)[half:].astype(
            o_ref.dtype)


def _inproj(x3, sh3, sc3, w, tile, tn, sel=None, conv_w=None, conv_b=None, out_dtype=F32, tail=None, x3_head=None):
    gdim, rdim, d = x3.shape
    g, r = tile
    n = w.shape[1]
    rows = g * r
    nblk_r = rdim // r
    nb0 = 0 if x3_head is None else x3_head.shape[0] // g
    nblk_g = gdim // g + nb0
    if sel is None:
        sel = lambda i: i
    conv = conv_w is not None
    mg, mr = sh3.shape[1], sh3.shape[2]
    in_specs = [pl.BlockSpec((g, r, d), lambda i, j: (jnp.maximum(i // nblk_r - nb0, 0), i % nblk_r, 0)),
                pl.BlockSpec((1, mg, mr), lambda i, j: (sel(i // nblk_r), 0, 0)),
                pl.BlockSpec((1, mg, mr), lambda i, j: (sel(i // nblk_r), 0, 0)),
                pl.BlockSpec((d, tn), lambda i, j: (0, j))]
    args = [x3, sh3, sc3, w]
    if nb0:
        assert nblk_r == 1 and x3_head.shape[1:] == x3.shape[1:]
        in_specs.append(pl.BlockSpec((g, r, d), lambda i, j: (jnp.minimum(i, nb0 - 1), 0, 0)))
        args.append(x3_head)
    if conv:
        assert r == rdim and g == 1
        in_specs += [pl.BlockSpec((CONV_W, tn), lambda i, j: (0, j)),
                     pl.BlockSpec((1, tn), lambda i, j: (0, j))]
        args += [conv_w, conv_b.reshape(1, n)]
    out_g = nblk_g * g
    out_idx = lambda i, j: (i // nblk_r, i % nblk_r, j)
    out_shape = jax.ShapeDtypeStruct((out_g, rdim, n), out_dtype)
    out_specs = pl.BlockSpec((g, r, tn), out_idx)
    if tail is not None:
        n2, dtype2 = tail
        assert tn == n and not conv
        out_shape = [jax.ShapeDtypeStruct((out_g, rdim, n - n2), out_dtype),
                     jax.ShapeDtypeStruct((out_g, rdim, n2), dtype2)]
        out_specs = [pl.BlockSpec((g, r, n - n2), out_idx), pl.BlockSpec((g, r, n2), out_idx)]
    return pl.pallas_call(
        functools.partial(_inproj_kernel, conv=conv, tail=tail is not None, head_blocks=nb0),
        out_shape=out_shape,
        grid=(nblk_g * nblk_r, n // tn),
        in_specs=in_specs,
        out_specs=out_specs,
        scratch_shapes=[pltpu.VMEM((rows, d), BF16)],
        compiler_params=_cparams("arbitrary", "arbitrary"),
        name="inproj_conv" if conv else "inproj",
    )(*args)


def _lru_gates(win_ref, w_ref, ba_ref, bx_ref, lam_ref, cw_ref, cb_ref, a_ref, b_ref, tt):
    bsz, c = win_ref.shape[1], win_ref.shape[2]
    cw = cw_ref[...]
    u = cb_ref[...].reshape(1, 1, c) + sum(win_ref[pl.ds(j, tt)] * cw[j:j + 1, :].reshape(1, 1, c)
                                           for j in range(CONV_W))
    u2 = u.reshape(tt * bsz, c)
    ub = u2.astype(BF16)
    sp = _softplus(-lam_ref[...])
    pack = w_ref.shape[1]
    for j in range(c // pack):
        lo = j * pack
        pre = jnp.dot(ub[:, lo:lo + pack], w_ref[j], preferred_element_type=F32)
        r = _sigmoid(pre[:, :pack] + ba_ref[:, lo:lo + pack])
        i = _sigmoid(pre[:, pack:] + bx_ref[:, lo:lo + pack])
        log_a = (-LRU_C) * r * sp[:, lo:lo + pack]
        a = jnp.exp(log_a)
        bt = jnp.sqrt(1.0 - jnp.exp(2.0 * log_a)) * (i * u2[:, lo:lo + pack])
        a_ref[:, :, lo:lo + pack] = a.reshape(tt, bsz, pack)
        b_ref[:, :, lo:lo + pack] = bt.reshape(tt, bsz, pack)


def _lru_fill_window(win_ref, x_ref, prev_ref, next_ref, at_start, at_end, tt):
    zero2 = jnp.zeros(prev_ref.shape, F32)
    win_ref[pl.ds(0, 2)] = jnp.where(at_start, zero2, prev_ref[...])
    win_ref[pl.ds(2, tt)] = x_ref[...]
    win_ref[pl.ds(tt + 2, 1)] = jnp.where(at_end, jnp.zeros(next_ref.shape, F32), next_ref[...])


def _lru_bwd_kernel(x_ref, prev_ref, next_ref, w_ref, ba_ref, bx_ref, lam_ref, cw_ref, cb_ref,
                    hb_ref, win_ref, a_ref, b_ref, h_ref, *, tt, ncb, nlb):
    i = pl.program_id(0)
    blk = jnp.where(i < ncb, ncb - 1 - i, ncb + nlb - 1 - (i - ncb))
    at_start = (blk == 0) | (blk == ncb)
    at_end = (blk == ncb - 1) | (blk == ncb + nlb - 1)

    @pl.when(i == 0)
    def _():
        h_ref[...] = jnp.zeros(h_ref.shape, F32)

    _lru_fill_window(win_ref, x_ref, prev_ref, next_ref, at_start, at_end, tt)
    _lru_gates(win_ref, w_ref, ba_ref, bx_ref, lam_ref, cw_ref, cb_ref, a_ref, b_ref, tt)

    def step(k, h):
        t = tt - 1 - k
        h = a_ref[t] * h + b_ref[t]
        a_ref[t] = h
        return h

    h_ref[...] = lax.fori_loop(0, tt, step, h_ref[...])

    @pl.when(i >= ncb)
    def _():
        hb_ref[...] = a_ref[...].astype(hb_ref.dtype)


def _lru_fwd_kernel(x_ref, prev_ref, next_ref, w_ref, ba_ref, bx_ref, lam_ref, cw_ref, cb_ref,
                    hb_ref, gr_ref, gm_ref, bm_ref, wbr_ref,
                    o_ref, win_ref, a_ref, b_ref, h_ref, *, tt, ncb, nlb):
    i = pl.program_id(0)
    at_start = (i == 0) | (i == ncb)
    at_end = (i == ncb - 1) | (i == ncb + nlb - 1)

    @pl.when(i == 0)
    def _():
        h_ref[...] = jnp.zeros(h_ref.shape, F32)

    _lru_fill_window(win_ref, x_ref, prev_ref, next_ref, at_start, at_end, tt)
    _lru_gates(win_ref, w_ref, ba_ref, bx_ref, lam_ref, cw_ref, cb_ref, a_ref, b_ref, tt)

    def step(t, h):
        h = a_ref[t] * h + b_ref[t]
        a_ref[t] = h
        return h

    h_ref[...] = lax.fori_loop(0, tt, step, h_ref[...])

    @pl.when(i >= ncb)
    def _():
        bsz, c = h_ref.shape
        rows = tt * bsz
        a_lat = (a_ref[...] + hb_ref[...].astype(F32)) * jax.nn.gelu(gr_ref[...].astype(F32), approximate=True)
        proj = jnp.dot(a_lat.reshape(rows, c).astype(BF16), wbr_ref[...], preferred_element_type=F32)
        gate = _sigmoid(gm_ref[...].astype(F32).reshape(rows, -1) + bm_ref[...])
        o_ref[...] = (gate * proj).reshape(o_ref.shape).astype(o_ref.dtype)


def _lru_specs(tt, bsz, c, blk_of, ttot):
    half = tt // 2
    return [pl.BlockSpec((tt, bsz, c), lambda i: (blk_of(i), 0, 0)),
            pl.BlockSpec((2, bsz, c), lambda i: (jnp.maximum(blk_of(i) * half - 1, 0), 0, 0)),
            pl.BlockSpec((1, bsz, c), lambda i: (jnp.minimum(blk_of(i) * tt + tt, ttot - 1), 0, 0))]


def _lru(xr_all, grgm, p_f, p_b, conv_w, conv_b, b_merge_lru, w_br, t_ctx, tt):
    ttot, bsz, c = xr_all.shape
    ncb, nlb = t_ctx // tt, (ttot - t_ctx) // tt
    nblk = ncb + nlb
    d = w_br.shape[1]
    const2 = lambda i: (0, 0)
    const3 = lambda i: (0, 0, 0)
    par_specs = [pl.BlockSpec(p_f[0].shape, const3), pl.BlockSpec((1, c), const2), pl.BlockSpec((1, c), const2),
                 pl.BlockSpec((1, c), const2), pl.BlockSpec((CONV_W, c), const2), pl.BlockSpec((1, c), const2)]
    scratch = [pltpu.VMEM((tt + 3, bsz, c), F32), pltpu.VMEM((tt, bsz, c), F32),
               pltpu.VMEM((tt, bsz, c), F32), pltpu.VMEM((bsz, c), F32)]

    bwd_blk = lambda i: jnp.where(i < ncb, ncb - 1 - i, ncb + nlb - 1 - (i - ncb))
    hb = pl.pallas_call(
        functools.partial(_lru_bwd_kernel, tt=tt, ncb=ncb, nlb=nlb),
        out_shape=jax.ShapeDtypeStruct((nlb * tt, bsz, c), BF16),
        grid=(nblk,),
        in_specs=_lru_specs(tt, bsz, c, bwd_blk, ttot) + par_specs,
        out_specs=pl.BlockSpec((tt, bsz, c), lambda i: (jnp.where(i < ncb, nlb - 1, nblk - 1 - i), 0, 0)),
        scratch_shapes=scratch,
        compiler_params=_cparams("arbitrary"),
        name="lru_bwd",
    )(xr_all, xr_all, xr_all, *p_b, conv_w, conv_b.reshape(1, c))

    assert c == d
    lat = lambda i: (jnp.maximum(i - ncb, 0), 0, 0)
    lat1 = lambda i: (jnp.maximum(i - ncb, 0), 0, 1)
    return pl.pallas_call(
        functools.partial(_lru_fwd_kernel, tt=tt, ncb=ncb, nlb=nlb),
        out_shape=jax.ShapeDtypeStruct((nlb * tt, bsz, d), BF16),
        grid=(nblk,),
        in_specs=_lru_specs(tt, bsz, c, lambda i: i, ttot) + par_specs + [
            pl.BlockSpec((tt, bsz, c), lat), pl.BlockSpec((tt, bsz, c), lat), pl.BlockSpec((tt, bsz, d), lat1),
            pl.BlockSpec((1, d), const2), pl.BlockSpec((c, d), const2)],
        out_specs=pl.BlockSpec((tt, bsz, d), lat),
        scratch_shapes=scratch,
        compiler_params=_cparams("arbitrary"),
        name="lru_fwd",
    )(xr_all, xr_all, xr_all, *p_f, conv_w, conv_b.reshape(1, c), hb, grgm, grgm, b_merge_lru.reshape(1, d), w_br)


def _lru_params(wa, ba, wx, bx, lam, pack):
    nb, bw, _ = wa.shape
    per = pack // bw
    c = nb * bw

    def bd(w):
        w4 = w.reshape(nb // per, per, bw, bw)
        eye = jnp.eye(per, dtype=w.dtype)
        return jnp.einsum('gpde,pq->gpdqe', w4, eye).reshape(nb // per, pack, pack)

    w = jnp.concatenate([bd(wa), bd(wx)], axis=-1).astype(BF16)
    return w, ba.reshape(1, c), bx.reshape(1, c), lam.reshape(1, c)


def _hi_lo(v):
    hi = v.astype(BF16)
    lo = (v - hi.astype(F32)).astype(BF16)
    return jnp.concatenate([hi, lo], axis=-1)


def _ssd_kernel(xbc_ref, dt_ref, dtb_ref, alog_ref, e_ref, dsk_ref, h0_ref, *outs,
                reverse, with_y, add_skip, lane0, inner):
    if with_y:
        y_ref, hfin_ref, st_ref = outs
    else:
        hfin_ref, st_ref = outs
    i = pl.program_id(1)
    q = SSD_CHUNK
    n = SSD_STATE
    gw = inner // SSD_GROUPS
    hpg = gw // SSD_HEAD_DIM

    @pl.when(i == 0)
    def _():
        st_ref[...] = h0_ref[0]

    dt = _softplus(dt_ref[0] + dtb_ref[...])
    da = dt * (-jnp.exp(alog_ref[...]))
    ri = lax.broadcasted_iota(jnp.int32, (q, q), 0)
    ci = lax.broadcasted_iota(jnp.int32, (q, q), 1)
    tri = (ri <= ci) if reverse else (ri >= ci)
    acum = jnp.dot(tri.astype(F32), da, precision=HIGHEST, preferred_element_type=F32)
    a_tot = jnp.sum(da, axis=0, keepdims=True)
    w_state = jnp.exp(a_tot - acum) * dt
    e2 = e_ref[...]
    dec_x = jnp.dot(_hi_lo(jnp.broadcast_to(jnp.exp(a_tot), (8, LANES))), e2,
                    preferred_element_type=F32)[0:1, :]
    ws_x = jnp.dot(w_state.astype(BF16), e2[0:LANES, :], preferred_element_type=F32)
    head_of_lane = lax.broadcasted_iota(jnp.int32, (q, gw), 1) // SSD_HEAD_DIM

    def expand(v, g):
        l0 = lane0 + g * hpg
        out = jnp.broadcast_to(v[:, l0:l0 + 1], (q, gw))
        for r in range(1, hpg):
            out = jnp.where(head_of_lane == r, v[:, l0 + r:l0 + r + 1], out)
        return out

    if with_y:
        eacum = jnp.exp(acum)
        acum_t = acum.T
        dt_t = dt.T
        rb = lax.broadcasted_iota(jnp.int32, (hpg * q, gw), 0) // q
        lb = lax.broadcasted_iota(jnp.int32, (hpg * q, gw), 1) // SSD_HEAD_DIM
        bd_mask = rb == lb

    for g in range(SSD_GROUPS):
        lo = g * gw
        xg = xbc_ref[0, :, lo:lo + gw]
        bg = xbc_ref[0, :, inner + g * n:inner + (g + 1) * n]
        cg = xbc_ref[0, :, inner + (SSD_GROUPS + g) * n:inner + (SSD_GROUPS + g + 1) * n]
        xgf = xg.astype(F32)
        xw = (xgf * ws_x[:, lo:lo + gw]).astype(BF16)
        st = st_ref[g]
        upd = lax.dot_general(bg, xw, (((0,), (0,)), ((), ())), preferred_element_type=F32)
        st_ref[g] = dec_x[:, lo:lo + gw] * st + upd
        if with_y:
            y_off = jnp.dot(cg, st.astype(BF16), preferred_element_type=F32) * expand(eacum, g)
            cb = lax.dot_general(cg, bg, (((1,), (1,)), ((), ())), preferred_element_type=F32)
            ls = []
            for r in range(hpg):
                lane = lane0 + g * hpg + r
                seg = acum[:, lane:lane + 1] - acum_t[lane:lane + 1, :]
                l_h = cb * jnp.exp(jnp.where(tri, seg, NEG_BIG)) * dt_t[lane:lane + 1, :]
                ls.append(l_h.astype(BF16))
            lcat = jnp.concatenate(ls, axis=1)
            xbd = jnp.where(bd_mask, jnp.concatenate([xg] * hpg, axis=0), jnp.zeros((), BF16))
            y = y_off + jnp.dot(lcat, xbd, preferred_element_type=F32)
            if add_skip:
                y = y + dsk_ref[:, lo:lo + gw] * xgf
            y_ref[0, :, lo:lo + gw] = y.astype(y_ref.dtype)

    @pl.when(i == pl.num_programs(1) - 1)
    def _():
        hfin_ref[0] = st_ref[...]


def _ssd(xbc, dt_raw, dtb, alog, e2, dsk, h0, *, reverse, with_y, add_skip, lane0):
    bsz, s, width = xbc.shape
    inner = e2.shape[1]
    nc = s // SSD_CHUNK
    gw = inner // SSD_GROUPS
    cidx = (lambda b, i: (b, nc - 1 - i, 0)) if reverse else (lambda b, i: (b, i, 0))
    const2 = lambda b, i: (0, 0)
    st_spec = pl.BlockSpec((1, SSD_GROUPS, SSD_STATE, gw), lambda b, i: (b, 0, 0, 0))
    st_shape = jax.ShapeDtypeStruct((bsz, SSD_GROUPS, SSD_STATE, gw), F32)
    out_shape, out_specs = [st_shape], [st_spec]
    if with_y:
        out_shape = [jax.ShapeDtypeStruct((bsz, s, inner), BF16)] + out_shape
        out_specs = [pl.BlockSpec((1, SSD_CHUNK, inner), cidx)] + out_specs
    return pl.pallas_call(
        functools.partial(_ssd_kernel, reverse=reverse, with_y=with_y, add_skip=add_skip, lane0=lane0, inner=inner),
        out_shape=out_shape,
        grid=(bsz, nc),
        in_specs=[pl.BlockSpec((1, SSD_CHUNK, width), cidx),
                  pl.BlockSpec((1, SSD_CHUNK, LANES), cidx),
                  pl.BlockSpec((1, LANES), const2), pl.BlockSpec((1, LANES), const2),
                  pl.BlockSpec(e2.shape, const2), pl.BlockSpec((1, inner), const2), st_spec],
        out_specs=out_specs,
        scratch_shapes=[pltpu.VMEM((SSD_GROUPS, SSD_STATE, gw), F32)],
        compiler_params=_cparams("arbitrary", "arbitrary"),
        name="ssd_y" if with_y else "ssd_state",
    )(xbc, dt_raw, dtb, alog, e2, dsk, h0)


def _ssdbr_kernel(yf_ref, yb_ref, z_ref, gm_ref, nw_ref, bm_ref, gsum_ref, gexp_ref, w_ref, o_ref, *, group_w):
    y = (yf_ref[0].astype(F32) + yb_ref[0].astype(F32)) * _silu(z_ref[0].astype(F32))
    ms = jnp.dot((y * y).astype(BF16), gsum_ref[...], preferred_element_type=F32) * (1.0 / group_w)
    rs = lax.rsqrt(ms + RMS_EPS)
    rs_x = jnp.dot(_hi_lo(rs), gexp_ref[...], preferred_element_type=F32)
    yn = (y * rs_x * nw_ref[...]).astype(BF16)
    proj = jnp.dot(yn, w_ref[...], preferred_element_type=F32)
    o_ref[0] = (_sigmoid(gm_ref[0].astype(F32) + bm_ref[...]) * proj).astype(o_ref.dtype)


def _ssdbr(y_f, y_b, zgm, norm_w, b_merge_ssd, gsum, gexp, w_br, rows):
    bsz, s, inner = y_f.shape
    d = w_br.shape[1]
    assert inner % d == 0
    tok = lambda b, i: (b, i, 0)
    const2 = lambda b, i: (0, 0)
    return pl.pallas_call(
        functools.partial(_ssdbr_kernel, group_w=inner // SSD_GROUPS),
        out_shape=jax.ShapeDtypeStruct((bsz, s, d), BF16),
        grid=(bsz, s // rows),
        in_specs=[pl.BlockSpec((1, rows, inner), tok), pl.BlockSpec((1, rows, inner), tok),
                  pl.BlockSpec((1, rows, inner), tok),
                  pl.BlockSpec((1, rows, d), lambda b, i: (b, i, inner // d)),
                  pl.BlockSpec((1, inner), const2), pl.BlockSpec((1, d), const2),
                  pl.BlockSpec(gsum.shape, const2), pl.BlockSpec(gexp.shape, const2),
                  pl.BlockSpec((inner, d), const2)],
        out_specs=pl.BlockSpec((1, rows, d), tok),
        compiler_params=_cparams("arbitrary", "arbitrary"),
        name="ssdbr",
    )(y_f, y_b, zgm, zgm, norm_w.reshape(1, inner), b_merge_ssd.reshape(1, d), gsum, gexp, w_br)


def _pack_bf16_pair(lo, hi):
    lo_bits = lax.bitcast_convert_type(lo.astype(BF16).astype(F32), jnp.uint32)
    hi_bits = lax.bitcast_convert_type(hi.astype(BF16).astype(F32), jnp.uint32)
    return (lo_bits >> 16) | hi_bits


def _unpack_bf16_pair(p):
    lo = lax.bitcast_convert_type(p << 16, F32).astype(BF16)
    hi = lax.bitcast_convert_type(p & jnp.uint32(0xFFFF0000), F32).astype(BF16)
    return jnp.concatenate([lo, hi], axis=1)


def _out_kernel(ul_ref, us_ref, x_ref, g1_ref, sh2_ref, sc2_ref, wout_ref, lg_ref, lb_ref, rw_ref, rb_ref,
                x1_ref, hp_ref, gates_ref, sel_ref, *, alpha):
    u = (ul_ref[0].astype(F32) + us_ref[0].astype(F32)).astype(BF16)
    mix = jnp.dot(u, wout_ref[...], preferred_element_type=F32)
    x1 = _ln_rows(alpha * x_ref[0] + g1_ref[0] * mix) * lg_ref[...] + lb_ref[...]
    x1_ref[0] = x1
    h2 = _ln_rows(x1) * (1.0 + sc2_ref[0]) + sh2_ref[0]
    half = h2.shape[1] // 2
    hp_ref[0] = _pack_bf16_pair(h2[:, :half], h2[:, half:])
    h_hi = h2.astype(BF16)
    h_lo = (h2 - h_hi.astype(F32)).astype(BF16)
    logits = jnp.dot(jnp.concatenate([h_hi, h_lo, h_hi], axis=1), rw_ref[...],
                     preferred_element_type=F32) + rb_ref[...]
    lane = lax.broadcasted_iota(jnp.int32, logits.shape, 1)
    work = logits
    sel = jnp.zeros(logits.shape, jnp.bool_)
    top = None
    for k in range(MOE_TOP_K):
        m = jnp.max(work, axis=1, keepdims=True)
        if k == 0:
            top = m
        idx = jnp.min(jnp.where(work == m, lane, LANES), axis=1, keepdims=True)
        pick = lane == idx
        sel = sel | pick
        work = jnp.where(pick, 3.0 * NEG_BIG, work)
    e = jnp.where(sel, jnp.exp(logits - top), 0.0)
    gates_ref[0] = e / jnp.sum(e, axis=1, keepdims=True)
    sel_ref[0] = sel.astype(F32)


def _out(u_l, u_s, x, g1, sh2, sc2, w_out, ln_g, ln_b, rw, rb, rows, alpha):
    bsz, t, d = x.shape
    tok = lambda b, i: (b, i, 0)
    per_b = lambda b, i: (b, 0, 0)
    const2 = lambda b, i: (0, 0)
    return pl.pallas_call(
        functools.partial(_out_kernel, alpha=alpha),
        out_shape=[jax.ShapeDtypeStruct((bsz, t, d), F32), jax.ShapeDtypeStruct((bsz, t, d // 2), jnp.uint32),
                   jax.ShapeDtypeStruct((bsz, t, LANES), F32), jax.ShapeDtypeStruct((bsz, t, LANES), F32)],
        grid=(bsz, t // rows),
        in_specs=[pl.BlockSpec((1, rows, d), tok), pl.BlockSpec((1, rows, d), tok), pl.BlockSpec((1, rows, d), tok),
                  pl.BlockSpec((1, 1, d), per_b), pl.BlockSpec((1, 1, d), per_b), pl.BlockSpec((1, 1, d), per_b),
                  pl.BlockSpec((d, d), const2), pl.BlockSpec((1, d), const2), pl.BlockSpec((1, d), const2),
                  pl.BlockSpec((3 * d, LANES), const2), pl.BlockSpec((1, LANES), const2)],
        out_specs=[pl.BlockSpec((1, rows, d), tok), pl.BlockSpec((1, rows, d // 2), tok),
                   pl.BlockSpec((1, rows, LANES), tok), pl.BlockSpec((1, rows, LANES), tok)],
        compiler_params=_cparams("arbitrary", "arbitrary"),
        name="out",
    )(u_l, u_s, x, g1, sh2, sc2, w_out, ln_g.reshape(1, d), ln_b.reshape(1, d), rw, rb)


def _rank_kernel(sel_ref, rank_ref, cnt_ref, carry_ref):
    @pl.when(pl.program_id(0) == 0)
    def _():
        carry_ref[...] = jnp.zeros(carry_ref.shape, F32)

    s = sel_ref[...]
    tt = s.shape[0]
    ri = lax.broadcasted_iota(jnp.int32, (tt, tt), 0)
    ci = lax.broadcasted_iota(jnp.int32, (tt, tt), 1)
    earlier = (ri > ci).astype(BF16)
    rank_ref[...] = jnp.dot(earlier, s.astype(BF16), preferred_element_type=F32) + carry_ref[0:1, :]
    carry_ref[...] = carry_ref[...] + jnp.sum(s, axis=0, keepdims=True)
    cnt_ref[...] = carry_ref[...]


def _rank(sel, tt):
    n = sel.shape[0]
    return pl.pallas_call(
        _rank_kernel,
        out_shape=[jax.ShapeDtypeStruct((n, LANES), F32), jax.ShapeDtypeStruct((8, LANES), F32)],
        grid=(n // tt,),
        in_specs=[pl.BlockSpec((tt, LANES), lambda i: (i, 0))],
        out_specs=[pl.BlockSpec((tt, LANES), lambda i: (i, 0)), pl.BlockSpec((8, LANES), lambda i: (0, 0))],
        scratch_shapes=[pltpu.VMEM((8, LANES), F32)],
        compiler_params=_cparams("arbitrary"),
        name="moe_rank",
    )(sel)


def _pos_kernel(sel_ref, rank_ref, gates_ref, off_ref, pos_ref, w_ref):
    avail = sel_ref[...] > 0.5
    posf = off_ref[...] + rank_ref[...]
    gates = gates_ref[...]
    lane = lax.broadcasted_iota(jnp.int32, posf.shape, 1)
    cols_p = jnp.zeros(posf.shape, F32)
    cols_w = jnp.zeros(posf.shape, F32)
    for k in range(MOE_TOP_K):
        m = jnp.min(jnp.where(avail, lane, LANES), axis=1, keepdims=True)
        pick = lane == m
        cols_p = jnp.where(lane == k, jnp.sum(jnp.where(pick, posf, 0.0), axis=1, keepdims=True), cols_p)
        cols_w = jnp.where(lane == k, jnp.sum(jnp.where(pick, gates, 0.0), axis=1, keepdims=True), cols_w)
        avail = avail & jnp.logical_not(pick)
    w_ref[...] = cols_w
    pos_ref[...] = cols_p.astype(jnp.int32)


def _pos(sel, rank, gates, off, tt):
    n = sel.shape[0]
    tok = lambda i: (i, 0)
    return pl.pallas_call(
        _pos_kernel,
        out_shape=[jax.ShapeDtypeStruct((n, LANES), jnp.int32), jax.ShapeDtypeStruct((n, LANES), F32)],
        grid=(n // tt,),
        in_specs=[pl.BlockSpec((tt, LANES), tok), pl.BlockSpec((tt, LANES), tok), pl.BlockSpec((tt, LANES), tok),
                  pl.BlockSpec((1, LANES), lambda i: (0, 0))],
        out_specs=[pl.BlockSpec((tt, LANES), tok), pl.BlockSpec((tt, LANES), tok)],
        compiler_params=_cparams("arbitrary"),
        name="moe_pos",
    )(sel, rank, gates, off)


ROW_GROUP = 8
ROW_TILE = 8


def _dispatch_kernel(pos_ref, hp_ref, xs_init_hbm, xs_hbm, sem, *, tt):
    del xs_init_hbm

    def body(i, carry):
        row0 = pl.multiple_of(i * ROW_GROUP, ROW_GROUP)
        p0 = i * (ROW_GROUP * MOE_TOP_K)
        for u in range(ROW_GROUP):
            src = hp_ref.at[pl.ds(row0 + u, 1)]
            for k in range(MOE_TOP_K):
                dst = xs_hbm.at[pl.ds(pos_ref[p0 + u * MOE_TOP_K + k], 1)]
                pltpu.make_async_copy(src, dst, sem).start(priority=k % 2)
        return carry

    lax.fori_loop(0, tt // ROW_GROUP, body, 0)
    for k in range(MOE_TOP_K):
        pltpu.make_async_copy(hp_ref, xs_hbm.at[pl.ds(0, tt)], sem).wait()


def _dispatch(pos, hp, xs_init, tt):
    n, half = hp.shape
    any_spec = pl.BlockSpec(memory_space=pl.ANY)
    return pl.pallas_call(
        functools.partial(_dispatch_kernel, tt=tt),
        out_shape=jax.ShapeDtypeStruct(xs_init.shape, xs_init.dtype),
        grid=(n // tt,),
        in_specs=[pl.BlockSpec((tt * MOE_TOP_K,), lambda i: (i,), memory_space=pltpu.SMEM),
                  pl.BlockSpec((tt, half), lambda i: (i, 0)), any_spec],
        out_specs=any_spec,
        scratch_shapes=[pltpu.SemaphoreType.DMA(())],
        input_output_aliases={2: 0},
        compiler_params=pltpu.CompilerParams(dimension_semantics=("arbitrary",), has_side_effects=True),
        name="moe_dispatch",
    )(pos, hp, xs_init)


def _expert_kernel(te_ref, nv_ref, xs_ref, w1_ref, b1_ref, w2_ref, b2_ref, ys_ref, w1b_ref, w2b_ref, *, ff):
    t = pl.program_id(0)
    e = te_ref[t]
    prev = te_ref[jnp.maximum(t - 1, 0)]

    @pl.when((t == 0) | (e != prev))
    def _():
        w1b_ref[...] = w1_ref[0].astype(BF16)
        w2b_ref[...] = w2_ref[0].astype(BF16)

    @pl.when(t < nv_ref[0])
    def _():
        xrow = _unpack_bf16_pair(xs_ref[...])
        gu = jnp.dot(xrow, w1b_ref[...], preferred_element_type=F32) + b1_ref[0]
        g = jnp.minimum(gu[:, :ff], SWIGLU_LIMIT)
        u = jnp.clip(gu[:, ff:], -SWIGLU_LIMIT, SWIGLU_LIMIT)
        act = ((u + 1.0) * g * _sigmoid(SWIGLU_ALPHA * g)).astype(BF16)
        y = jnp.dot(act, w2b_ref[...], preferred_element_type=F32) + b2_ref[0]
        for g in range(y.shape[1] // LANES):
            ys_ref[pl.ds(g, y.shape[0], stride=ROW_TILE), :] = y[:, g * LANES:(g + 1) * LANES]

    @pl.when(t >= nv_ref[0])
    def _():
        ys_ref[...] = jnp.zeros(ys_ref.shape, F32)


def _experts(tile_expert, n_valid, xs, w1, b1, w2, b2, tm):
    rows, half = xs.shape
    n_exp, d, ff2 = w1.shape
    ff = ff2 // 2
    per_e = lambda t, te, nv: (te[t], 0, 0)
    assert d == ROW_TILE * LANES
    return pl.pallas_call(
        functools.partial(_expert_kernel, ff=ff),
        out_shape=jax.ShapeDtypeStruct((rows * ROW_TILE, LANES), F32),
        grid_spec=pltpu.PrefetchScalarGridSpec(
            num_scalar_prefetch=2,
            grid=(rows // tm,),
            in_specs=[pl.BlockSpec((tm, half), lambda t, te, nv: (t, 0)),
                      pl.BlockSpec((1, d, ff2), per_e), pl.BlockSpec((1, 1, ff2), per_e),
                      pl.BlockSpec((1, ff, d), per_e), pl.BlockSpec((1, 1, d), per_e)],
            out_specs=pl.BlockSpec((tm * ROW_TILE, LANES), lambda t, te, nv: (t, 0)),
            scratch_shapes=[pltpu.VMEM((d, ff2), BF16), pltpu.VMEM((ff, d), BF16)]),
        compiler_params=_cparams("arbitrary"),
        name="moe_experts",
    )(tile_expert, n_valid, xs, w1, b1.reshape(n_exp, 1, ff2), w2, b2.reshape(n_exp, 1, d))


def _combine_kernel(pos_ref, ys_hbm, ys_flat_hbm, w_ref, x1_ref, g2_ref, lg_ref, lb_ref, o_ref, buf_ref, sem,
                    *, tt, alpha):
    def body(i, carry):
        row0 = pl.multiple_of(i * ROW_GROUP, ROW_GROUP)
        p0 = i * (ROW_GROUP * MOE_TOP_K)
        for u in range(ROW_GROUP):
            for k in range(MOE_TOP_K):
                src = ys_hbm.at[pos_ref[p0 + u * MOE_TOP_K + k]]
                dst = buf_ref.at[pl.ds(pl.multiple_of((k * tt + row0 + u) * ROW_TILE, ROW_TILE), ROW_TILE)]
                pltpu.make_async_copy(src, dst, sem).start(priority=k % 2)
        return carry

    lax.fori_loop(0, tt // ROW_GROUP, body, 0)
    span = tt * ROW_TILE
    for k in range(MOE_TOP_K):
        pltpu.make_async_copy(ys_flat_hbm.at[pl.ds(0, span)], buf_ref.at[pl.ds(k * span, span)], sem).wait()

    chunk = 64

    def rows(c, carry):
        r0 = pl.multiple_of(c * chunk, chunk)
        r = pl.ds(r0, chunk)
        w = w_ref[r, :]

        def slot_rows(k):
            first = (k * tt + r0) * ROW_TILE
            return jnp.concatenate([buf_ref[pl.ds(first + g, chunk, stride=ROW_TILE), :] for g in range(ROW_TILE)],
                                   axis=1)

        acc = sum(w[:, k:k + 1] * slot_rows(k) for k in range(MOE_TOP_K))
        o_ref[0, r, :] = _ln_rows(alpha * x1_ref[0, r, :] + g2_ref[0] * acc) * lg_ref[...] + lb_ref[...]
        return carry

    lax.fori_loop(0, tt // chunk, rows, 0)


def _combine(pos, ys, w, x1, g2, ln_g, ln_b, tt, alpha):
    bsz, t, d = x1.shape
    nt = t // tt
    return pl.pallas_call(
        functools.partial(_combine_kernel, tt=tt, alpha=alpha),
        out_shape=jax.ShapeDtypeStruct((bsz, t, d), F32),
        grid=(bsz, nt),
        in_specs=[pl.BlockSpec((tt * MOE_TOP_K,), lambda b, i: (b * nt + i,), memory_space=pltpu.SMEM),
                  pl.BlockSpec(memory_space=pl.ANY), pl.BlockSpec(memory_space=pl.ANY),
                  pl.BlockSpec((tt, LANES), lambda b, i: (b * nt + i, 0)),
                  pl.BlockSpec((1, tt, d), lambda b, i: (b, i, 0)),
                  pl.BlockSpec((1, 1, d), lambda b, i: (b, 0, 0)),
                  pl.BlockSpec((1, d), lambda b, i: (0, 0)), pl.BlockSpec((1, d), lambda b, i: (0, 0))],
        out_specs=pl.BlockSpec((1, tt, d), lambda b, i: (b, i, 0)),
        scratch_shapes=[pltpu.VMEM((MOE_TOP_K * tt * ROW_TILE, LANES), F32), pltpu.SemaphoreType.DMA(())],
        compiler_params=_cparams("arbitrary", "arbitrary"),
        name="moe_combine",
    )(pos, ys.reshape(-1, ROW_TILE, LANES), ys, w, x1, g2, ln_g.reshape(1, d), ln_b.reshape(1, d))


def _moe(hp, gates, sel, x1, g2, w1, b1, w2, b2, ln_g, ln_b, alpha, tt, tm):
    bsz, t, d = x1.shape
    n = bsz * t
    n_exp = w1.shape[0]
    sel2, gates2 = sel.reshape(n, LANES), gates.reshape(n, LANES)
    rank, cnt = _rank(sel2, tt)
    counts = cnt[0, :n_exp].astype(jnp.int32)
    tiles_per = (counts + tm - 1) // tm
    tile_end = jnp.cumsum(tiles_per)
    off = jnp.pad(((tile_end - tiles_per) * tm).astype(F32).reshape(1, n_exp), ((0, 0), (0, LANES - n_exp)))
    n_tiles = (n * MOE_TOP_K) // tm + n_exp
    n_valid = tile_end[-1:]
    tile_id = jnp.minimum(jnp.arange(n_tiles), n_valid - 1)
    tile_expert = jnp.sum((tile_end[None, :] <= tile_id[:, None]).astype(jnp.int32), axis=1)
    tile_expert = jnp.minimum(tile_expert, n_exp - 1)
    pos_lanes, w = _pos(sel2, rank, gates2, off, tt)
    pos = pos_lanes[:, :MOE_TOP_K].reshape(n * MOE_TOP_K)
    xs = _dispatch(pos, hp.reshape(n, d // 2), jnp.zeros((n_tiles * tm, d // 2), jnp.uint32), tt)
    ys = _experts(tile_expert, n_valid.astype(jnp.int32), xs, w1, b1, w2, b2, tm)
    return _combine(pos, ys, w, x1, g2, ln_g, ln_b, tt, alpha)


def _to_cols(u, grid_rows):
    b, t, c = u.shape
    return u.reshape(b, grid_rows, GRID_W, c).transpose(0, 2, 1, 3).reshape(b, t, c)


def _from_cols(u, grid_rows):
    b, t, c = u.shape
    return u.reshape(b, GRID_W, grid_rows, c).transpose(0, 2, 1, 3).reshape(b, t, c)


def kernel(x, c, ctx, c_ctx, w_ada, b_ada, w_in, b_merge, conv_lru_w, conv_lru_b, lru_wa, lru_ba, lru_wx, lru_bx, lru_lambda, conv_ssd_w, conv_ssd_b, ssd_dt_bias, ssd_a_log, ssd_d, ssd_norm_w, w_br_lru, w_br_ssd, w_out, ln1_g, ln1_b, router_w, router_b, moe_w1, moe_b1, moe_w2, moe_b2, ln2_g, ln2_b):
    depth = w_ada.shape[0]
    assert depth == 1, "single-layer stack: the context tokens only supply scan states"
    bsz, t, d = x.shape
    t_ctx = ctx.shape[1]
    alpha = (2.0 * depth) ** 0.25
    grid_rows = t // GRID_W
    d_rnn = w_br_lru.shape[1]
    inner = w_br_ssd.shape[1]
    heads = inner // SSD_HEAD_DIM
    gn = SSD_GROUPS * SSD_STATE
    col_gr = d_rnn
    col_z = col_gr + d_rnn
    col_xbc = col_z + inner
    col_dt = col_xbc + inner + 2 * gn
    col_gm = col_dt + 2 * heads
    assert 2 * heads <= LANES and t_ctx % SSD_CHUNK == 0 and t % SSD_CHUNK == 0

    pad = (-(bsz + 1)) % 8
    c_all = jnp.concatenate([c, c_ctx[None, :], jnp.zeros((pad, d), F32)], axis=0)
    mod = _ada(c_all, w_ada[0], b_ada[0])
    sh1, sc1, g1, sh2, sc2, g2 = (mod[:bsz, k * d:(k + 1) * d] for k in range(6))
    csh1, csc1 = mod[bsz:bsz + 1, 0:d], mod[bsz:bsz + 1, d:2 * d]

    w_in_b = w_in[0].astype(BF16)

    tt = 32
    zero = lambda i: 0
    x_tm, ctx_tm = jnp.transpose(x, (1, 0, 2)), jnp.transpose(ctx, (1, 0, 2))
    sh_tm = jnp.stack([jnp.broadcast_to(csh1, (bsz, d)), sh1])
    sc_tm = jnp.stack([jnp.broadcast_to(csc1, (bsz, d)), sc1])
    seg = lambda i: jnp.where(i * tt >= t_ctx, 1, 0)
    xr_all = _inproj(x_tm, sh_tm, sc_tm, w_in_b[:, :col_gr], (tt, bsz), 1024, sel=seg, x3_head=ctx_tm)
    w_grgm = jnp.concatenate([w_in_b[:, col_gr:col_z], w_in_b[:, col_gm:col_gm + d]], axis=1)
    grgm = _inproj(x_tm, sh_tm[1:], sc_tm[1:], w_grgm, (tt, bsz), w_grgm.shape[1], sel=zero, out_dtype=BF16)
    p_f = _lru_params(lru_wa[0, 0], lru_ba[0, 0], lru_wx[0, 0], lru_bx[0, 0], lru_lambda[0, 0], 256)
    p_b = _lru_params(lru_wa[0, 1], lru_ba[0, 1], lru_wx[0, 1], lru_bx[0, 1], lru_lambda[0, 1], 256)
    u_lru_tm = _lru(xr_all, grgm, p_f, p_b, conv_lru_w[0], conv_lru_b[0], b_merge[0, :d],
                    w_br_lru[0].astype(BF16), t_ctx, tt)

    x_cm = _to_cols(x, grid_rows)
    sh_b, sc_b = sh1[:, None, :], sc1[:, None, :]
    csh_b, csc_b = csh1[None], csc1[None]
    w_xbc = w_in_b[:, col_xbc:col_dt]
    w_dt = jnp.pad(w_in_b[:, col_dt:col_gm], ((0, 0), (0, LANES - 2 * heads)))
    w_zgm = jnp.concatenate([w_in_b[:, col_z:col_xbc], w_in_b[:, col_gm + d:]], axis=1)
    xbc = _inproj(x_cm, sh_b, sc_b, w_xbc, (1, t), 512, conv_w=conv_ssd_w[0], conv_b=conv_ssd_b[0], out_dtype=BF16)
    xbc_c = _inproj(ctx, csh_b, csc_b, w_xbc, (1, t_ctx), 512, sel=zero, conv_w=conv_ssd_w[0],
                    conv_b=conv_ssd_b[0], out_dtype=BF16)
    dt_raw_c = _inproj(ctx, csh_b, csc_b, w_dt, (1, t_ctx), LANES, sel=zero)
    w_zgm_dt = jnp.concatenate([w_zgm, w_dt], axis=1)
    zgm, dt_raw = _inproj(x_cm, sh_b, sc_b, w_zgm_dt, (1, 512), w_zgm_dt.shape[1], out_dtype=BF16,
                          tail=(LANES, F32))

    lane_pad = LANES - 2 * heads
    dtb = jnp.pad(ssd_dt_bias[0].reshape(1, 2 * heads), ((0, 0), (0, lane_pad)))
    alog = jnp.pad(ssd_a_log[0].reshape(1, 2 * heads), ((0, 0), (0, lane_pad)))
    dsk = jnp.repeat(ssd_d[0], SSD_HEAD_DIM).reshape(1, inner)
    head_of_lane = jnp.arange(inner) // SSD_HEAD_DIM

    def expand(lane0):
        e = (jnp.arange(LANES)[:, None] == head_of_lane[None, :] + lane0).astype(BF16)
        return jnp.concatenate([e, e], axis=0)

    gw = inner // SSD_GROUPS
    s0 = jnp.zeros((bsz, SSD_GROUPS, SSD_STATE, gw), F32)
    common = (dtb, alog)
    (st_f,) = _ssd(xbc_c, dt_raw_c, *common, expand(0), dsk, s0, reverse=False, with_y=False, add_skip=False, lane0=0)
    (st_b,) = _ssd(xbc_c, dt_raw_c, *common, expand(heads), dsk, s0, reverse=True, with_y=False, add_skip=False,
                   lane0=heads)
    y_f, _ = _ssd(xbc, dt_raw, *common, expand(0), dsk, st_f, reverse=False, with_y=True, add_skip=True, lane0=0)
    y_b, _ = _ssd(xbc, dt_raw, *common, expand(heads), dsk, st_b, reverse=True, with_y=True, add_skip=False,
                  lane0=heads)

    group_of_lane = jnp.arange(inner) // gw
    gsum = (group_of_lane[:, None] == jnp.arange(LANES)[None, :]).astype(BF16)
    gexp = jnp.concatenate([gsum.T, gsum.T], axis=0)
    u_ssd_cm = _ssdbr(y_f, y_b, zgm, ssd_norm_w[0], b_merge[0, d:], gsum, gexp, w_br_ssd[0].astype(BF16), 256)

    u_lru = jnp.transpose(u_lru_tm, (1, 0, 2))
    u_ssd = _from_cols(u_ssd_cm, grid_rows)
    n_exp = router_w.shape[2]
    rw = jnp.pad(router_w[0], ((0, 0), (0, LANES - n_exp)))
    rw_hi = rw.astype(BF16)
    rw_lo = (rw - rw_hi.astype(F32)).astype(BF16)
    rw = jnp.concatenate([rw_hi, rw_hi, rw_lo], axis=0)
    rb = jnp.pad(router_b[0].reshape(1, n_exp), ((0, 0), (0, LANES - n_exp)), constant_values=NEG_BIG)
    x1, hp, gates, sel = _out(u_lru, u_ssd, x, g1[:, None, :], sh2[:, None, :], sc2[:, None, :],
                              w_out[0].astype(BF16), ln1_g[0], ln1_b[0], rw, rb, 512, alpha)

    return _moe(hp, gates, sel, x1, g2[:, None, :], moe_w1[0], moe_b1[0], moe_w2[0], moe_b2[0],
                ln2_g[0], ln2_b[0], alpha, 1024, 512)
```
